```python
import math
import jax
import jax.numpy as jnp
from jax import lax
import numpy as np

D_MODEL = 1024
BATCH = 8
SEQ = 2048
DEPTH = 2
DEC_BATCH = 128
DEC_SEQ = 1
PAST_LEN = 16384
PAGE_SIZE = 128

N_A_LAYERS = DEPTH // 2
N_B_LAYERS = DEPTH - N_A_LAYERS
A_HEADS = 8
A_DK = D_MODEL // A_HEADS
A_DV = D_MODEL // A_HEADS
A_CHUNK = 64
B_HEAD_DIM = 64
B_HEADS = D_MODEL // B_HEAD_DIM
B_KV_HEADS = 4
B_GROUP = B_HEADS // B_KV_HEADS
WINDOW = 128
BLOCK = 128
ATTN_SCALE = 1.0 / math.sqrt(B_HEAD_DIM)
N_BUCKETS = 32
MAX_DISTANCE = 128
D_FF = 4 * D_MODEL
EPS = 1e-6
NEG = -1e30

kernel_name = 'yoco_hgrn2_swa_sink_decoder'


def rmsnorm(x, g):
    xf = x.astype(jnp.float32)
    y = xf * lax.rsqrt(jnp.mean(xf * xf, axis=-1, keepdims=True) + EPS)
    return (y * g.astype(jnp.float32)).astype(x.dtype)


def sq_relu_mlp(h, w_up, w_down):
    u = jnp.maximum(h @ w_up, 0)
    return (u * u) @ w_down


def t5_bucket(dist):
    n = jnp.maximum(dist, 0)
    max_exact = N_BUCKETS // 2
    nf = jnp.maximum(n, 1).astype(jnp.float32)
    large = max_exact + (jnp.log(nf / max_exact) / math.log(MAX_DISTANCE / max_exact)
                         * (N_BUCKETS - max_exact)).astype(jnp.int32)
    large = jnp.minimum(large, N_BUCKETS - 1)
    return jnp.where(n < max_exact, n, large)


def t5_bias(dist, rel_bias):
    b = rel_bias[t5_bucket(dist)].astype(jnp.float32)
    q_len, k_len = dist.shape
    return jnp.transpose(b, (2, 0, 1)).reshape(B_KV_HEADS, B_GROUP, q_len, k_len)


def sink_attend(s, sink, v, eq):
    sink = sink.astype(jnp.float32)
    m = jnp.maximum(jnp.max(s, axis=-1, keepdims=True), sink)
    p = jnp.exp(s - m)
    denom = jnp.sum(p, axis=-1, keepdims=True) + jnp.exp(sink - m)
    return jnp.einsum(eq, (p / denom).astype(v.dtype), v)


def swa_prompt(q, k, v, sink, rel_bias):
    bsz, seq_len = q.shape[:2]
    nb = seq_len // BLOCK
    qb = q.reshape(bsz, nb, BLOCK, B_KV_HEADS, B_GROUP, B_HEAD_DIM)
    kb = k.reshape(bsz, nb, BLOCK, B_KV_HEADS, B_HEAD_DIM)
    vb = v.reshape(bsz, nb, BLOCK, B_KV_HEADS, B_HEAD_DIM)
    pad = ((0, 0), (1, 0), (0, 0), (0, 0), (0, 0))
    kk = jnp.concatenate([jnp.pad(kb, pad)[:, :-1], kb], axis=2)
    vv = jnp.concatenate([jnp.pad(vb, pad)[:, :-1], vb], axis=2)
    s = jnp.einsum('bnqhgd,bnkhd->bnhgqk', qb, kk).astype(jnp.float32) * ATTN_SCALE
    qi = jnp.arange(BLOCK)[:, None]
    kj = jnp.arange(2 * BLOCK)[None, :]
    dist = qi + BLOCK - kj
    valid = (dist >= 0) & (dist < WINDOW)
    valid = valid[None] & ((jnp.arange(nb)[:, None, None] > 0) | (kj >= BLOCK)[None])
    s = jnp.where(valid[None, :, None, None], s + t5_bias(dist, rel_bias), NEG)
    o = sink_attend(s, sink.reshape(B_KV_HEADS, B_GROUP)[:, :, None, None], vv,
                    'bnhgqk,bnkhd->bnqhgd')
    return o.reshape(bsz, seq_len, B_HEADS * B_HEAD_DIM)


def swa_sample(q, kk, vv, n_buf, sink, rel_bias):
    bsz, new_len = q.shape[:2]
    s = jnp.einsum('bqhgd,bkhd->bhgqk', q, kk).astype(jnp.float32) * ATTN_SCALE
    dist = (jnp.arange(new_len)[:, None] + n_buf) - jnp.arange(n_buf + new_len)[None, :]
    valid = (dist >= 0) & (dist < WINDOW)
    s = jnp.where(valid, s + t5_bias(dist, rel_bias), NEG)
    o = sink_attend(s, sink.reshape(B_KV_HEADS, B_GROUP)[:, :, None, None], vv,
                    'bhgqk,bkhd->bqhgd')
    return o.reshape(bsz, new_len, B_HEADS * B_HEAD_DIM)


def hgrn2_chunked(q, k, v, log_f, s0):
    bsz, seq_len = q.shape[:2]
    c = A_CHUNK
    n = seq_len // c

    def to_chunks(t):
        return t.reshape(bsz, n, c, t.shape[2], t.shape[3]).transpose(1, 0, 3, 2, 4)

    qc, kc, vc, fc = to_chunks(q), to_chunks(k), to_chunks(v), to_chunks(log_f)
    b = jnp.cumsum(fc, axis=3)
    b_ref = b[:, :, :, c // 2 - 1:c // 2]
    att = jnp.einsum('nbhtk,nbhsk->nbhts', qc * jnp.exp(b - b_ref), kc * jnp.exp(b_ref - b))
    att = jnp.where(jnp.tril(jnp.ones((c, c), dtype=bool)), att, 0.0)
    o_intra = jnp.einsum('nbhts,nbhsv->nbhtv', att, vc)
    b_last = b[:, :, :, -1:]
    q_inter = qc * jnp.exp(b)
    k_state = kc * jnp.exp(b_last - b)
    decay = jnp.exp(b_last[:, :, :, 0])

    def step(state, xs):
        qi, ks, vs, dl = xs
        o = jnp.einsum('bhtk,bhkv->bhtv', qi, state)
        state = dl[..., None] * state + jnp.einsum('bhsk,bhsv->bhkv', ks, vs)
        return state, o

    s_final, o_inter = lax.scan(step, s0, (q_inter, k_state, vc, decay))
    o = (o_intra + o_inter).transpose(1, 0, 3, 2, 4).reshape(bsz, seq_len, A_HEADS, A_DV)
    return o, s_final


def hgrn2_recurrent(q, k, v, log_f, s0):
    def step(state, xs):
        qt, kt, vt, ft = xs
        state = jnp.exp(ft)[..., None] * state + kt[..., :, None] * vt[..., None, :]
        return state, jnp.einsum('bhk,bhkv->bhv', qt, state)

    xs = tuple(jnp.swapaxes(t, 0, 1) for t in (q, k, v, log_f))
    s_final, o = lax.scan(step, s0, xs)
    return jnp.swapaxes(o, 0, 1), s_final


def hgrn2_mixer(h, s0, w_in, lb, g_norm, w_out, per_token):
    bsz, seq_len, _ = h.shape
    hk = A_HEADS * A_DK
    hv = A_HEADS * A_DV
    q, f, i, g = jnp.split(h @ w_in, [hk, 2 * hk, 2 * hk + hv], axis=-1)
    f32 = jnp.float32
    forget = lb + (1.0 - lb) * jax.nn.sigmoid(f.astype(f32))
    log_f = jnp.log(forget).reshape(bsz, seq_len, A_HEADS, A_DK)
    k = (1.0 - forget).reshape(bsz, seq_len, A_HEADS, A_DK)
    q = q.astype(f32).reshape(bsz, seq_len, A_HEADS, A_DK)
    v = i.astype(f32).reshape(bsz, seq_len, A_HEADS, A_DV)
    s0 = s0.astype(f32)
    if per_token:
        o, s_new = hgrn2_recurrent(q, k, v, log_f, s0)
    else:
        o, s_new = hgrn2_chunked(q, k, v, log_f, s0)
    o = rmsnorm(o, g_norm).reshape(bsz, seq_len, hv) * jax.nn.silu(g.astype(f32))
    return o.astype(h.dtype) @ w_out, s_new


def shared_kv(x, g_kv, w_kv):
    bsz, seq_len = x.shape[:2]
    k, v = jnp.split(rmsnorm(x, g_kv) @ w_kv, 2, axis=-1)
    shp = (bsz, seq_len, B_KV_HEADS, B_HEAD_DIM)
    return k.reshape(shp), v.reshape(shp)


def trunk(x, s_in, k_buf, v_buf, w_a_in, a_lb, a_gnorm, w_a_out, g_mix, g_mlp, g_kv, w_kv,
          w_b_q, b_sink, w_b_out, rel_bias, w_up, w_down, g_final, prompt):
    bsz, seq_len, _ = x.shape
    lb_all = jnp.cumsum(jax.nn.softmax(a_lb.astype(jnp.float32), axis=0), axis=0)
    a_states = []
    kk = vv = new_k = new_v = None
    n_buf = 0
    for layer in range(DEPTH):
        if layer < N_A_LAYERS:
            h = rmsnorm(x, g_mix[layer])
            if prompt:
                s0 = jnp.zeros((bsz, A_HEADS, A_DK, A_DV), jnp.float32)
            else:
                s0 = s_in[layer]
            o, s_new = hgrn2_mixer(h, s0, w_a_in[layer], lb_all[layer], a_gnorm[layer],
                                   w_a_out[layer], per_token=not prompt)
            a_states.append(s_new.astype(x.dtype))
            x = x + o
        else:
            j = layer - N_A_LAYERS
            if j == 0:
                k, v = shared_kv(x, g_kv, w_kv)
                if prompt:
                    kk, vv = k, v
                    w_keep = min(WINDOW, seq_len)
                    new_k, new_v = k[:, seq_len - w_keep:], v[:, seq_len - w_keep:]
                else:
                    n_buf = k_buf.shape[1]
                    kk = jnp.concatenate([k_buf.astype(k.dtype), k], axis=1)
                    vv = jnp.concatenate([v_buf.astype(v.dtype), v], axis=1)
                    new_k, new_v = kk[:, -n_buf:], vv[:, -n_buf:]
            h = rmsnorm(x, g_mix[layer])
            q = (h @ w_b_q[j]).reshape(bsz, seq_len, B_KV_HEADS, B_GROUP, B_HEAD_DIM)
            if prompt:
                a = swa_prompt(q, kk, vv, b_sink[j], rel_bias)
            else:
                a = swa_sample(q, kk, vv, n_buf, b_sink[j], rel_bias)
            x = x + a @ w_b_out[j]
        x = x + sq_relu_mlp(rmsnorm(x, g_mlp[layer]), w_up[layer], w_down[layer])
    return rmsnorm(x, g_final), jnp.stack(a_states), new_k, new_v


def setup_inputs(seed: int = 0) -> dict:
    key = jax.random.key(seed)
    ks = jax.random.split(key, 24)
    f32 = jnp.float32
    d = D_MODEL
    w_buf = min(WINDOW, PAST_LEN)

    def nrm(k, shape, scale):
        return jax.random.normal(k, shape, f32) * scale

    a_in_cols = 2 * A_HEADS * A_DK + 2 * A_HEADS * A_DV
    return {
        'x_prompt': nrm(ks[0], (BATCH, SEQ, d), 1.0),
        'x_sample': nrm(ks[1], (DEC_BATCH, DEC_SEQ, d), 1.0),
        'state_hgrn': nrm(ks[2], (N_A_LAYERS, DEC_BATCH, A_HEADS, A_DK, A_DV), 0.3),
        'cache_k_win': nrm(ks[3], (DEC_BATCH, w_buf, B_KV_HEADS, B_HEAD_DIM), 1.0),
        'cache_v_win': nrm(ks[4], (DEC_BATCH, w_buf, B_KV_HEADS, B_HEAD_DIM), 1.0),
        'w_a_in': nrm(ks[5], (N_A_LAYERS, d, a_in_cols), d ** -0.5),
        'a_lb': nrm(ks[6], (N_A_LAYERS + 1, A_HEADS * A_DK), 0.5),
        'a_gnorm': 1.0 + nrm(ks[7], (N_A_LAYERS, A_DV), 0.02),
        'w_a_out': nrm(ks[8], (N_A_LAYERS, A_HEADS * A_DV, d), (A_HEADS * A_DV) ** -0.5),
        'g_mix': 1.0 + nrm(ks[9], (DEPTH, d), 0.02),
        'g_mlp': 1.0 + nrm(ks[10], (DEPTH, d), 0.02),
        'g_kv': 1.0 + nrm(ks[11], (d,), 0.02),
        'w_kv': nrm(ks[12], (d, 2 * B_KV_HEADS * B_HEAD_DIM), d ** -0.5),
        'w_b_q': nrm(ks[13], (N_B_LAYERS, d, B_HEADS * B_HEAD_DIM), d ** -0.5),
        'b_sink': nrm(ks[14], (N_B_LAYERS, B_HEADS), 0.5),
        'w_b_out': nrm(ks[15], (N_B_LAYERS, B_HEADS * B_HEAD_DIM, d), (B_HEADS * B_HEAD_DIM) ** -0.5),
        'rel_bias': nrm(ks[16], (N_BUCKETS, B_HEADS), 0.5),
        'w_up': nrm(ks[17], (DEPTH, d, D_FF), d ** -0.5),
        'w_down': nrm(ks[18], (DEPTH, D_FF, d), D_FF ** -0.5),
        'g_final': 1.0 + nrm(ks[19], (d,), 0.02),
    }


def reference(x_prompt, x_sample, state_hgrn, cache_k_win, cache_v_win, w_a_in, a_lb, a_gnorm,
              w_a_out, g_mix, g_mlp, g_kv, w_kv, w_b_q, b_sink, w_b_out, rel_bias, w_up, w_down,
              g_final):
    y_prompt, s_prompt, k_prompt, v_prompt = trunk(
        x_prompt, None, None, None, w_a_in, a_lb, a_gnorm, w_a_out, g_mix, g_mlp, g_kv, w_kv,
        w_b_q, b_sink, w_b_out, rel_bias, w_up, w_down, g_final, prompt=True)
    y_sample, s_sample, k_sample, v_sample = trunk(
        x_sample, state_hgrn, cache_k_win, cache_v_win, w_a_in, a_lb, a_gnorm, w_a_out, g_mix,
        g_mlp, g_kv, w_kv, w_b_q, b_sink, w_b_out, rel_bias, w_up, w_down, g_final, prompt=False)
    return (y_prompt, y_sample, s_prompt, s_sample, k_prompt, v_prompt, k_sample, v_sample)
```

```python
import functools
import math

import jax
import jax.numpy as jnp
from jax import lax
from jax.experimental import pallas as pl
from jax.experimental.pallas import tpu as pltpu

F32 = jnp.float32
BF16 = jnp.bfloat16

EPS = 1e-6
NEG = -1e30
A_HEADS = 8
A_CHUNK = 64
B_HEAD_DIM = 64
B_KV_HEADS = 4
WINDOW = 128
N_BUCKETS = 32
MAX_DISTANCE = 128

V7X_VMEM_LIMIT_BYTES = 56 * 1024 * 1024


def _cparams(*semantics):
    return pltpu.CompilerParams(dimension_semantics=semantics,
                                vmem_limit_bytes=V7X_VMEM_LIMIT_BYTES)


def _resident(shape):
    nd = len(shape)
    return pl.BlockSpec(shape, lambda *_: (0,) * nd, pipeline_mode=pl.Buffered(1))


def _rms_scale(x):
    return x * lax.rsqrt(jnp.mean(x * x, axis=-1, keepdims=True) + EPS)


def _sigmoid(x):
    return 1.0 / (1.0 + jnp.exp(-x))


def _dot(a, b):
    return jnp.dot(a, b, preferred_element_type=F32)


def _dot_nt(a, b):
    return lax.dot_general(a, b, (((1,), (1,)), ((), ())), preferred_element_type=F32)


def _dot_tn(a, b):
    return lax.dot_general(a, b, (((0,), (0,)), ((), ())), preferred_element_type=F32)


def _lower_bound(a_lb, layer):
    m = jnp.max(a_lb, axis=0, keepdims=True)
    e = jnp.exp(a_lb - m)
    return jnp.sum(e[: layer + 1], axis=0, keepdims=True) / jnp.sum(e, axis=0, keepdims=True)


def _forget_gate(f_raw, lb):
    return lb + (1.0 - lb) * _sigmoid(f_raw)


def _chunk_cumsum(tri, x):
    hi = x.astype(BF16)
    r1 = x - hi.astype(F32)
    mid = r1.astype(BF16)
    lo = (r1 - mid.astype(F32)).astype(BF16)
    return _dot(tri, hi) + _dot(tri, mid) + _dot(tri, lo)


def _hgrn_prompt_kernel(layer, x_ref, w_in_ref, a_lb_ref, g_mix_ref, g_norm_ref, w_out_ref,
                        y_ref, s_ref, st_ref, proj_ref, o_ref):
    t = pl.program_id(1)
    tile, d = x_ref.shape[1], x_ref.shape[2]
    dk = d // A_HEADS
    c = A_CHUNK

    @pl.when(t == 0)
    def _():
        st_ref[...] = jnp.zeros_like(st_ref)

    x = x_ref[0]
    h = (_rms_scale(x) * g_mix_ref[...]).astype(BF16)
    proj_ref[...] = _dot(h, w_in_ref[...])

    lb = _lower_bound(a_lb_ref[...], layer)
    row = lax.broadcasted_iota(jnp.int32, (c, c), 0)
    col = lax.broadcasted_iota(jnp.int32, (c, c), 1)
    causal = row >= col
    tri = jnp.where(causal, 1.0, 0.0).astype(BF16)
    g_norm = g_norm_ref[...]

    for ci in range(tile // c):
        rows = slice(ci * c, (ci + 1) * c)
        forget = _forget_gate(proj_ref[rows, d:2 * d], lb)
        k = 1.0 - forget
        b = _chunk_cumsum(tri, jnp.log(forget))
        b_mid = b[c // 2 - 1:c // 2]
        b_last = b[c - 1:c]
        q = proj_ref[rows, 0:d]
        v = proj_ref[rows, 2 * d:3 * d].astype(BF16)
        q_intra = (q * jnp.exp(b - b_mid)).astype(BF16)
        k_intra = (k * jnp.exp(b_mid - b)).astype(BF16)
        q_inter = (q * jnp.exp(b)).astype(BF16)
        k_state = (k * jnp.exp(b_last - b)).astype(BF16)
        decay = jnp.exp(b_last)
        for hd in range(A_HEADS):
            ls = slice(hd * dk, (hd + 1) * dk)
            att = jnp.where(causal, _dot_nt(q_intra[:, ls], k_intra[:, ls]), 0.0)
            o = _dot(att.astype(BF16), v[:, ls])
            st = st_ref[hd]
            o = o + _dot_nt(q_inter[:, ls], st.astype(BF16))
            st_ref[hd] = st * decay[:, ls] + _dot_tn(v[:, ls], k_state[:, ls])
            o_ref[rows, ls] = _rms_scale(o) * g_norm

    g = proj_ref[:, 3 * d:4 * d]
    a = (o_ref[...] * (g * _sigmoid(g))).astype(BF16)
    y_ref[0] = x + _dot(a, w_out_ref[...])

    @pl.when(t == pl.num_programs(1) - 1)
    def _():
        for hd in range(A_HEADS):
            s_ref[0, hd] = st_ref[hd].T


def _hgrn_prompt(x, w_in, a_lb, g_mix, g_norm, w_out, layer, tile):
    bsz, seq, d = x.shape
    dk = d // A_HEADS
    tile = min(tile, seq)
    return pl.pallas_call(
        functools.partial(_hgrn_prompt_kernel, layer),
        grid=(bsz, seq // tile),
        in_specs=[
            pl.BlockSpec((1, tile, d), lambda b, t: (b, t, 0)),
            _resident(w_in.shape),
            _resident(a_lb.shape),
            _resident(g_mix.shape),
            _resident(g_norm.shape),
            _resident(w_out.shape),
        ],
        out_specs=[
            pl.BlockSpec((1, tile, d), lambda b, t: (b, t, 0)),
            pl.BlockSpec((1, A_HEADS, dk, dk), lambda b, t: (b, 0, 0, 0)),
        ],
        out_shape=[
            jax.ShapeDtypeStruct((bsz, seq, d), F32),
            jax.ShapeDtypeStruct((bsz, A_HEADS, dk, dk), F32),
        ],
        scratch_shapes=[
            pltpu.VMEM((A_HEADS, dk, dk), F32),
            pltpu.VMEM((tile, 4 * d), F32),
            pltpu.VMEM((tile, d), F32),
        ],
        compiler_params=_cparams("arbitrary", "arbitrary"),
        name="hgrn_prompt",
    )(x, w_in, a_lb, g_mix, g_norm, w_out)


def _mlp_kernel(ff_chunk, has_final, x_ref, g_ref, w_up_ref, w_down_ref, *rest):
    if has_final:
        g_final_ref, y_ref = rest
    else:
        (y_ref,) = rest
    x = x_ref[...]
    h = (_rms_scale(x) * g_ref[...]).astype(BF16)
    acc = x
    for j in range(w_up_ref.shape[1] // ff_chunk):
        cols = slice(j * ff_chunk, (j + 1) * ff_chunk)
        u = jnp.maximum(_dot(h, w_up_ref[:, cols]), 0.0)
        acc = acc + _dot((u * u).astype(BF16), w_down_ref[cols, :])
    if has_final:
        acc = _rms_scale(acc) * g_final_ref[...]
    y_ref[...] = acc


def _mlp(x, g, w_up, w_down, g_final, tile, ff_chunk=1024):
    m, d = x.shape
    tile = min(tile, m)
    has_final = g_final is not None
    in_specs = [
        pl.BlockSpec((tile, d), lambda i: (i, 0)),
        _resident(g.shape),
        _resident(w_up.shape),
        _resident(w_down.shape),
    ]
    args = [x, g, w_up, w_down]
    if has_final:
        in_specs.append(_resident(g_final.shape))
        args.append(g_final)
    return pl.pallas_call(
        functools.partial(_mlp_kernel, ff_chunk, has_final),
        grid=(m // tile,),
        in_specs=in_specs,
        out_specs=pl.BlockSpec((tile, d), lambda i: (i, 0)),
        out_shape=jax.ShapeDtypeStruct((m, d), F32),
        compiler_params=_cparams("arbitrary"),
        name="mlp_final" if has_final else "mlp",
    )(*args)


def _t5_bucket(dist):
    n = jnp.maximum(dist, 0)
    max_exact = N_BUCKETS // 2
    nf = jnp.maximum(n, 1).astype(F32)
    large = max_exact + (jnp.log(nf / max_exact) / math.log(MAX_DISTANCE / max_exact)
                         * (N_BUCKETS - max_exact)).astype(jnp.int32)
    large = jnp.minimum(large, N_BUCKETS - 1)
    return jnp.where(n < max_exact, n, large)


def _band_bias_kernel(rel_bias_ref, out_ref):
    hd = pl.program_id(0)
    w = out_ref.shape[1]
    qi = lax.broadcasted_iota(jnp.int32, (w, 2 * w), 0)
    kj = lax.broadcasted_iota(jnp.int32, (w, 2 * w), 1)
    dist = qi + w - kj
    bucket = _t5_bucket(dist)
    bias = jnp.zeros((w, 2 * w), F32)
    for bk in range(N_BUCKETS):
        bias = jnp.where(bucket == bk, rel_bias_ref[bk, hd], bias)
    out_ref[0] = jnp.where((dist >= 0) & (dist < WINDOW), bias, NEG)


def _band_bias(rel_bias):
    n_heads = rel_bias.shape[1]
    return pl.pallas_call(
        _band_bias_kernel,
        grid=(n_heads,),
        in_specs=[pl.BlockSpec(memory_space=pltpu.SMEM)],
        out_specs=pl.BlockSpec((1, WINDOW, 2 * WINDOW), lambda h: (h, 0, 0)),
        out_shape=jax.ShapeDtypeStruct((n_heads, WINDOW, 2 * WINDOW), F32),
        compiler_params=_cparams("arbitrary"),
        name="band_bias",
    )(rel_bias)


def _softmax_sink_pv(s, sink, v):
    m = jnp.maximum(jnp.max(s, axis=-1, keepdims=True), sink)
    p = jnp.exp(s - m)
    denom = jnp.sum(p, axis=-1, keepdims=True) + jnp.exp(sink - m)
    return _dot(p.astype(BF16), v) / denom


def _swa_prompt_kernel(x_ref, g_kv_ref, w_kv_ref, g_q_ref, w_q_ref, sink_ref, bias_ref, w_out_ref,
                       y_ref, k_out_ref, v_out_ref, kv_ref, q_ref, a_ref):
    t = pl.program_id(1)
    tile, d = x_ref.shape[1], x_ref.shape[2]
    w = WINDOW
    hd = B_HEAD_DIM
    n_heads = d // hd
    group = n_heads // B_KV_HEADS
    kvd = B_KV_HEADS * hd
    scale = 1.0 / math.sqrt(hd)

    @pl.when(t == 0)
    def _():
        kv_ref[0:w, :] = jnp.zeros((w, 2 * kvd), F32)

    @pl.when(t > 0)
    def _():
        kv_ref[0:w, :] = kv_ref[tile:tile + w, :]

    x = x_ref[0]
    xn = _rms_scale(x)
    kv = _dot((xn * g_kv_ref[...]).astype(BF16), w_kv_ref[...])
    kv_ref[w:w + tile, :] = kv
    q_ref[...] = (_dot((xn * g_q_ref[...]).astype(BF16), w_q_ref[...]) * scale).astype(BF16)

    @pl.when(t == pl.num_programs(1) - 1)
    def _():
        k_out_ref[0] = kv[tile - w:, 0:kvd]
        v_out_ref[0] = kv[tile - w:, kvd:2 * kvd]

    key_col = lax.broadcasted_iota(jnp.int32, (w, 2 * w), 1)
    for blk in range(tile // w):
        rows = slice(blk * w, (blk + 1) * w)
        kblk = kv_ref[blk * w:(blk + 2) * w, 0:kvd].astype(BF16)
        vblk = kv_ref[blk * w:(blk + 2) * w, kvd:2 * kvd].astype(BF16)
        for h in range(n_heads):
            kh = h // group
            s = _dot_nt(q_ref[rows, h * hd:(h + 1) * hd], kblk[:, kh * hd:(kh + 1) * hd])
            s = s + bias_ref[h]
            if blk == 0:
                s = jnp.where((key_col >= w) | (t > 0), s, NEG)
            a_ref[rows, h * hd:(h + 1) * hd] = _softmax_sink_pv(
                s, sink_ref[0, h], vblk[:, kh * hd:(kh + 1) * hd])

    y_ref[0] = x + _dot(a_ref[...].astype(BF16), w_out_ref[...])


def _swa_prompt(x, g_kv, w_kv, g_q, w_q, sink, band_bias, w_out, tile):
    bsz, seq, d = x.shape
    tile = min(tile, seq)
    kvd = w_kv.shape[1] // 2
    w = WINDOW
    return pl.pallas_call(
        _swa_prompt_kernel,
        grid=(bsz, seq // tile),
        in_specs=[
            pl.BlockSpec((1, tile, d), lambda b, t: (b, t, 0)),
            _resident(g_kv.shape),
            _resident(w_kv.shape),
            _resident(g_q.shape),
            _resident(w_q.shape),
            pl.BlockSpec(memory_space=pltpu.SMEM),
            _resident(band_bias.shape),
            _resident(w_out.shape),
        ],
        out_specs=[
            pl.BlockSpec((1, tile, d), lambda b, t: (b, t, 0)),
            pl.BlockSpec((1, w, kvd), lambda b, t: (b, 0, 0)),
            pl.BlockSpec((1, w, kvd), lambda b, t: (b, 0, 0)),
        ],
        out_shape=[
            jax.ShapeDtypeStruct((bsz, seq, d), F32),
            jax.ShapeDtypeStruct((bsz, w, kvd), F32),
            jax.ShapeDtypeStruct((bsz, w, kvd), F32),
        ],
        scratch_shapes=[
            pltpu.VMEM((tile + w, 2 * kvd), F32),
            pltpu.VMEM((tile, d), BF16),
            pltpu.VMEM((tile, d), F32),
        ],
        compiler_params=_cparams("arbitrary", "arbitrary"),
        name="swa_prompt",
    )(x, g_kv, w_kv, g_q, w_q, sink, band_bias, w_out)


def _hgrn_sample_gates_kernel(layer, x_ref, w_in_ref, a_lb_ref, g_mix_ref,
                              ft_ref, qt_ref, v_ref, g_ref):
    d = x_ref.shape[1]
    dk = d // A_HEADS
    h = (_rms_scale(x_ref[...]) * g_mix_ref[...]).astype(BF16)
    proj = _dot(h, w_in_ref[...])
    forget = _forget_gate(proj[:, d:2 * d], _lower_bound(a_lb_ref[...], layer))
    v_ref[...] = proj[:, 2 * d:3 * d]
    g_ref[...] = proj[:, 3 * d:4 * d]
    for hd in range(A_HEADS):
        ls = slice(hd * dk, (hd + 1) * dk)
        ft_ref[hd] = forget[:, ls].T
        qt_ref[hd] = proj[:, ls].T


def _hgrn_sample_gates(x, w_in, a_lb, g_mix, layer):
    m, d = x.shape
    dk = d // A_HEADS
    vm = pl.BlockSpec(memory_space=pltpu.VMEM)
    return pl.pallas_call(
        functools.partial(_hgrn_sample_gates_kernel, layer),
        in_specs=[vm, vm, vm, vm],
        out_specs=[vm, vm, vm, vm],
        out_shape=[
            jax.ShapeDtypeStruct((A_HEADS, dk, m), F32),
            jax.ShapeDtypeStruct((A_HEADS, dk, m), F32),
            jax.ShapeDtypeStruct((m, d), F32),
            jax.ShapeDtypeStruct((m, d), F32),
        ],
        compiler_params=pltpu.CompilerParams(vmem_limit_bytes=V7X_VMEM_LIMIT_BYTES),
        name="hgrn_sample_gates",
    )(x, w_in, a_lb, g_mix)


def _hgrn_sample_state_kernel(ft_ref, qt_ref, v_ref, s0_ref, s_ref, o_ref):
    i = pl.program_id(0)
    tb = s0_ref.shape[0]
    dk = s0_ref.shape[2]
    m = ft_ref.shape[2]
    lane = lax.broadcasted_iota(jnp.int32, (dk, m), 1)
    for tk in range(tb):
        pick = lane == i * tb + tk
        for hd in range(A_HEADS):
            ls = slice(hd * dk, (hd + 1) * dk)
            f = jnp.sum(jnp.where(pick, ft_ref[hd], 0.0), axis=1, keepdims=True)
            q = jnp.sum(jnp.where(pick, qt_ref[hd], 0.0), axis=1, keepdims=True)
            s_new = f * s0_ref[tk, hd] + (1.0 - f) * v_ref[tk:tk + 1, ls]
            s_ref[tk, hd] = s_new
            o_ref[tk:tk + 1, ls] = jnp.sum(q * s_new, axis=0, keepdims=True)


def _hgrn_sample_state(ft, qt, v, s0, tb):
    m, d = v.shape
    dk = d // A_HEADS
    return pl.pallas_call(
        _hgrn_sample_state_kernel,
        grid=(m // tb,),
        in_specs=[
            _resident(ft.shape),
            _resident(qt.shape),
            pl.BlockSpec((tb, d), lambda i: (i, 0)),
            pl.BlockSpec((tb, A_HEADS, dk, dk), lambda i: (i, 0, 0, 0)),
        ],
        out_specs=[
            pl.BlockSpec((tb, A_HEADS, dk, dk), lambda i: (i, 0, 0, 0)),
            pl.BlockSpec((tb, d), lambda i: (i, 0)),
        ],
        out_shape=[
            jax.ShapeDtypeStruct(s0.shape, F32),
            jax.ShapeDtypeStruct((m, d), F32),
        ],
        compiler_params=_cparams("arbitrary"),
        name="hgrn_sample_state",
    )(ft, qt, v, s0)


def _hgrn_sample_out_kernel(x_ref, o_ref, g_ref, g_norm_ref, w_out_ref, y_ref):
    d = x_ref.shape[1]
    dk = d // A_HEADS
    g = g_ref[...]
    gate = g * _sigmoid(g)
    parts = []
    for hd in range(A_HEADS):
        ls = slice(hd * dk, (hd + 1) * dk)
        parts.append(_rms_scale(o_ref[:, ls]) * g_norm_ref[...] * gate[:, ls])
    a = jnp.concatenate(parts, axis=1).astype(BF16)
    y_ref[...] = x_ref[...] + _dot(a, w_out_ref[...])


def _hgrn_sample_out(x, o, g, g_norm, w_out):
    vm = pl.BlockSpec(memory_space=pltpu.VMEM)
    return pl.pallas_call(
        _hgrn_sample_out_kernel,
        in_specs=[vm] * 5,
        out_specs=vm,
        out_shape=jax.ShapeDtypeStruct(x.shape, F32),
        compiler_params=pltpu.CompilerParams(vmem_limit_bytes=V7X_VMEM_LIMIT_BYTES),
        name="hgrn_sample_out",
    )(x, o, g, g_norm, w_out)


def _swa_sample_proj_kernel(x_ref, g_kv_ref, w_kv_ref, g_q_ref, w_qx_ref, kv_ref, qx_ref):
    scale = 1.0 / math.sqrt(B_HEAD_DIM)
    xn = _rms_scale(x_ref[...])
    kv_ref[...] = _dot((xn * g_kv_ref[...]).astype(BF16), w_kv_ref[...])
    qx_ref[...] = (_dot((xn * g_q_ref[...]).astype(BF16), w_qx_ref[...]) * scale).astype(BF16)


def _swa_sample_proj(x, g_kv, w_kv, g_q, w_qx):
    m = x.shape[0]
    vm = pl.BlockSpec(memory_space=pltpu.VMEM)
    return pl.pallas_call(
        _swa_sample_proj_kernel,
        in_specs=[vm] * 5,
        out_specs=[vm, vm],
        out_shape=[
            jax.ShapeDtypeStruct((m, w_kv.shape[1]), F32),
            jax.ShapeDtypeStruct((m, w_qx.shape[1]), BF16),
        ],
        compiler_params=pltpu.CompilerParams(vmem_limit_bytes=V7X_VMEM_LIMIT_BYTES),
        name="swa_sample_proj",
    )(x, g_kv, w_kv, g_q, w_qx)


def _swa_sample_attend_kernel(qx_ref, k_new_ref, v_new_ref, k_buf_ref, v_buf_ref, bias_ref,
                              sink_ref, k_out_ref, v_out_ref, r_ref):
    n_buf = k_buf_ref.shape[1]
    k_out_ref[:, 0:n_buf - 1, :] = k_buf_ref[:, 1:n_buf, :]
    v_out_ref[:, 0:n_buf - 1, :] = v_buf_ref[:, 1:n_buf, :]
    k_out_ref[:, n_buf - 1:n_buf, :] = k_new_ref[...]
    v_out_ref[:, n_buf - 1:n_buf, :] = v_new_ref[...]
    k = k_out_ref[...].astype(BF16)
    v = v_out_ref[...].astype(BF16)
    s = jnp.einsum("bhc,brc->bhr", qx_ref[...], k, preferred_element_type=F32)
    s = s + bias_ref[...][None]
    sink = sink_ref[...][None]
    m = jnp.maximum(jnp.max(s, axis=-1, keepdims=True), sink)
    p = jnp.exp(s - m)
    denom = jnp.sum(p, axis=-1, keepdims=True) + jnp.exp(sink - m)
    r = jnp.einsum("bhr,brc->bhc", p.astype(BF16), v, preferred_element_type=F32)
    r_ref[...] = (r / denom).astype(BF16)


def _swa_sample_attend(qx, k_new, v_new, k_buf, v_buf, bias_row, sink_col, tb):
    m, n_buf, kvd = k_buf.shape
    n_heads = qx.shape[1]
    blk3 = lambda i: (i, 0, 0)
    return pl.pallas_call(
        _swa_sample_attend_kernel,
        grid=(m // tb,),
        in_specs=[
            pl.BlockSpec((tb, n_heads, kvd), blk3),
            pl.BlockSpec((tb, 1, kvd), blk3),
            pl.BlockSpec((tb, 1, kvd), blk3),
            pl.BlockSpec((tb, n_buf, kvd), blk3),
            pl.BlockSpec((tb, n_buf, kvd), blk3),
            _resident(bias_row.shape),
            _resident(sink_col.shape),
        ],
        out_specs=[
            pl.BlockSpec((tb, n_buf, kvd), blk3),
            pl.BlockSpec((tb, n_buf, kvd), blk3),
            pl.BlockSpec((tb, n_heads, kvd), blk3),
        ],
        out_shape=[
            jax.ShapeDtypeStruct((m, n_buf, kvd), F32),
            jax.ShapeDtypeStruct((m, n_buf, kvd), F32),
            jax.ShapeDtypeStruct((m, n_heads, kvd), BF16),
        ],
        compiler_params=_cparams("arbitrary"),
        name="swa_sample_attend",
    )(qx, k_new, v_new, k_buf, v_buf, bias_row, sink_col)


def _swa_sample_out_kernel(x_ref, r_ref, w_ox_ref, y_ref):
    y_ref[...] = x_ref[...] + _dot(r_ref[...], w_ox_ref[...])


def _swa_sample_out(x, r_flat, w_ox):
    vm = pl.BlockSpec(memory_space=pltpu.VMEM)
    return pl.pallas_call(
        _swa_sample_out_kernel,
        in_specs=[vm] * 3,
        out_specs=vm,
        out_shape=jax.ShapeDtypeStruct(x.shape, F32),
        compiler_params=pltpu.CompilerParams(vmem_limit_bytes=V7X_VMEM_LIMIT_BYTES),
        name="swa_sample_out",
    )(x, r_flat, w_ox)


def _expand_heads(w_q, w_out):
    d = w_q.shape[0]
    hd = B_HEAD_DIM
    n_heads = w_q.shape[1] // hd
    group = n_heads // B_KV_HEADS
    slot = (jnp.arange(n_heads) // group)[:, None] == jnp.arange(B_KV_HEADS)[None, :]
    w_qx = jnp.where(slot[None, :, :, None], w_q.reshape(d, n_heads, 1, hd), 0)
    w_ox = jnp.where(slot[:, :, None, None], w_out.reshape(n_heads, 1, hd, -1), 0)
    kvd = B_KV_HEADS * hd
    return w_qx.reshape(d, n_heads * kvd), w_ox.reshape(n_heads * kvd, -1)


def kernel(x_prompt, x_sample, state_hgrn, cache_k_win, cache_v_win, w_a_in, a_lb, a_gnorm, w_a_out,
           g_mix, g_mlp, g_kv, w_kv, w_b_q, b_sink, w_b_out, rel_bias, w_up, w_down, g_final):
    bsz, seq, d = x_prompt.shape
    n_dec = x_sample.shape[0]
    assert x_sample.shape[1] == 1, "the sample group decodes one token per sequence"
    n_a = w_a_in.shape[0]
    n_b = w_b_q.shape[0]
    assert n_a == 1 and n_b == 1, "depth-2 trunk: one HGRN2 layer, then one attention layer"
    n_buf = cache_k_win.shape[1]
    assert n_buf == WINDOW and seq % WINDOW == 0
    kvd = B_KV_HEADS * B_HEAD_DIM
    n_heads = d // B_HEAD_DIM

    bf = lambda w: w.astype(BF16)
    row = lambda g: g.reshape(1, -1)
    w_in, w_ao = bf(w_a_in[0]), bf(w_a_out[0])
    w_kvb, w_q, w_bo = bf(w_kv), bf(w_b_q[0]), bf(w_b_out[0])
    w_qx, w_ox = _expand_heads(w_q, w_bo)
    w_up_b, w_down_b = bf(w_up), bf(w_down)
    band_bias = _band_bias(rel_bias)
    sink = b_sink[0].reshape(1, n_heads)

    x, s_prompt = _hgrn_prompt(x_prompt, w_in, a_lb, row(g_mix[0]), row(a_gnorm[0]), w_ao,
                               layer=0, tile=256)
    x = _mlp(x.reshape(bsz * seq, d), row(g_mlp[0]), w_up_b[0], w_down_b[0], None, tile=512)
    x, k_prompt, v_prompt = _swa_prompt(x.reshape(bsz, seq, d), row(g_kv), w_kvb, row(g_mix[1]),
                                        w_q, sink, band_bias, w_bo, tile=512)
    y_prompt = _mlp(x.reshape(bsz * seq, d), row(g_mlp[1]), w_up_b[1], w_down_b[1],
                    row(g_final), tile=512).reshape(bsz, seq, d)

    xs = x_sample.reshape(n_dec, d)
    ft, qt, v, g = _hgrn_sample_gates(xs, w_in, a_lb, row(g_mix[0]), layer=0)
    s_sample, o = _hgrn_sample_state(ft, qt, v, state_hgrn[0], tb=8)
    xs = _hgrn_sample_out(xs, o, g, row(a_gnorm[0]), w_ao)
    xs = _mlp(xs, row(g_mlp[0]), w_up_b[0], w_down_b[0], None, tile=n_dec)
    kv_new, qx = _swa_sample_proj(xs, row(g_kv), w_kvb, row(g_mix[1]), w_qx)
    bias_row = band_bias[:, 0, 1:n_buf + 1]
    k_sample, v_sample, r = _swa_sample_attend(
        qx.reshape(n_dec, n_heads, kvd),
        kv_new[:, :kvd].reshape(n_dec, 1, kvd), kv_new[:, kvd:].reshape(n_dec, 1, kvd),
        cache_k_win.reshape(n_dec, n_buf, kvd), cache_v_win.reshape(n_dec, n_buf, kvd),
        bias_row, sink.reshape(n_heads, 1), tb=8)
    xs = _swa_sample_out(xs, r.reshape(n_dec, n_heads * kvd), w_ox)
    y_sample = _mlp(xs, row(g_mlp[1]), w_up_b[1], w_down_b[1], row(g_final),
                    tile=n_dec).reshape(n_dec, 1, d)

    cache_shape = (-1, WINDOW, B_KV_HEADS, B_HEAD_DIM)
    return (y_prompt, y_sample, s_prompt[None], s_sample[None],
            k_prompt.reshape(cache_shape), v_prompt.reshape(cache_shape),
            k_sample.reshape(cache_shape), v_sample.reshape(cache_shape))
```

```python
import functools
import math

import jax
import jax.numpy as jnp
from jax import lax
from jax.experimental import pallas as pl
from jax.experimental.pallas import tpu as pltpu

F32 = jnp.float32
BF16 = jnp.bfloat16

EPS = 1e-6
NEG = -1e30
A_HEADS = 8
A_CHUNK = 64
B_HEAD_DIM = 64
B_KV_HEADS = 4
WINDOW = 128
N_BUCKETS = 32
MAX_DISTANCE = 128

V7X_VMEM_LIMIT_BYTES = 56 * 1024 * 1024


def _cparams(*semantics):
    return pltpu.CompilerParams(dimension_semantics=semantics,
                                vmem_limit_bytes=V7X_VMEM_LIMIT_BYTES)


def _resident(shape):
    nd = len(shape)
    return pl.BlockSpec(shape, lambda *_: (0,) * nd, pipeline_mode=pl.Buffered(1))


def _rms_scale(x):
    return x * lax.rsqrt(jnp.mean(x * x, axis=-1, keepdims=True) + EPS)


def _sigmoid(x):
    return 1.0 / (1.0 + jnp.exp(-x))


def _dot(a, b):
    return jnp.dot(a, b, preferred_element_type=F32)


def _dot_nt(a, b):
    return lax.dot_general(a, b, (((1,), (1,)), ((), ())), preferred_element_type=F32)


def _dot_tn(a, b):
    return lax.dot_general(a, b, (((0,), (0,)), ((), ())), preferred_element_type=F32)


def _lower_bound(a_lb, layer):
    m = jnp.max(a_lb, axis=0, keepdims=True)
    e = jnp.exp(a_lb - m)
    return jnp.sum(e[: layer + 1], axis=0, keepdims=True) / jnp.sum(e, axis=0, keepdims=True)


def _forget_gate(f_raw, lb):
    return lb + (1.0 - lb) * _sigmoid(f_raw)


def _chunk_cumsum(tri, x):
    hi = x.astype(BF16)
    r1 = x - hi.astype(F32)
    mid = r1.astype(BF16)
    lo = (r1 - mid.astype(F32)).astype(BF16)
    return _dot(tri, hi) + _dot(tri, mid) + _dot(tri, lo)


def _hgrn_prompt_kernel(layer, x_ref, w_in_ref, a_lb_ref, g_mix_ref, g_norm_ref, w_out_ref,
                        y_ref, s_ref, st_ref, proj_ref, o_ref):
    t = pl.program_id(1)
    tile, d = x_ref.shape[1], x_ref.shape[2]
    dk = d // A_HEADS
    c = A_CHUNK

    @pl.when(t == 0)
    def _():
        st_ref[...] = jnp.zeros_like(st_ref)

    x = x_ref[0]
    h = (_rms_scale(x) * g_mix_ref[...]).astype(BF16)
    proj_ref[...] = _dot(h, w_in_ref[...])

    lb = _lower_bound(a_lb_ref[...], layer)
    row = lax.broadcasted_iota(jnp.int32, (c, c), 0)
    col = lax.broadcasted_iota(jnp.int32, (c, c), 1)
    causal = row >= col
    tri = jnp.where(causal, 1.0, 0.0).astype(BF16)
    g_norm = g_norm_ref[...]

    for ci in range(tile // c):
        rows = slice(ci * c, (ci + 1) * c)
        forget = _forget_gate(proj_ref[rows, d:2 * d], lb)
        k = 1.0 - forget
        b = _chunk_cumsum(tri, jnp.log(forget))
        b_mid = b[c // 2 - 1:c // 2]
        b_last = b[c - 1:c]
        q = proj_ref[rows, 0:d]
        v = proj_ref[rows, 2 * d:3 * d].astype(BF16)
        q_intra = (q * jnp.exp(b - b_mid)).astype(BF16)
        k_intra = (k * jnp.exp(b_mid - b)).astype(BF16)
        q_inter = (q * jnp.exp(b)).astype(BF16)
        k_state = (k * jnp.exp(b_last - b)).astype(BF16)
        decay = jnp.exp(b_last)
        for hd in range(A_HEADS):
            ls = slice(hd * dk, (hd + 1) * dk)
            att = jnp.where(causal, _dot_nt(q_intra[:, ls], k_intra[:, ls]), 0.0)
            o = _dot(att.astype(BF16), v[:, ls])
            st = st_ref[hd]
            o = o + _dot_nt(q_inter[:, ls], st.astype(BF16))
            st_ref[hd] = st * decay[:, ls] + _dot_tn(v[:, ls], k_state[:, ls])
            o_ref[rows, ls] = _rms_scale(o) * g_norm

    g = proj_ref[:, 3 * d:4 * d]
    a = (o_ref[...] * (g * _sigmoid(g))).astype(BF16)
    y_ref[0] = x + _dot(a, w_out_ref[...])

    @pl.when(t == pl.num_programs(1) - 1)
    def _():
        for hd in range(A_HEADS):
            s_ref[0, hd] = st_ref[hd].T


def _hgrn_prompt(x, w_in, a_lb, g_mix, g_norm, w_out, layer, tile):
    bsz, seq, d = x.shape
    dk = d // A_HEADS
    tile = min(tile, seq)
    return pl.pallas_call(
        functools.partial(_hgrn_prompt_kernel, layer),
        grid=(bsz, seq // tile),
        in_specs=[
            pl.BlockSpec((1, tile, d), lambda b, t: (b, t, 0)),
            _resident(w_in.shape),
            _resident(a_lb.shape),
            _resident(g_mix.shape),
            _resident(g_norm.shape),
            _resident(w_out.shape),
        ],
        out_specs=[
            pl.BlockSpec((1, tile, d), lambda b, t: (b, t, 0)),
            pl.BlockSpec((1, A_HEADS, dk, dk), lambda b, t: (b, 0, 0, 0)),
        ],
        out_shape=[
            jax.ShapeDtypeStruct((bsz, seq, d), F32),
            jax.ShapeDtypeStruct((bsz, A_HEADS, dk, dk), F32),
        ],
        scratch_shapes=[
            pltpu.VMEM((A_HEADS, dk, dk), F32),
            pltpu.VMEM((tile, 4 * d), F32),
            pltpu.VMEM((tile, d), F32),
        ],
        compiler_params=_cparams("arbitrary", "arbitrary"),
        name="hgrn_prompt",
    )(x, w_in, a_lb, g_mix, g_norm, w_out)


def _mlp_kernel(ff_chunk, has_final, x_ref, g_ref, w_up_ref, w_down_ref, *rest):
    if has_final:
        g_final_ref, y_ref = rest
    else:
        (y_ref,) = rest
    x = x_ref[...]
    h = (_rms_scale(x) * g_ref[...]).astype(BF16)
    acc = x
    for j in range(w_up_ref.shape[1] // ff_chunk):
        cols = slice(j * ff_chunk, (j + 1) * ff_chunk)
        u = jnp.maximum(_dot(h, w_up_ref[:, cols]), 0.0)
        acc = acc + _dot((u * u).astype(BF16), w_down_ref[cols, :])
    if has_final:
        acc = _rms_scale(acc) * g_final_ref[...]
    y_ref[...] = acc


def _mlp(x, g, w_up, w_down, g_final, tile, ff_chunk=1024):
    m, d = x.shape
    tile = min(tile, m)
    has_final = g_final is not None
    in_specs = [
        pl.BlockSpec((tile, d), lambda i: (i, 0)),
        _resident(g.shape),
        _resident(w_up.shape),
        _resident(w_down.shape),
    ]
    args = [x, g, w_up, w_down]
    if has_final:
        in_specs.append(_resident(g_final.shape))
        args.append(g_final)
    return pl.pallas_call(
        functools.partial(_mlp_kernel, ff_chunk, has_final),
        grid=(m // tile,),
        in_specs=in_specs,
        out_specs=pl.BlockSpec((tile, d), lambda i: (i, 0)),
        out_shape=jax.ShapeDtypeStruct((m, d), F32),
        compiler_params=_cparams("arbitrary"),
        name="mlp_final" if has_final else "mlp",
    )(*args)


def _t5_bucket(dist):
    n = jnp.maximum(dist, 0)
    max_exact = N_BUCKETS // 2
    nf = jnp.maximum(n, 1).astype(F32)
    large = max_exact + (jnp.log(nf / max_exact) / math.log(MAX_DISTANCE / max_exact)
                         * (N_BUCKETS - max_exact)).astype(jnp.int32)
    large = jnp.minimum(large, N_BUCKETS - 1)
    return jnp.where(n < max_exact, n, large)


def _band_bias_kernel(rel_bias_ref, out_ref):
    hd = pl.program_id(0)
    w = out_ref.shape[2]
    kj = lax.broadcasted_iota(jnp.int32, (2 * w, w), 0)
    qi = lax.broadcasted_iota(jnp.int32, (2 * w, w), 1)
    dist = qi + w - kj
    bucket = _t5_bucket(dist)
    bias = jnp.zeros((2 * w, w), F32)
    for bk in range(N_BUCKETS):
        bias = jnp.where(bucket == bk, rel_bias_ref[bk, hd], bias)
    out_ref[0] = jnp.where((dist >= 0) & (dist < WINDOW), bias, NEG)


def _band_bias(rel_bias):
    n_heads = rel_bias.shape[1]
    return pl.pallas_call(
        _band_bias_kernel,
        grid=(n_heads,),
        in_specs=[pl.BlockSpec(memory_space=pltpu.SMEM)],
        out_specs=pl.BlockSpec((1, 2 * WINDOW, WINDOW), lambda h: (h, 0, 0)),
        out_shape=jax.ShapeDtypeStruct((n_heads, 2 * WINDOW, WINDOW), F32),
        compiler_params=_cparams("arbitrary"),
        name="band_bias",
    )(rel_bias)


def _col_max(x):
    return jnp.max(x, axis=0, keepdims=True)


def _col_sum(x):
    return jnp.sum(x, axis=0, keepdims=True)


def _swa_prompt_kernel(x_ref, g_kv_ref, w_kv_ref, g_q_ref, w_qt_ref, sink_ref, bias_ref, w_out_ref,
                       y_ref, k_out_ref, v_out_ref, k_ref, vt_ref, qt_ref, at_ref, s_ref, p_ref):
    t = pl.program_id(1)
    tile, d = x_ref.shape[1], x_ref.shape[2]
    w = WINDOW
    hd = B_HEAD_DIM
    n_heads = d // hd
    group = n_heads // B_KV_HEADS
    kvd = B_KV_HEADS * hd
    scale = 1.0 / math.sqrt(hd)

    @pl.when(t == 0)
    def _():
        k_ref[:, 0:w, :] = jnp.zeros((B_KV_HEADS, w, hd), BF16)
        vt_ref[:, 0:w] = jnp.zeros((kvd, w), BF16)

    @pl.when(t > 0)
    def _():
        k_ref[:, 0:w, :] = k_ref[:, tile:tile + w, :]
        vt_ref[:, 0:w] = vt_ref[:, tile:tile + w]

    x = x_ref[0]
    xn = _rms_scale(x)
    kv = _dot((xn * g_kv_ref[...]).astype(BF16), w_kv_ref[...])
    for kh in range(B_KV_HEADS):
        k_ref[kh, w:w + tile, :] = kv[:, kh * hd:(kh + 1) * hd].astype(BF16)
    vt_ref[:, w:w + tile] = kv[:, kvd:2 * kvd].T.astype(BF16)
    h_q = (xn * g_q_ref[...]).astype(BF16)
    qt_ref[...] = (_dot_nt(w_qt_ref[...], h_q) * scale).astype(BF16)

    @pl.when(t == pl.num_programs(1) - 1)
    def _():
        k_out_ref[0] = kv[tile - w:, 0:kvd]
        v_out_ref[0] = kv[tile - w:, kvd:2 * kvd]

    first = t == 0
    key_row = lax.broadcasted_iota(jnp.int32, (2 * w, group * w), 0)
    rc = 16
    unit = 0
    for blk in range(tile // w):
        cols = slice(blk * w, (blk + 1) * w)
        keys = slice(blk * w, (blk + 2) * w)
        for kh in range(B_KV_HEADS):
            heads = range(kh * group, (kh + 1) * group)
            s_buf = s_ref.at[unit % 2]
            p_buf = p_ref.at[unit % 2]
            unit += 1
            q4 = jnp.concatenate([qt_ref[h * hd:(h + 1) * hd, cols] for h in heads], axis=1)
            s = _dot(k_ref[kh, keys, :], q4)
            s = s + jnp.concatenate([bias_ref[h] for h in heads], axis=1)
            if blk == 0:
                s = jnp.where(key_row < jnp.where(first, w, 0), NEG, s)
            s_buf[...] = s
            sink = jnp.concatenate([jnp.full((1, w), sink_ref[0, h], F32) for h in heads], axis=1)
            m = s_buf[0:8, :]
            for r in range(1, 2 * w // 8):
                m = jnp.maximum(m, s_buf[r * 8:(r + 1) * 8, :])
            m = jnp.maximum(_col_max(m), sink)
            acc = jnp.zeros((rc, group * w), F32)
            for r in range(2 * w // rc):
                p = jnp.exp(s_buf[r * rc:(r + 1) * rc, :] - m)
                acc = acc + p
                p_buf[r * rc:(r + 1) * rc, :] = p.astype(BF16)
            denom = _col_sum(acc) + jnp.exp(sink - m)
            o = _dot(vt_ref[kh * hd:(kh + 1) * hd, keys], p_buf[...]) / denom
            for g, h in enumerate(heads):
                at_ref[h * hd:(h + 1) * hd, cols] = o[:, g * w:(g + 1) * w].astype(BF16)

    y_ref[0] = x + _dot_tn(at_ref[...], w_out_ref[...])


def _swa_prompt(x, g_kv, w_kv, g_q, w_qt, sink, band_bias, w_out, tile):
    bsz, seq, d = x.shape
    tile = min(tile, seq)
    kvd = w_kv.shape[1] // 2
    w = WINDOW
    return pl.pallas_call(
        _swa_prompt_kernel,
        grid=(bsz, seq // tile),
        in_specs=[
            pl.BlockSpec((1, tile, d), lambda b, t: (b, t, 0)),
            _resident(g_kv.shape),
            _resident(w_kv.shape),
            _resident(g_q.shape),
            _resident(w_qt.shape),
            pl.BlockSpec(memory_space=pltpu.SMEM),
            _resident(band_bias.shape),
            _resident(w_out.shape),
        ],
        out_specs=[
            pl.BlockSpec((1, tile, d), lambda b, t: (b, t, 0)),
            pl.BlockSpec((1, w, kvd), lambda b, t: (b, 0, 0)),
            pl.BlockSpec((1, w, kvd), lambda b, t: (b, 0, 0)),
        ],
        out_shape=[
            jax.ShapeDtypeStruct((bsz, seq, d), F32),
            jax.ShapeDtypeStruct((bsz, w, kvd), F32),
            jax.ShapeDtypeStruct((bsz, w, kvd), F32),
        ],
        scratch_shapes=[
            pltpu.VMEM((B_KV_HEADS, tile + w, B_HEAD_DIM), BF16),
            pltpu.VMEM((kvd, tile + w), BF16),
            pltpu.VMEM((d, tile), BF16),
            pltpu.VMEM((d, tile), BF16),
            pltpu.VMEM((2, 2 * w, d // B_KV_HEADS * w // B_HEAD_DIM), F32),
            pltpu.VMEM((2, 2 * w, d // B_KV_HEADS * w // B_HEAD_DIM), BF16),
        ],
        compiler_params=_cparams("arbitrary", "arbitrary"),
        name="swa_prompt",
    )(x, g_kv, w_kv, g_q, w_qt, sink, band_bias, w_out)


def _hgrn_sample_gates_kernel(layer, x_ref, w_in_ref, a_lb_ref, g_mix_ref,
                              ft_ref, qt_ref, v_ref, g_ref):
    d = x_ref.shape[1]
    dk = d // A_HEADS
    h = (_rms_scale(x_ref[...]) * g_mix_ref[...]).astype(BF16)
    proj = _dot(h, w_in_ref[...])
    forget = _forget_gate(proj[:, d:2 * d], _lower_bound(a_lb_ref[...], layer))
    v_ref[...] = proj[:, 2 * d:3 * d]
    g_ref[...] = proj[:, 3 * d:4 * d]
    for hd in range(A_HEADS):
        ls = slice(hd * dk, (hd + 1) * dk)
        ft_ref[hd] = forget[:, ls].T
        qt_ref[hd] = proj[:, ls].T


def _hgrn_sample_gates(x, w_in, a_lb, g_mix, layer):
    m, d = x.shape
    dk = d // A_HEADS
    vm = pl.BlockSpec(memory_space=pltpu.VMEM)
    return pl.pallas_call(
        functools.partial(_hgrn_sample_gates_kernel, layer),
        in_specs=[vm, vm, vm, vm],
        out_specs=[vm, vm, vm, vm],
        out_shape=[
            jax.ShapeDtypeStruct((A_HEADS, dk, m), F32),
            jax.ShapeDtypeStruct((A_HEADS, dk, m), F32),
            jax.ShapeDtypeStruct((m, d), F32),
            jax.ShapeDtypeStruct((m, d), F32),
        ],
        compiler_params=pltpu.CompilerParams(vmem_limit_bytes=V7X_VMEM_LIMIT_BYTES),
        name="hgrn_sample_gates",
    )(x, w_in, a_lb, g_mix)


def _hgrn_sample_state_kernel(ft_ref, qt_ref, v_ref, s0_ref, s_ref, o_ref):
    i = pl.program_id(0)
    tb = s0_ref.shape[0]
    dk = s0_ref.shape[2]
    m = ft_ref.shape[2]
    lane = lax.broadcasted_iota(jnp.int32, (dk, m), 1)
    for tk in range(tb):
        pick = lane == i * tb + tk
        for hd in range(A_HEADS):
            ls = slice(hd * dk, (hd + 1) * dk)
            f = jnp.sum(jnp.where(pick, ft_ref[hd], 0.0), axis=1, keepdims=True)
            q = jnp.sum(jnp.where(pick, qt_ref[hd], 0.0), axis=1, keepdims=True)
            s_new = f * s0_ref[tk, hd] + (1.0 - f) * v_ref[tk:tk + 1, ls]
            s_ref[tk, hd] = s_new
            o_ref[tk:tk + 1, ls] = jnp.sum(q * s_new, axis=0, keepdims=True)


def _hgrn_sample_state(ft, qt, v, s0, tb):
    m, d = v.shape
    dk = d // A_HEADS
    return pl.pallas_call(
        _hgrn_sample_state_kernel,
        grid=(m // tb,),
        in_specs=[
            _resident(ft.shape),
            _resident(qt.shape),
            pl.BlockSpec((tb, d), lambda i: (i, 0)),
            pl.BlockSpec((tb, A_HEADS, dk, dk), lambda i: (i, 0, 0, 0)),
        ],
        out_specs=[
            pl.BlockSpec((tb, A_HEADS, dk, dk), lambda i: (i, 0, 0, 0)),
            pl.BlockSpec((tb, d), lambda i: (i, 0)),
        ],
        out_shape=[
            jax.ShapeDtypeStruct(s0.shape, F32),
            jax.ShapeDtypeStruct((m, d), F32),
        ],
        compiler_params=_cparams("arbitrary"),
        name="hgrn_sample_state",
    )(ft, qt, v, s0)


def _hgrn_sample_out_kernel(x_ref, o_ref, g_ref, g_norm_ref, w_out_ref, y_ref):
    d = x_ref.shape[1]
    dk = d // A_HEADS
    g = g_ref[...]
    gate = g * _sigmoid(g)
    parts = []
    for hd in range(A_HEADS):
        ls = slice(hd * dk, (hd + 1) * dk)
        parts.append(_rms_scale(o_ref[:, ls]) * g_norm_ref[...] * gate[:, ls])
    a = jnp.concatenate(parts, axis=1).astype(BF16)
    y_ref[...] = x_ref[...] + _dot(a, w_out_ref[...])


def _hgrn_sample_out(x, o, g, g_norm, w_out):
    vm = pl.BlockSpec(memory_space=pltpu.VMEM)
    return pl.pallas_call(
        _hgrn_sample_out_kernel,
        in_specs=[vm] * 5,
        out_specs=vm,
        out_shape=jax.ShapeDtypeStruct(x.shape, F32),
        compiler_params=pltpu.CompilerParams(vmem_limit_bytes=V7X_VMEM_LIMIT_BYTES),
        name="hgrn_sample_out",
    )(x, o, g, g_norm, w_out)


def _swa_sample_proj_kernel(x_ref, g_kv_ref, w_kv_ref, g_q_ref, w_qx_ref, kv_ref, qx_ref):
    scale = 1.0 / math.sqrt(B_HEAD_DIM)
    xn = _rms_scale(x_ref[...])
    kv_ref[...] = _dot((xn * g_kv_ref[...]).astype(BF16), w_kv_ref[...])
    qx_ref[...] = (_dot((xn * g_q_ref[...]).astype(BF16), w_qx_ref[...]) * scale).astype(BF16)


def _swa_sample_proj(x, g_kv, w_kv, g_q, w_qx):
    m = x.shape[0]
    vm = pl.BlockSpec(memory_space=pltpu.VMEM)
    return pl.pallas_call(
        _swa_sample_proj_kernel,
        in_specs=[vm] * 5,
        out_specs=[vm, vm],
        out_shape=[
            jax.ShapeDtypeStruct((m, w_kv.shape[1]), F32),
            jax.ShapeDtypeStruct((m, w_qx.shape[1]), BF16),
        ],
        compiler_params=pltpu.CompilerParams(vmem_limit_bytes=V7X_VMEM_LIMIT_BYTES),
        name="swa_sample_proj",
    )(x, g_kv, w_kv, g_q, w_qx)


def _swa_sample_attend_kernel(qx_ref, k_new_ref, v_new_ref, k_buf_ref, v_buf_ref, bias_ref,
                              sink_ref, k_out_ref, v_out_ref, r_ref):
    n_buf = k_buf_ref.shape[1]
    k_out_ref[:, 0:n_buf - 1, :] = k_buf_ref[:, 1:n_buf, :]
    v_out_ref[:, 0:n_buf - 1, :] = v_buf_ref[:, 1:n_buf, :]
    k_out_ref[:, n_buf - 1:n_buf, :] = k_new_ref[...]
    v_out_ref[:, n_buf - 1:n_buf, :] = v_new_ref[...]
    k = k_out_ref[...].astype(BF16)
    v = v_out_ref[...].astype(BF16)
    s = jnp.einsum("bhc,brc->bhr", qx_ref[...], k, preferred_element_type=F32)
    s = s + bias_ref[...][None]
    sink = sink_ref[...][None]
    m = jnp.maximum(jnp.max(s, axis=-1, keepdims=True), sink)
    p = jnp.exp(s - m)
    denom = jnp.sum(p, axis=-1, keepdims=True) + jnp.exp(sink - m)
    r = jnp.einsum("bhr,brc->bhc", p.astype(BF16), v, preferred_element_type=F32)
    r_ref[...] = (r / denom).astype(BF16)


def _swa_sample_attend(qx, k_new, v_new, k_buf, v_buf, bias_row, sink_col, tb):
    m, n_buf, kvd = k_buf.shape
    n_heads = qx.shape[1]
    blk3 = lambda i: (i, 0, 0)
    return pl.pallas_call(
        _swa_sample_attend_kernel,
        grid=(m // tb,),
        in_specs=[
            pl.BlockSpec((tb, n_heads, kvd), blk3),
            pl.BlockSpec((tb, 1, kvd), blk3),
            pl.BlockSpec((tb, 1, kvd), blk3),
            pl.BlockSpec((tb, n_buf, kvd), blk3),
            pl.BlockSpec((tb, n_buf, kvd), blk3),
            _resident(bias_row.shape),
            _resident(sink_col.shape),
        ],
        out_specs=[
            pl.BlockSpec((tb, n_buf, kvd), blk3),
            pl.BlockSpec((tb, n_buf, kvd), blk3),
            pl.BlockSpec((tb, n_heads, kvd), blk3),
        ],
        out_shape=[
            jax.ShapeDtypeStruct((m, n_buf, kvd), F32),
            jax.ShapeDtypeStruct((m, n_buf, kvd), F32),
            jax.ShapeDtypeStruct((m, n_heads, kvd), BF16),
        ],
        compiler_params=_cparams("arbitrary"),
        name="swa_sample_attend",
    )(qx, k_new, v_new, k_buf, v_buf, bias_row, sink_col)


def _swa_sample_out_kernel(x_ref, r_ref, w_ox_ref, y_ref):
    y_ref[...] = x_ref[...] + _dot(r_ref[...], w_ox_ref[...])


def _swa_sample_out(x, r_flat, w_ox):
    vm = pl.BlockSpec(memory_space=pltpu.VMEM)
    return pl.pallas_call(
        _swa_sample_out_kernel,
        in_specs=[vm] * 3,
        out_specs=vm,
        out_shape=jax.ShapeDtypeStruct(x.shape, F32),
        compiler_params=pltpu.CompilerParams(vmem_limit_bytes=V7X_VMEM_LIMIT_BYTES),
        name="swa_sample_out",
    )(x, r_flat, w_ox)


def _expand_heads(w_q, w_out):
    d = w_q.shape[0]
    hd = B_HEAD_DIM
    n_heads = w_q.shape[1] // hd
    group = n_heads // B_KV_HEADS
    slot = (jnp.arange(n_heads) // group)[:, None] == jnp.arange(B_KV_HEADS)[None, :]
    w_qx = jnp.where(slot[None, :, :, None], w_q.reshape(d, n_heads, 1, hd), 0)
    w_ox = jnp.where(slot[:, :, None, None], w_out.reshape(n_heads, 1, hd, -1), 0)
    kvd = B_KV_HEADS * hd
    return w_qx.reshape(d, n_heads * kvd), w_ox.reshape(n_heads * kvd, -1)


def kernel(x_prompt, x_sample, state_hgrn, cache_k_win, cache_v_win, w_a_in, a_lb, a_gnorm, w_a_out,
           g_mix, g_mlp, g_kv, w_kv, w_b_q, b_sink, w_b_out, rel_bias, w_up, w_down, g_final):
    bsz, seq, d = x_prompt.shape
    n_dec = x_sample.shape[0]
    assert x_sample.shape[1] == 1, "the sample group decodes one token per sequence"
    n_a = w_a_in.shape[0]
    n_b = w_b_q.shape[0]
    assert n_a == 1 and n_b == 1, "depth-2 trunk: one HGRN2 layer, then one attention layer"
    n_buf = cache_k_win.shape[1]
    assert n_buf == WINDOW and seq % WINDOW == 0
    kvd = B_KV_HEADS * B_HEAD_DIM
    n_heads = d // B_HEAD_DIM

    bf = lambda w: w.astype(BF16)
    row = lambda g: g.reshape(1, -1)
    w_in, w_ao = bf(w_a_in[0]), bf(w_a_out[0])
    w_kvb, w_q, w_bo = bf(w_kv), bf(w_b_q[0]), bf(w_b_out[0])
    w_qx, w_ox = _expand_heads(w_q, w_bo)
    w_up_b, w_down_b = bf(w_up), bf(w_down)
    band_bias = _band_bias(rel_bias)
    sink = b_sink[0].reshape(1, n_heads)

    x, s_prompt = _hgrn_prompt(x_prompt, w_in, a_lb, row(g_mix[0]), row(a_gnorm[0]), w_ao,
                               layer=0, tile=256)
    x = _mlp(x.reshape(bsz * seq, d), row(g_mlp[0]), w_up_b[0], w_down_b[0], None, tile=512)
    x, k_prompt, v_prompt = _swa_prompt(x.reshape(bsz, seq, d), row(g_kv), w_kvb, row(g_mix[1]),
                                        w_q.T, sink, band_bias, w_bo, tile=512)
    y_prompt = _mlp(x.reshape(bsz * seq, d), row(g_mlp[1]), w_up_b[1], w_down_b[1],
                    row(g_final), tile=512).reshape(bsz, seq, d)

    xs = x_sample.reshape(n_dec, d)
    ft, qt, v, g = _hgrn_sample_gates(xs, w_in, a_lb, row(g_mix[0]), layer=0)
    s_sample, o = _hgrn_sample_state(ft, qt, v, state_hgrn[0], tb=8)
    xs = _hgrn_sample_out(xs, o, g, row(a_gnorm[0]), w_ao)
    xs = _mlp(xs, row(g_mlp[0]), w_up_b[0], w_down_b[0], None, tile=n_dec)
    kv_new, qx = _swa_sample_proj(xs, row(g_kv), w_kvb, row(g_mix[1]), w_qx)
    bias_row = band_bias[:, 1:n_buf + 1, 0]
    k_sample, v_sample, r = _swa_sample_attend(
        qx.reshape(n_dec, n_heads, kvd),
        kv_new[:, :kvd].reshape(n_dec, 1, kvd), kv_new[:, kvd:].reshape(n_dec, 1, kvd),
        cache_k_win.reshape(n_dec, n_buf, kvd), cache_v_win.reshape(n_dec, n_buf, kvd),
        bias_row, sink.reshape(n_heads, 1), tb=8)
    xs = _swa_sample_out(xs, r.reshape(n_dec, n_heads * kvd), w_ox)
    y_sample = _mlp(xs, row(g_mlp[1]), w_up_b[1], w_down_b[1], row(g_final),
                    tile=n_dec).reshape(n_dec, 1, d)

    cache_shape = (-1, WINDOW, B_KV_HEADS, B_HEAD_DIM)
    return (y_prompt, y_sample, s_prompt[None], s_sample[None],
            k_prompt.reshape(cache_shape), v_prompt.reshape(cache_shape),
            k_sample.reshape(cache_shape), v_sample.reshape(cache_shape))
```

```python
import functools
import math

import jax
import jax.numpy as jnp
from jax import lax
from jax.experimental import pallas as pl
from jax.experimental.pallas import tpu as pltpu

F32 = jnp.float32
BF16 = jnp.bfloat16

EPS = 1e-6
NEG = -1e30
A_HEADS = 8
A_CHUNK = 64
B_HEAD_DIM = 64
B_KV_HEADS = 4
WINDOW = 128
N_BUCKETS = 32
MAX_DISTANCE = 128

V7X_VMEM_LIMIT_BYTES = 56 * 1024 * 1024


def _cparams(*semantics):
    return pltpu.CompilerParams(dimension_semantics=semantics,
                                vmem_limit_bytes=V7X_VMEM_LIMIT_BYTES)


def _resident(shape):
    nd = len(shape)
    return pl.BlockSpec(shape, lambda *_: (0,) * nd, pipeline_mode=pl.Buffered(1))


def _rms_scale(x):
    return x * lax.rsqrt(jnp.mean(x * x, axis=-1, keepdims=True) + EPS)


def _sigmoid(x):
    return 1.0 / (1.0 + jnp.exp(-x))


def _dot(a, b):
    return jnp.dot(a, b, preferred_element_type=F32)


def _dot_nt(a, b):
    return lax.dot_general(a, b, (((1,), (1,)), ((), ())), preferred_element_type=F32)


def _dot_tn(a, b):
    return lax.dot_general(a, b, (((0,), (0,)), ((), ())), preferred_element_type=F32)


def _lower_bound(a_lb, layer):
    m = jnp.max(a_lb, axis=0, keepdims=True)
    e = jnp.exp(a_lb - m)
    return jnp.sum(e[: layer + 1], axis=0, keepdims=True) / jnp.sum(e, axis=0, keepdims=True)


def _forget_gate(f_raw, lb):
    return lb + (1.0 - lb) * _sigmoid(f_raw)


def _hgrn_prompt_kernel(layer, x_ref, w_in_ref, a_lb_ref, g_mix_ref, g_norm_ref, w_out_ref,
                        y_ref, s_out_ref,
                        st_ref, proj_ref, lf_ref, kk_ref, b_ref, qi_ref, ki_ref, qn_ref, ks_ref, vb_ref,
                        dec_ref, att_ref, o_ref):
    t = pl.program_id(1)
    tile, d = x_ref.shape[1], x_ref.shape[2]
    dk = d // A_HEADS
    c = A_CHUNK
    n_chunks = tile // c

    @pl.when(t == 0)
    def _():
        st_ref[...] = jnp.zeros_like(st_ref)
        dec_ref[...] = jnp.zeros_like(dec_ref)

    x = x_ref[0]
    h = (_rms_scale(x) * g_mix_ref[...]).astype(BF16)
    proj_ref[...] = _dot(h, w_in_ref[...])

    lb = _lower_bound(a_lb_ref[...], layer)
    for ci in range(n_chunks):
        rows = slice(ci * c, (ci + 1) * c)
        forget = _forget_gate(proj_ref[rows, d:2 * d], lb)
        kk_ref[rows, :] = 1.0 - forget
        logf = jnp.log(forget)
        hi = logf.astype(BF16)
        r1 = logf - hi.astype(F32)
        mid = r1.astype(BF16)
        lf_ref[0, rows, :] = hi
        lf_ref[1, rows, :] = mid
        lf_ref[2, rows, :] = (r1 - mid.astype(F32)).astype(BF16)

    row = lax.broadcasted_iota(jnp.int32, (tile, tile), 0)
    col = lax.broadcasted_iota(jnp.int32, (tile, tile), 1)
    tri = jnp.where((row >= col) & (row // c == col // c), 1.0, 0.0).astype(BF16)
    b_ref[...] = _dot(tri, lf_ref[0]) + _dot(tri, lf_ref[1]) + _dot(tri, lf_ref[2])

    for ci in range(n_chunks):
        rows = slice(ci * c, (ci + 1) * c)
        b = b_ref[rows, :]
        b_mid = b_ref[ci * c + c // 2 - 1:ci * c + c // 2, :]
        b_last = b_ref[ci * c + c - 1:ci * c + c, :]
        q = proj_ref[rows, 0:d]
        k = kk_ref[rows, :]
        qi_ref[rows, :] = (q * jnp.exp(b - b_mid)).astype(BF16)
        ki_ref[rows, :] = (k * jnp.exp(b_mid - b)).astype(BF16)
        qn_ref[rows, :] = (q * jnp.exp(b)).astype(BF16)
        ks_ref[rows, :] = (k * jnp.exp(b_last - b)).astype(BF16)
        vb_ref[rows, :] = proj_ref[rows, 2 * d:3 * d].astype(BF16)
        decay = jnp.exp(b_last)
        for hd in range(A_HEADS):
            dec_ref[ci * A_HEADS + hd:ci * A_HEADS + hd + 1, :] = decay[:, hd * dk:(hd + 1) * dk]

    crow = lax.broadcasted_iota(jnp.int32, (c, c), 0)
    ccol = lax.broadcasted_iota(jnp.int32, (c, c), 1)
    causal = crow >= ccol
    for ci in range(n_chunks):
        rows = slice(ci * c, (ci + 1) * c)
        for hd in range(A_HEADS):
            ls = slice(hd * dk, (hd + 1) * dk)
            att = jnp.where(causal, _dot_nt(qi_ref[rows, ls], ki_ref[rows, ls]), 0.0)
            att_ref[ci * A_HEADS + hd] = att.astype(BF16)

    dec_t = dec_ref[...].T
    g_norm = g_norm_ref[...]
    for ci in range(n_chunks):
        rows = slice(ci * c, (ci + 1) * c)
        for hd in range(A_HEADS):
            ls = slice(hd * dk, (hd + 1) * dk)
            j = ci * A_HEADS + hd
            st = st_ref[hd]
            lhs = jnp.concatenate([qn_ref[rows, ls], att_ref[j]], axis=1)
            rhs = jnp.concatenate([st.astype(BF16), vb_ref[rows, ls]], axis=0)
            o = _dot(lhs, rhs)
            st_ref[hd] = st * dec_t[:, j:j + 1] + _dot_tn(ks_ref[rows, ls], vb_ref[rows, ls])
            o_ref[rows, ls] = _rms_scale(o) * g_norm

    g = proj_ref[:, 3 * d:4 * d]
    a = (o_ref[...] * (g * _sigmoid(g))).astype(BF16)
    y_ref[0] = x + _dot(a, w_out_ref[...])

    @pl.when(t == pl.num_programs(1) - 1)
    def _():
        s_out_ref[0] = st_ref[...]


def _hgrn_prompt(x, w_in, a_lb, g_mix, g_norm, w_out, layer, tile):
    bsz, seq, d = x.shape
    dk = d // A_HEADS
    tile = min(tile, seq)
    return pl.pallas_call(
        functools.partial(_hgrn_prompt_kernel, layer),
        grid=(bsz, seq // tile),
        in_specs=[
            pl.BlockSpec((1, tile, d), lambda b, t: (b, t, 0)),
            _resident(w_in.shape),
            _resident(a_lb.shape),
            _resident(g_mix.shape),
            _resident(g_norm.shape),
            _resident(w_out.shape),
        ],
        out_specs=[
            pl.BlockSpec((1, tile, d), lambda b, t: (b, t, 0)),
            pl.BlockSpec((1, A_HEADS, dk, dk), lambda b, t: (b, 0, 0, 0)),
        ],
        out_shape=[
            jax.ShapeDtypeStruct((bsz, seq, d), F32),
            jax.ShapeDtypeStruct((bsz, A_HEADS, dk, dk), F32),
        ],
        scratch_shapes=[
            pltpu.VMEM((A_HEADS, dk, dk), F32),
            pltpu.VMEM((tile, 4 * d), F32),
            pltpu.VMEM((3, tile, d), BF16),
            pltpu.VMEM((tile, d), F32),
            pltpu.VMEM((tile, d), F32),
            pltpu.VMEM((tile, d), BF16),
            pltpu.VMEM((tile, d), BF16),
            pltpu.VMEM((tile, d), BF16),
            pltpu.VMEM((tile, d), BF16),
            pltpu.VMEM((tile, d), BF16),
            pltpu.VMEM((dk, dk), F32),
            pltpu.VMEM((tile // A_CHUNK * A_HEADS, A_CHUNK, A_CHUNK), BF16),
            pltpu.VMEM((tile, d), F32),
        ],
        compiler_params=_cparams("arbitrary", "arbitrary"),
        name="hgrn_prompt",
    )(x, w_in, a_lb, g_mix, g_norm, w_out)


def _mlp_kernel(ff_chunk, has_final, x_ref, g_ref, w_up_ref, w_down_ref, *rest):
    if has_final:
        g_final_ref, y_ref = rest
    else:
        (y_ref,) = rest
    x = x_ref[...]
    h = (_rms_scale(x) * g_ref[...]).astype(BF16)
    acc = x
    for j in range(w_up_ref.shape[1] // ff_chunk):
        cols = slice(j * ff_chunk, (j + 1) * ff_chunk)
        u = jnp.maximum(_dot(h, w_up_ref[:, cols]), 0.0)
        acc = acc + _dot((u * u).astype(BF16), w_down_ref[cols, :])
    if has_final:
        acc = _rms_scale(acc) * g_final_ref[...]
    y_ref[...] = acc


def _mlp(x, g, w_up, w_down, g_final, tile, ff_chunk=1024):
    m, d = x.shape
    tile = min(tile, m)
    has_final = g_final is not None
    in_specs = [
        pl.BlockSpec((tile, d), lambda i: (i, 0)),
        _resident(g.shape),
        _resident(w_up.shape),
        _resident(w_down.shape),
    ]
    args = [x, g, w_up, w_down]
    if has_final:
        in_specs.append(_resident(g_final.shape))
        args.append(g_final)
    return pl.pallas_call(
        functools.partial(_mlp_kernel, ff_chunk, has_final),
        grid=(m // tile,),
        in_specs=in_specs,
        out_specs=pl.BlockSpec((tile, d), lambda i: (i, 0)),
        out_shape=jax.ShapeDtypeStruct((m, d), F32),
        compiler_params=_cparams("arbitrary"),
        name="mlp_final" if has_final else "mlp",
    )(*args)


def _t5_bucket(dist):
    n = jnp.maximum(dist, 0)
    max_exact = N_BUCKETS // 2
    nf = jnp.maximum(n, 1).astype(F32)
    large = max_exact + (jnp.log(nf / max_exact) / math.log(MAX_DISTANCE / max_exact)
                         * (N_BUCKETS - max_exact)).astype(jnp.int32)
    large = jnp.minimum(large, N_BUCKETS - 1)
    return jnp.where(n < max_exact, n, large)


def _band_bias_kernel(rel_bias_ref, out_ref):
    hd = pl.program_id(0)
    w = out_ref.shape[2]
    kj = lax.broadcasted_iota(jnp.int32, (2 * w, w), 0)
    qi = lax.broadcasted_iota(jnp.int32, (2 * w, w), 1)
    dist = qi + w - kj
    bucket = _t5_bucket(dist)
    bias = jnp.zeros((2 * w, w), F32)
    for bk in range(N_BUCKETS):
        bias = jnp.where(bucket == bk, rel_bias_ref[bk, hd], bias)
    out_ref[0] = jnp.where((dist >= 0) & (dist < WINDOW), bias, NEG)


def _band_bias(rel_bias):
    n_heads = rel_bias.shape[1]
    return pl.pallas_call(
        _band_bias_kernel,
        grid=(n_heads,),
        in_specs=[pl.BlockSpec(memory_space=pltpu.SMEM)],
        out_specs=pl.BlockSpec((1, 2 * WINDOW, WINDOW), lambda h: (h, 0, 0)),
        out_shape=jax.ShapeDtypeStruct((n_heads, 2 * WINDOW, WINDOW), F32),
        compiler_params=_cparams("arbitrary"),
        name="band_bias",
    )(rel_bias)


def _col_max(x):
    return jnp.max(x, axis=0, keepdims=True)


def _col_sum(x):
    return jnp.sum(x, axis=0, keepdims=True)


def _swa_prompt_kernel(x_ref, g_kv_ref, w_kv_ref, g_q_ref, w_qt_ref, sink_ref, bias_ref, w_out_ref,
                       y_ref, k_out_ref, v_out_ref, k_ref, vt_ref, qt_ref, at_ref, s_ref, p_ref):
    t = pl.program_id(1)
    tile, d = x_ref.shape[1], x_ref.shape[2]
    w = WINDOW
    hd = B_HEAD_DIM
    n_heads = d // hd
    group = n_heads // B_KV_HEADS
    kvd = B_KV_HEADS * hd
    scale = 1.0 / math.sqrt(hd)

    @pl.when(t == 0)
    def _():
        k_ref[:, 0:w, :] = jnp.zeros((B_KV_HEADS, w, hd), BF16)
        vt_ref[:, 0:w] = jnp.zeros((kvd, w), BF16)

    @pl.when(t > 0)
    def _():
        k_ref[:, 0:w, :] = k_ref[:, tile:tile + w, :]
        vt_ref[:, 0:w] = vt_ref[:, tile:tile + w]

    x = x_ref[0]
    xn = _rms_scale(x)
    kv = _dot((xn * g_kv_ref[...]).astype(BF16), w_kv_ref[...])
    for kh in range(B_KV_HEADS):
        k_ref[kh, w:w + tile, :] = kv[:, kh * hd:(kh + 1) * hd].astype(BF16)
    vt_ref[:, w:w + tile] = kv[:, kvd:2 * kvd].T.astype(BF16)
    h_q = (xn * g_q_ref[...]).astype(BF16)
    qt_ref[...] = (_dot_nt(w_qt_ref[...], h_q) * scale).astype(BF16)

    @pl.when(t == pl.num_programs(1) - 1)
    def _():
        k_out_ref[0] = kv[tile - w:, 0:kvd]
        v_out_ref[0] = kv[tile - w:, kvd:2 * kvd]

    first = t == 0
    key_row = lax.broadcasted_iota(jnp.int32, (2 * w, group * w), 0)
    rc = 16
    unit = 0
    for blk in range(tile // w):
        cols = slice(blk * w, (blk + 1) * w)
        keys = slice(blk * w, (blk + 2) * w)
        for kh in range(B_KV_HEADS):
            heads = range(kh * group, (kh + 1) * group)
            s_buf = s_ref.at[unit % 2]
            p_buf = p_ref.at[unit % 2]
            unit += 1
            q4 = jnp.concatenate([qt_ref[h * hd:(h + 1) * hd, cols] for h in heads], axis=1)
            s = _dot(k_ref[kh, keys, :], q4)
            s = s + jnp.concatenate([bias_ref[h] for h in heads], axis=1)
            if blk == 0:
                s = jnp.where(key_row < jnp.where(first, w, 0), NEG, s)
            s_buf[...] = s
            sink = jnp.concatenate([jnp.full((1, w), sink_ref[0, h], F32) for h in heads], axis=1)
            m = s_buf[0:8, :]
            for r in range(1, 2 * w // 8):
                m = jnp.maximum(m, s_buf[r * 8:(r + 1) * 8, :])
            m = jnp.maximum(_col_max(m), sink)
            acc = jnp.zeros((rc, group * w), F32)
            for r in range(2 * w // rc):
                p = jnp.exp(s_buf[r * rc:(r + 1) * rc, :] - m)
                acc = acc + p
                p_buf[r * rc:(r + 1) * rc, :] = p.astype(BF16)
            denom = _col_sum(acc) + jnp.exp(sink - m)
            o = _dot(vt_ref[kh * hd:(kh + 1) * hd, keys], p_buf[...]) / denom
            for g, h in enumerate(heads):
                at_ref[h * hd:(h + 1) * hd, cols] = o[:, g * w:(g + 1) * w].astype(BF16)

    y_ref[0] = x + _dot_tn(at_ref[...], w_out_ref[...])


def _swa_prompt(x, g_kv, w_kv, g_q, w_qt, sink, band_bias, w_out, tile):
    bsz, seq, d = x.shape
    tile = min(tile, seq)
    kvd = w_kv.shape[1] // 2
    w = WINDOW
    return pl.pallas_call(
        _swa_prompt_kernel,
        grid=(bsz, seq // tile),
        in_specs=[
            pl.BlockSpec((1, tile, d), lambda b, t: (b, t, 0)),
            _resident(g_kv.shape),
            _resident(w_kv.shape),
            _resident(g_q.shape),
            _resident(w_qt.shape),
            pl.BlockSpec(memory_space=pltpu.SMEM),
            _resident(band_bias.shape),
            _resident(w_out.shape),
        ],
        out_specs=[
            pl.BlockSpec((1, tile, d), lambda b, t: (b, t, 0)),
            pl.BlockSpec((1, w, kvd), lambda b, t: (b, 0, 0)),
            pl.BlockSpec((1, w, kvd), lambda b, t: (b, 0, 0)),
        ],
        out_shape=[
            jax.ShapeDtypeStruct((bsz, seq, d), F32),
            jax.ShapeDtypeStruct((bsz, w, kvd), F32),
            jax.ShapeDtypeStruct((bsz, w, kvd), F32),
        ],
        scratch_shapes=[
            pltpu.VMEM((B_KV_HEADS, tile + w, B_HEAD_DIM), BF16),
            pltpu.VMEM((kvd, tile + w), BF16),
            pltpu.VMEM((d, tile), BF16),
            pltpu.VMEM((d, tile), BF16),
            pltpu.VMEM((2, 2 * w, d // B_KV_HEADS * w // B_HEAD_DIM), F32),
            pltpu.VMEM((2, 2 * w, d // B_KV_HEADS * w // B_HEAD_DIM), BF16),
        ],
        compiler_params=_cparams("arbitrary", "arbitrary"),
        name="swa_prompt",
    )(x, g_kv, w_kv, g_q, w_qt, sink, band_bias, w_out)


def _hgrn_sample_gates_kernel(layer, x_ref, w_in_ref, a_lb_ref, g_mix_ref,
                              ft_ref, qt_ref, v_ref, g_ref):
    d = x_ref.shape[1]
    dk = d // A_HEADS
    h = (_rms_scale(x_ref[...]) * g_mix_ref[...]).astype(BF16)
    proj = _dot(h, w_in_ref[...])
    forget = _forget_gate(proj[:, d:2 * d], _lower_bound(a_lb_ref[...], layer))
    v_ref[...] = proj[:, 2 * d:3 * d]
    g_ref[...] = proj[:, 3 * d:4 * d]
    for hd in range(A_HEADS):
        ls = slice(hd * dk, (hd + 1) * dk)
        ft_ref[hd] = forget[:, ls].T
        qt_ref[hd] = proj[:, ls].T


def _hgrn_sample_gates(x, w_in, a_lb, g_mix, layer):
    m, d = x.shape
    dk = d // A_HEADS
    vm = pl.BlockSpec(memory_space=pltpu.VMEM)
    return pl.pallas_call(
        functools.partial(_hgrn_sample_gates_kernel, layer),
        in_specs=[vm, vm, vm, vm],
        out_specs=[vm, vm, vm, vm],
        out_shape=[
            jax.ShapeDtypeStruct((A_HEADS, dk, m), F32),
            jax.ShapeDtypeStruct((A_HEADS, dk, m), F32),
            jax.ShapeDtypeStruct((m, d), F32),
            jax.ShapeDtypeStruct((m, d), F32),
        ],
        compiler_params=pltpu.CompilerParams(vmem_limit_bytes=V7X_VMEM_LIMIT_BYTES),
        name="hgrn_sample_gates",
    )(x, w_in, a_lb, g_mix)


def _hgrn_sample_state_kernel(ft_ref, qt_ref, v_ref, s0_ref, s_ref, o_ref):
    i = pl.program_id(0)
    tb = s0_ref.shape[0]
    dk = s0_ref.shape[2]
    m = ft_ref.shape[2]
    lane = lax.broadcasted_iota(jnp.int32, (dk, m), 1)
    for tk in range(tb):
        pick = lane == i * tb + tk
        for hd in range(A_HEADS):
            ls = slice(hd * dk, (hd + 1) * dk)
            f = jnp.sum(jnp.where(pick, ft_ref[hd], 0.0), axis=1, keepdims=True)
            q = jnp.sum(jnp.where(pick, qt_ref[hd], 0.0), axis=1, keepdims=True)
            s_new = f * s0_ref[tk, hd] + (1.0 - f) * v_ref[tk:tk + 1, ls]
            s_ref[tk, hd] = s_new
            o_ref[tk:tk + 1, ls] = jnp.sum(q * s_new, axis=0, keepdims=True)


def _hgrn_sample_state(ft, qt, v, s0, tb):
    m, d = v.shape
    dk = d // A_HEADS
    return pl.pallas_call(
        _hgrn_sample_state_kernel,
        grid=(m // tb,),
        in_specs=[
            _resident(ft.shape),
            _resident(qt.shape),
            pl.BlockSpec((tb, d), lambda i: (i, 0)),
            pl.BlockSpec((tb, A_HEADS, dk, dk), lambda i: (i, 0, 0, 0)),
        ],
        out_specs=[
            pl.BlockSpec((tb, A_HEADS, dk, dk), lambda i: (i, 0, 0, 0)),
            pl.BlockSpec((tb, d), lambda i: (i, 0)),
        ],
        out_shape=[
            jax.ShapeDtypeStruct(s0.shape, F32),
            jax.ShapeDtypeStruct((m, d), F32),
        ],
        compiler_params=_cparams("arbitrary"),
        name="hgrn_sample_state",
    )(ft, qt, v, s0)


def _hgrn_sample_out_kernel(x_ref, o_ref, g_ref, g_norm_ref, w_out_ref, y_ref):
    d = x_ref.shape[1]
    dk = d // A_HEADS
    g = g_ref[...]
    gate = g * _sigmoid(g)
    parts = []
    for hd in range(A_HEADS):
        ls = slice(hd * dk, (hd + 1) * dk)
        parts.append(_rms_scale(o_ref[:, ls]) * g_norm_ref[...] * gate[:, ls])
    a = jnp.concatenate(parts, axis=1).astype(BF16)
    y_ref[...] = x_ref[...] + _dot(a, w_out_ref[...])


def _hgrn_sample_out(x, o, g, g_norm, w_out):
    vm = pl.BlockSpec(memory_space=pltpu.VMEM)
    return pl.pallas_call(
        _hgrn_sample_out_kernel,
        in_specs=[vm] * 5,
        out_specs=vm,
        out_shape=jax.ShapeDtypeStruct(x.shape, F32),
        compiler_params=pltpu.CompilerParams(vmem_limit_bytes=V7X_VMEM_LIMIT_BYTES),
        name="hgrn_sample_out",
    )(x, o, g, g_norm, w_out)


def _swa_sample_proj_kernel(x_ref, g_kv_ref, w_kv_ref, g_q_ref, w_qx_ref, kv_ref, qx_ref):
    scale = 1.0 / math.sqrt(B_HEAD_DIM)
    xn = _rms_scale(x_ref[...])
    kv_ref[...] = _dot((xn * g_kv_ref[...]).astype(BF16), w_kv_ref[...])
    qx_ref[...] = (_dot((xn * g_q_ref[...]).astype(BF16), w_qx_ref[...]) * scale).astype(BF16)


def _swa_sample_proj(x, g_kv, w_kv, g_q, w_qx):
    m = x.shape[0]
    vm = pl.BlockSpec(memory_space=pltpu.VMEM)
    return pl.pallas_call(
        _swa_sample_proj_kernel,
        in_specs=[vm] * 5,
        out_specs=[vm, vm],
        out_shape=[
            jax.ShapeDtypeStruct((m, w_kv.shape[1]), F32),
            jax.ShapeDtypeStruct((m, w_qx.shape[1]), BF16),
        ],
        compiler_params=pltpu.CompilerParams(vmem_limit_bytes=V7X_VMEM_LIMIT_BYTES),
        name="swa_sample_proj",
    )(x, g_kv, w_kv, g_q, w_qx)


def _swa_sample_attend_kernel(qx_ref, k_new_ref, v_new_ref, k_buf_ref, v_buf_ref, bias_ref,
                              sink_ref, k_out_ref, v_out_ref, r_ref):
    n_buf = k_buf_ref.shape[1]
    k_out_ref[:, 0:n_buf - 1, :] = k_buf_ref[:, 1:n_buf, :]
    v_out_ref[:, 0:n_buf - 1, :] = v_buf_ref[:, 1:n_buf, :]
    k_out_ref[:, n_buf - 1:n_buf, :] = k_new_ref[...]
    v_out_ref[:, n_buf - 1:n_buf, :] = v_new_ref[...]
    k = k_out_ref[...].astype(BF16)
    v = v_out_ref[...].astype(BF16)
    s = jnp.einsum("bhc,brc->bhr", qx_ref[...], k, preferred_element_type=F32)
    s = s + bias_ref[...][None]
    sink = sink_ref[...][None]
    m = jnp.maximum(jnp.max(s, axis=-1, keepdims=True), sink)
    p = jnp.exp(s - m)
    denom = jnp.sum(p, axis=-1, keepdims=True) + jnp.exp(sink - m)
    r = jnp.einsum("bhr,brc->bhc", p.astype(BF16), v, preferred_element_type=F32)
    r_ref[...] = (r / denom).astype(BF16)


def _swa_sample_attend(qx, k_new, v_new, k_buf, v_buf, bias_row, sink_col, tb):
    m, n_buf, kvd = k_buf.shape
    n_heads = qx.shape[1]
    blk3 = lambda i: (i, 0, 0)
    return pl.pallas_call(
        _swa_sample_attend_kernel,
        grid=(m // tb,),
        in_specs=[
            pl.BlockSpec((tb, n_heads, kvd), blk3),
            pl.BlockSpec((tb, 1, kvd), blk3),
            pl.BlockSpec((tb, 1, kvd), blk3),
            pl.BlockSpec((tb, n_buf, kvd), blk3),
            pl.BlockSpec((tb, n_buf, kvd), blk3),
            _resident(bias_row.shape),
            _resident(sink_col.shape),
        ],
        out_specs=[
            pl.BlockSpec((tb, n_buf, kvd), blk3),
            pl.BlockSpec((tb, n_buf, kvd), blk3),
            pl.BlockSpec((tb, n_heads, kvd), blk3),
        ],
        out_shape=[
            jax.ShapeDtypeStruct((m, n_buf, kvd), F32),
            jax.ShapeDtypeStruct((m, n_buf, kvd), F32),
            jax.ShapeDtypeStruct((m, n_heads, kvd), BF16),
        ],
        compiler_params=_cparams("arbitrary"),
        name="swa_sample_attend",
    )(qx, k_new, v_new, k_buf, v_buf, bias_row, sink_col)


def _swa_sample_out_kernel(x_ref, r_ref, w_ox_ref, y_ref):
    y_ref[...] = x_ref[...] + _dot(r_ref[...], w_ox_ref[...])


def _swa_sample_out(x, r_flat, w_ox):
    vm = pl.BlockSpec(memory_space=pltpu.VMEM)
    return pl.pallas_call(
        _swa_sample_out_kernel,
        in_specs=[vm] * 3,
        out_specs=vm,
        out_shape=jax.ShapeDtypeStruct(x.shape, F32),
        compiler_params=pltpu.CompilerParams(vmem_limit_bytes=V7X_VMEM_LIMIT_BYTES),
        name="swa_sample_out",
    )(x, r_flat, w_ox)


def _expand_heads(w_q, w_out):
    d = w_q.shape[0]
    hd = B_HEAD_DIM
    n_heads = w_q.shape[1] // hd
    group = n_heads // B_KV_HEADS
    slot = (jnp.arange(n_heads) // group)[:, None] == jnp.arange(B_KV_HEADS)[None, :]
    w_qx = jnp.where(slot[None, :, :, None], w_q.reshape(d, n_heads, 1, hd), 0)
    w_ox = jnp.where(slot[:, :, None, None], w_out.reshape(n_heads, 1, hd, -1), 0)
    kvd = B_KV_HEADS * hd
    return w_qx.reshape(d, n_heads * kvd), w_ox.reshape(n_heads * kvd, -1)


def kernel(x_prompt, x_sample, state_hgrn, cache_k_win, cache_v_win, w_a_in, a_lb, a_gnorm, w_a_out,
           g_mix, g_mlp, g_kv, w_kv, w_b_q, b_sink, w_b_out, rel_bias, w_up, w_down, g_final):
    bsz, seq, d = x_prompt.shape
    n_dec = x_sample.shape[0]
    assert x_sample.shape[1] == 1, "the sample group decodes one token per sequence"
    n_a = w_a_in.shape[0]
    n_b = w_b_q.shape[0]
    assert n_a == 1 and n_b == 1, "depth-2 trunk: one HGRN2 layer, then one attention layer"
    n_buf = cache_k_win.shape[1]
    assert n_buf == WINDOW and seq % WINDOW == 0
    kvd = B_KV_HEADS * B_HEAD_DIM
    n_heads = d // B_HEAD_DIM

    bf = lambda w: w.astype(BF16)
    row = lambda g: g.reshape(1, -1)
    w_in, w_ao = bf(w_a_in[0]), bf(w_a_out[0])
    w_kvb, w_q, w_bo = bf(w_kv), bf(w_b_q[0]), bf(w_b_out[0])
    w_qx, w_ox = _expand_heads(w_q, w_bo)
    w_up_b, w_down_b = bf(w_up), bf(w_down)
    band_bias = _band_bias(rel_bias)
    sink = b_sink[0].reshape(1, n_heads)

    x, s_prompt = _hgrn_prompt(x_prompt, w_in, a_lb, row(g_mix[0]), row(a_gnorm[0]), w_ao,
                               layer=0, tile=256)
    x = _mlp(x.reshape(bsz * seq, d), row(g_mlp[0]), w_up_b[0], w_down_b[0], None, tile=512)
    x, k_prompt, v_prompt = _swa_prompt(x.reshape(bsz, seq, d), row(g_kv), w_kvb, row(g_mix[1]),
                                        w_q.T, sink, band_bias, w_bo, tile=512)
    y_prompt = _mlp(x.reshape(bsz * seq, d), row(g_mlp[1]), w_up_b[1], w_down_b[1],
                    row(g_final), tile=512).reshape(bsz, seq, d)

    xs = x_sample.reshape(n_dec, d)
    ft, qt, v, g = _hgrn_sample_gates(xs, w_in, a_lb, row(g_mix[0]), layer=0)
    s_sample, o = _hgrn_sample_state(ft, qt, v, state_hgrn[0], tb=8)
    xs = _hgrn_sample_out(xs, o, g, row(a_gnorm[0]), w_ao)
    xs = _mlp(xs, row(g_mlp[0]), w_up_b[0], w_down_b[0], None, tile=n_dec)
    kv_new, qx = _swa_sample_proj(xs, row(g_kv), w_kvb, row(g_mix[1]), w_qx)
    bias_row = band_bias[:, 1:n_buf + 1, 0]
    k_sample, v_sample, r = _swa_sample_attend(
        qx.reshape(n_dec, n_heads, kvd),
        kv_new[:, :kvd].reshape(n_dec, 1, kvd), kv_new[:, kvd:].reshape(n_dec, 1, kvd),
        cache_k_win.reshape(n_dec, n_buf, kvd), cache_v_win.reshape(n_dec, n_buf, kvd),
        bias_row, sink.reshape(n_heads, 1), tb=8)
    xs = _swa_sample_out(xs, r.reshape(n_dec, n_heads * kvd), w_ox)
    y_sample = _mlp(xs, row(g_mlp[1]), w_up_b[1], w_down_b[1], row(g_final),
                    tile=n_dec).reshape(n_dec, 1, d)

    cache_shape = (-1, WINDOW, B_KV_HEADS, B_HEAD_DIM)
    return (y_prompt, y_sample, s_prompt[None], s_sample[None],
            k_prompt.reshape(cache_shape), v_prompt.reshape(cache_shape),
            k_sample.reshape(cache_shape), v_sample.reshape(cache_shape))
```

```python
import functools
import math

import jax
import jax.numpy as jnp
from jax import lax
from jax.experimental import pallas as pl
from jax.experimental.pallas import tpu as pltpu

F32 = jnp.float32
BF16 = jnp.bfloat16

EPS = 1e-6
NEG = -1e30
A_HEADS = 8
A_CHUNK = 64
B_HEAD_DIM = 64
B_KV_HEADS = 4
WINDOW = 128
N_BUCKETS = 32
MAX_DISTANCE = 128

V7X_VMEM_LIMIT_BYTES = 56 * 1024 * 1024


def _cparams(*semantics):
    return pltpu.CompilerParams(dimension_semantics=semantics,
                                vmem_limit_bytes=V7X_VMEM_LIMIT_BYTES)


def _resident(shape):
    nd = len(shape)
    return pl.BlockSpec(shape, lambda *_: (0,) * nd, pipeline_mode=pl.Buffered(1))


def _rms_scale(x):
    return x * lax.rsqrt(jnp.mean(x * x, axis=-1, keepdims=True) + EPS)


def _sigmoid(x):
    return 1.0 / (1.0 + jnp.exp(-x))


def _dot(a, b):
    return jnp.dot(a, b, preferred_element_type=F32)


def _dot_nt(a, b):
    return lax.dot_general(a, b, (((1,), (1,)), ((), ())), preferred_element_type=F32)


def _dot_tn(a, b):
    return lax.dot_general(a, b, (((0,), (0,)), ((), ())), preferred_element_type=F32)


def _lower_bound(a_lb, layer):
    m = jnp.max(a_lb, axis=0, keepdims=True)
    e = jnp.exp(a_lb - m)
    return jnp.sum(e[: layer + 1], axis=0, keepdims=True) / jnp.sum(e, axis=0, keepdims=True)


def _forget_gate(f_raw, lb):
    return lb + (1.0 - lb) * _sigmoid(f_raw)


def _hgrn_prompt_kernel(layer, x_ref, w_in_ref, a_lb_ref, g_mix_ref, g_norm_ref, w_out_ref,
                        y_ref, s_out_ref,
                        st_ref, proj_ref, lf_ref, kk_ref, b_ref, qi_ref, ki_ref, qn_ref, ks_ref, vb_ref,
                        dec_ref, att_ref, o_ref):
    t = pl.program_id(1)
    tile, d = x_ref.shape[1], x_ref.shape[2]
    dk = d // A_HEADS
    c = A_CHUNK
    n_chunks = tile // c

    @pl.when(t == 0)
    def _():
        st_ref[...] = jnp.zeros_like(st_ref)
        dec_ref[...] = jnp.zeros_like(dec_ref)

    x = x_ref[0]
    h = (_rms_scale(x) * g_mix_ref[...]).astype(BF16)
    proj_ref[...] = _dot(h, w_in_ref[...])

    lb = _lower_bound(a_lb_ref[...], layer)
    for ci in range(n_chunks):
        rows = slice(ci * c, (ci + 1) * c)
        forget = _forget_gate(proj_ref[rows, d:2 * d], lb)
        kk_ref[rows, :] = 1.0 - forget
        logf = jnp.log(forget)
        hi = logf.astype(BF16)
        r1 = logf - hi.astype(F32)
        mid = r1.astype(BF16)
        lf_ref[0, rows, :] = hi
        lf_ref[1, rows, :] = mid
        lf_ref[2, rows, :] = (r1 - mid.astype(F32)).astype(BF16)

    row = lax.broadcasted_iota(jnp.int32, (tile, tile), 0)
    col = lax.broadcasted_iota(jnp.int32, (tile, tile), 1)
    tri = jnp.where((row >= col) & (row // c == col // c), 1.0, 0.0).astype(BF16)
    b_ref[...] = _dot(tri, lf_ref[0]) + _dot(tri, lf_ref[1]) + _dot(tri, lf_ref[2])

    for ci in range(n_chunks):
        rows = slice(ci * c, (ci + 1) * c)
        b = b_ref[rows, :]
        b_mid = b_ref[ci * c + c // 2 - 1:ci * c + c // 2, :]
        b_last = b_ref[ci * c + c - 1:ci * c + c, :]
        q = proj_ref[rows, 0:d]
        k = kk_ref[rows, :]
        qi_ref[rows, :] = (q * jnp.exp(b - b_mid)).astype(BF16)
        ki_ref[rows, :] = (k * jnp.exp(b_mid - b)).astype(BF16)
        qn_ref[rows, :] = (q * jnp.exp(b)).astype(BF16)
        ks_ref[rows, :] = (k * jnp.exp(b_last - b)).astype(BF16)
        vb_ref[rows, :] = proj_ref[rows, 2 * d:3 * d].astype(BF16)
        decay = jnp.exp(b_last)
        for hd in range(A_HEADS):
            dec_ref[ci * A_HEADS + hd:ci * A_HEADS + hd + 1, :] = decay[:, hd * dk:(hd + 1) * dk]

    crow = lax.broadcasted_iota(jnp.int32, (c, c), 0)
    ccol = lax.broadcasted_iota(jnp.int32, (c, c), 1)
    causal = crow >= ccol
    for ci in range(n_chunks):
        rows = slice(ci * c, (ci + 1) * c)
        for hd in range(A_HEADS):
            ls = slice(hd * dk, (hd + 1) * dk)
            att = jnp.where(causal, _dot_nt(qi_ref[rows, ls], ki_ref[rows, ls]), 0.0)
            att_ref[ci * A_HEADS + hd] = att.astype(BF16)

    dec_t = dec_ref[...].T
    g_norm = g_norm_ref[...]
    for ci in range(n_chunks):
        rows = slice(ci * c, (ci + 1) * c)
        for hd in range(A_HEADS):
            ls = slice(hd * dk, (hd + 1) * dk)
            j = ci * A_HEADS + hd
            st = st_ref[hd]
            lhs = jnp.concatenate([qn_ref[rows, ls], att_ref[j]], axis=1)
            rhs = jnp.concatenate([st.astype(BF16), vb_ref[rows, ls]], axis=0)
            o = _dot(lhs, rhs)
            st_ref[hd] = st * dec_t[:, j:j + 1] + _dot_tn(ks_ref[rows, ls], vb_ref[rows, ls])
            o_ref[rows, ls] = _rms_scale(o) * g_norm

    g = proj_ref[:, 3 * d:4 * d]
    a = (o_ref[...] * (g * _sigmoid(g))).astype(BF16)
    y_ref[0] = x + _dot(a, w_out_ref[...])

    @pl.when(t == pl.num_programs(1) - 1)
    def _():
        s_out_ref[0] = st_ref[...]


def _hgrn_prompt(x, w_in, a_lb, g_mix, g_norm, w_out, layer, tile):
    bsz, seq, d = x.shape
    dk = d // A_HEADS
    tile = min(tile, seq)
    return pl.pallas_call(
        functools.partial(_hgrn_prompt_kernel, layer),
        grid=(bsz, seq // tile),
        in_specs=[
            pl.BlockSpec((1, tile, d), lambda b, t: (b, t, 0)),
            _resident(w_in.shape),
            _resident(a_lb.shape),
            _resident(g_mix.shape),
            _resident(g_norm.shape),
            _resident(w_out.shape),
        ],
        out_specs=[
            pl.BlockSpec((1, tile, d), lambda b, t: (b, t, 0)),
            pl.BlockSpec((1, A_HEADS, dk, dk), lambda b, t: (b, 0, 0, 0)),
        ],
        out_shape=[
            jax.ShapeDtypeStruct((bsz, seq, d), F32),
            jax.ShapeDtypeStruct((bsz, A_HEADS, dk, dk), F32),
        ],
        scratch_shapes=[
            pltpu.VMEM((A_HEADS, dk, dk), F32),
            pltpu.VMEM((tile, 4 * d), F32),
            pltpu.VMEM((3, tile, d), BF16),
            pltpu.VMEM((tile, d), F32),
            pltpu.VMEM((tile, d), F32),
            pltpu.VMEM((tile, d), BF16),
            pltpu.VMEM((tile, d), BF16),
            pltpu.VMEM((tile, d), BF16),
            pltpu.VMEM((tile, d), BF16),
            pltpu.VMEM((tile, d), BF16),
            pltpu.VMEM((dk, dk), F32),
            pltpu.VMEM((tile // A_CHUNK * A_HEADS, A_CHUNK, A_CHUNK), BF16),
            pltpu.VMEM((tile, d), F32),
        ],
        compiler_params=_cparams("arbitrary", "arbitrary"),
        name="hgrn_prompt",
    )(x, w_in, a_lb, g_mix, g_norm, w_out)


def _mlp_kernel(ff_chunk, has_final, x_ref, g_ref, w_up_ref, w_down_ref, *rest):
    if has_final:
        g_final_ref, y_ref = rest
    else:
        (y_ref,) = rest
    x = x_ref[...]
    h = (_rms_scale(x) * g_ref[...]).astype(BF16)
    acc = x
    for j in range(w_up_ref.shape[1] // ff_chunk):
        cols = slice(j * ff_chunk, (j + 1) * ff_chunk)
        u = jnp.maximum(_dot(h, w_up_ref[:, cols]), 0.0)
        acc = acc + _dot((u * u).astype(BF16), w_down_ref[cols, :])
    if has_final:
        acc = _rms_scale(acc) * g_final_ref[...]
    y_ref[...] = acc


def _mlp(x, g, w_up, w_down, g_final, tile, ff_chunk=1024):
    m, d = x.shape
    tile = min(tile, m)
    has_final = g_final is not None
    in_specs = [
        pl.BlockSpec((tile, d), lambda i: (i, 0)),
        _resident(g.shape),
        _resident(w_up.shape),
        _resident(w_down.shape),
    ]
    args = [x, g, w_up, w_down]
    if has_final:
        in_specs.append(_resident(g_final.shape))
        args.append(g_final)
    return pl.pallas_call(
        functools.partial(_mlp_kernel, ff_chunk, has_final),
        grid=(m // tile,),
        in_specs=in_specs,
        out_specs=pl.BlockSpec((tile, d), lambda i: (i, 0)),
        out_shape=jax.ShapeDtypeStruct((m, d), F32),
        compiler_params=_cparams("arbitrary"),
        name="mlp_final" if has_final else "mlp",
    )(*args)


def _t5_bucket(dist):
    n = jnp.maximum(dist, 0)
    max_exact = N_BUCKETS // 2
    nf = jnp.maximum(n, 1).astype(F32)
    large = max_exact + (jnp.log(nf / max_exact) / math.log(MAX_DISTANCE / max_exact)
                         * (N_BUCKETS - max_exact)).astype(jnp.int32)
    large = jnp.minimum(large, N_BUCKETS - 1)
    return jnp.where(n < max_exact, n, large)


def _band_bias_kernel(rel_bias_ref, out_ref):
    hd = pl.program_id(0)
    w = out_ref.shape[2]
    kj = lax.broadcasted_iota(jnp.int32, (2 * w, w), 0)
    qi = lax.broadcasted_iota(jnp.int32, (2 * w, w), 1)
    dist = qi + w - kj
    bucket = _t5_bucket(dist)
    bias = jnp.zeros((2 * w, w), F32)
    for bk in range(N_BUCKETS):
        bias = jnp.where(bucket == bk, rel_bias_ref[bk, hd], bias)
    out_ref[0] = jnp.where((dist >= 0) & (dist < WINDOW), bias, NEG)


def _band_bias(rel_bias):
    n_heads = rel_bias.shape[1]
    return pl.pallas_call(
        _band_bias_kernel,
        grid=(n_heads,),
        in_specs=[pl.BlockSpec(memory_space=pltpu.SMEM)],
        out_specs=pl.BlockSpec((1, 2 * WINDOW, WINDOW), lambda h: (h, 0, 0)),
        out_shape=jax.ShapeDtypeStruct((n_heads, 2 * WINDOW, WINDOW), F32),
        compiler_params=_cparams("arbitrary"),
        name="band_bias",
    )(rel_bias)


def _col_max(x):
    return jnp.max(x, axis=0, keepdims=True)


def _col_sum(x):
    return jnp.sum(x, axis=0, keepdims=True)


def _swa_prompt_kernel(x_ref, g_kv_ref, w_kv_ref, g_q_ref, w_qt_ref, sink_ref, bias_ref, w_out_ref,
                       y_ref, k_out_ref, v_out_ref, k_ref, vt_ref, qt_ref, at_ref, s_ref, p_ref):
    t = pl.program_id(1)
    tile, d = x_ref.shape[1], x_ref.shape[2]
    w = WINDOW
    hd = B_HEAD_DIM
    n_heads = d // hd
    group = n_heads // B_KV_HEADS
    kvd = B_KV_HEADS * hd
    scale = 1.0 / math.sqrt(hd)

    @pl.when(t == 0)
    def _():
        k_ref[:, 0:w, :] = jnp.zeros((B_KV_HEADS, w, hd), BF16)
        vt_ref[:, 0:w] = jnp.zeros((kvd, w), BF16)

    @pl.when(t > 0)
    def _():
        k_ref[:, 0:w, :] = k_ref[:, tile:tile + w, :]
        vt_ref[:, 0:w] = vt_ref[:, tile:tile + w]

    x = x_ref[0]
    xn = _rms_scale(x)
    kv = _dot((xn * g_kv_ref[...]).astype(BF16), w_kv_ref[...])
    for kh in range(B_KV_HEADS):
        k_ref[kh, w:w + tile, :] = kv[:, kh * hd:(kh + 1) * hd].astype(BF16)
    vt_ref[:, w:w + tile] = kv[:, kvd:2 * kvd].T.astype(BF16)
    h_q = (xn * g_q_ref[...]).astype(BF16)
    qt_ref[...] = (_dot_nt(w_qt_ref[...], h_q) * scale).astype(BF16)

    @pl.when(t == pl.num_programs(1) - 1)
    def _():
        k_out_ref[0] = kv[tile - w:, 0:kvd]
        v_out_ref[0] = kv[tile - w:, kvd:2 * kvd]

    first = t == 0
    sc = 64
    rc = 16
    units = [(blk, kh) for blk in range(tile // w) for kh in range(B_KV_HEADS)]

    def scores(u):
        blk, kh = units[u]
        cols = slice(blk * w, (blk + 1) * w)
        heads = range(kh * group, (kh + 1) * group)
        q4 = jnp.concatenate([qt_ref[h * hd:(h + 1) * hd, cols] for h in heads], axis=1)
        m = None
        for r in range(2 * w // sc):
            s = _dot(k_ref[kh, blk * w + r * sc:blk * w + (r + 1) * sc, :], q4)
            s = s + jnp.concatenate([bias_ref[h, r * sc:(r + 1) * sc, :] for h in heads], axis=1)
            if blk == 0 and r * sc < w:
                s = jnp.where(first, NEG, s)
            s_ref[u % 2, r * sc:(r + 1) * sc, :] = s
            for i in range(sc // 8):
                part = s[i * 8:(i + 1) * 8, :]
                m = part if m is None else jnp.maximum(m, part)
        return m

    def softmax(u, m8):
        blk, kh = units[u]
        heads = range(kh * group, (kh + 1) * group)
        sink = jnp.concatenate([jnp.full((1, w), sink_ref[0, h], F32) for h in heads], axis=1)
        m = jnp.maximum(_col_max(m8), sink)
        acc = jnp.zeros((rc, group * w), F32)
        for r in range(2 * w // rc):
            p = jnp.exp(s_ref[u % 2, r * rc:(r + 1) * rc, :] - m)
            acc = acc + p
            p_ref[u % 2, r * rc:(r + 1) * rc, :] = p.astype(BF16)
        return _col_sum(acc) + jnp.exp(sink - m)

    def weighted_values(u, denom):
        blk, kh = units[u]
        cols = slice(blk * w, (blk + 1) * w)
        keys = slice(blk * w, (blk + 2) * w)
        o = _dot(vt_ref[kh * hd:(kh + 1) * hd, keys], p_ref[u % 2]) / denom
        for g in range(group):
            h = kh * group + g
            at_ref[h * hd:(h + 1) * hd, cols] = o[:, g * w:(g + 1) * w].astype(BF16)

    m_next = scores(0)
    denom_prev = None
    for u in range(len(units)):
        m_cur = m_next
        if u + 1 < len(units):
            m_next = scores(u + 1)
        denom = softmax(u, m_cur)
        if u > 0:
            weighted_values(u - 1, denom_prev)
        denom_prev = denom
    weighted_values(len(units) - 1, denom_prev)

    y_ref[0] = x + _dot_tn(at_ref[...], w_out_ref[...])


def _swa_prompt(x, g_kv, w_kv, g_q, w_qt, sink, band_bias, w_out, tile):
    bsz, seq, d = x.shape
    tile = min(tile, seq)
    kvd = w_kv.shape[1] // 2
    w = WINDOW
    return pl.pallas_call(
        _swa_prompt_kernel,
        grid=(bsz, seq // tile),
        in_specs=[
            pl.BlockSpec((1, tile, d), lambda b, t: (b, t, 0)),
            _resident(g_kv.shape),
            _resident(w_kv.shape),
            _resident(g_q.shape),
            _resident(w_qt.shape),
            pl.BlockSpec(memory_space=pltpu.SMEM),
            _resident(band_bias.shape),
            _resident(w_out.shape),
        ],
        out_specs=[
            pl.BlockSpec((1, tile, d), lambda b, t: (b, t, 0)),
            pl.BlockSpec((1, w, kvd), lambda b, t: (b, 0, 0)),
            pl.BlockSpec((1, w, kvd), lambda b, t: (b, 0, 0)),
        ],
        out_shape=[
            jax.ShapeDtypeStruct((bsz, seq, d), F32),
            jax.ShapeDtypeStruct((bsz, w, kvd), F32),
            jax.ShapeDtypeStruct((bsz, w, kvd), F32),
        ],
        scratch_shapes=[
            pltpu.VMEM((B_KV_HEADS, tile + w, B_HEAD_DIM), BF16),
            pltpu.VMEM((kvd, tile + w), BF16),
            pltpu.VMEM((d, tile), BF16),
            pltpu.VMEM((d, tile), BF16),
            pltpu.VMEM((2, 2 * w, d // B_KV_HEADS * w // B_HEAD_DIM), F32),
            pltpu.VMEM((2, 2 * w, d // B_KV_HEADS * w // B_HEAD_DIM), BF16),
        ],
        compiler_params=_cparams("arbitrary", "arbitrary"),
        name="swa_prompt",
    )(x, g_kv, w_kv, g_q, w_qt, sink, band_bias, w_out)


def _hgrn_sample_gates_kernel(layer, x_ref, w_in_ref, a_lb_ref, g_mix_ref,
                              ft_ref, qt_ref, v_ref, g_ref):
    d = x_ref.shape[1]
    dk = d // A_HEADS
    h = (_rms_scale(x_ref[...]) * g_mix_ref[...]).astype(BF16)
    proj = _dot(h, w_in_ref[...])
    forget = _forget_gate(proj[:, d:2 * d], _lower_bound(a_lb_ref[...], layer))
    v_ref[...] = proj[:, 2 * d:3 * d]
    g_ref[...] = proj[:, 3 * d:4 * d]
    for hd in range(A_HEADS):
        ls = slice(hd * dk, (hd + 1) * dk)
        ft_ref[hd] = forget[:, ls].T
        qt_ref[hd] = proj[:, ls].T


def _hgrn_sample_gates(x, w_in, a_lb, g_mix, layer):
    m, d = x.shape
    dk = d // A_HEADS
    vm = pl.BlockSpec(memory_space=pltpu.VMEM)
    return pl.pallas_call(
        functools.partial(_hgrn_sample_gates_kernel, layer),
        in_specs=[vm, vm, vm, vm],
        out_specs=[vm, vm, vm, vm],
        out_shape=[
            jax.ShapeDtypeStruct((A_HEADS, dk, m), F32),
            jax.ShapeDtypeStruct((A_HEADS, dk, m), F32),
            jax.ShapeDtypeStruct((m, d), F32),
            jax.ShapeDtypeStruct((m, d), F32),
        ],
        compiler_params=pltpu.CompilerParams(vmem_limit_bytes=V7X_VMEM_LIMIT_BYTES),
        name="hgrn_sample_gates",
    )(x, w_in, a_lb, g_mix)


def _hgrn_sample_state_kernel(ft_ref, qt_ref, v_ref, s0_ref, s_ref, o_ref):
    i = pl.program_id(0)
    tb = s0_ref.shape[0]
    dk = s0_ref.shape[2]
    m = ft_ref.shape[2]
    lane = lax.broadcasted_iota(jnp.int32, (dk, m), 1)
    for tk in range(tb):
        pick = lane == i * tb + tk
        for hd in range(A_HEADS):
            ls = slice(hd * dk, (hd + 1) * dk)
            f = jnp.sum(jnp.where(pick, ft_ref[hd], 0.0), axis=1, keepdims=True)
            q = jnp.sum(jnp.where(pick, qt_ref[hd], 0.0), axis=1, keepdims=True)
            s_new = f * s0_ref[tk, hd] + (1.0 - f) * v_ref[tk:tk + 1, ls]
            s_ref[tk, hd] = s_new
            o_ref[tk:tk + 1, ls] = jnp.sum(q * s_new, axis=0, keepdims=True)


def _hgrn_sample_state(ft, qt, v, s0, tb):
    m, d = v.shape
    dk = d // A_HEADS
    return pl.pallas_call(
        _hgrn_sample_state_kernel,
        grid=(m // tb,),
        in_specs=[
            _resident(ft.shape),
            _resident(qt.shape),
            pl.BlockSpec((tb, d), lambda i: (i, 0)),
            pl.BlockSpec((tb, A_HEADS, dk, dk), lambda i: (i, 0, 0, 0)),
        ],
        out_specs=[
            pl.BlockSpec((tb, A_HEADS, dk, dk), lambda i: (i, 0, 0, 0)),
            pl.BlockSpec((tb, d), lambda i: (i, 0)),
        ],
        out_shape=[
            jax.ShapeDtypeStruct(s0.shape, F32),
            jax.ShapeDtypeStruct((m, d), F32),
        ],
        compiler_params=_cparams("arbitrary"),
        name="hgrn_sample_state",
    )(ft, qt, v, s0)


def _hgrn_sample_out_kernel(x_ref, o_ref, g_ref, g_norm_ref, w_out_ref, y_ref):
    d = x_ref.shape[1]
    dk = d // A_HEADS
    g = g_ref[...]
    gate = g * _sigmoid(g)
    parts = []
    for hd in range(A_HEADS):
        ls = slice(hd * dk, (hd + 1) * dk)
        parts.append(_rms_scale(o_ref[:, ls]) * g_norm_ref[...] * gate[:, ls])
    a = jnp.concatenate(parts, axis=1).astype(BF16)
    y_ref[...] = x_ref[...] + _dot(a, w_out_ref[...])


def _hgrn_sample_out(x, o, g, g_norm, w_out):
    vm = pl.BlockSpec(memory_space=pltpu.VMEM)
    return pl.pallas_call(
        _hgrn_sample_out_kernel,
        in_specs=[vm] * 5,
        out_specs=vm,
        out_shape=jax.ShapeDtypeStruct(x.shape, F32),
        compiler_params=pltpu.CompilerParams(vmem_limit_bytes=V7X_VMEM_LIMIT_BYTES),
        name="hgrn_sample_out",
    )(x, o, g, g_norm, w_out)


def _swa_sample_proj_kernel(x_ref, g_kv_ref, w_kv_ref, g_q_ref, w_qx_ref, kv_ref, qx_ref):
    scale = 1.0 / math.sqrt(B_HEAD_DIM)
    xn = _rms_scale(x_ref[...])
    kv_ref[...] = _dot((xn * g_kv_ref[...]).astype(BF16), w_kv_ref[...])
    qx_ref[...] = (_dot((xn * g_q_ref[...]).astype(BF16), w_qx_ref[...]) * scale).astype(BF16)


def _swa_sample_proj(x, g_kv, w_kv, g_q, w_qx):
    m = x.shape[0]
    vm = pl.BlockSpec(memory_space=pltpu.VMEM)
    return pl.pallas_call(
        _swa_sample_proj_kernel,
        in_specs=[vm] * 5,
        out_specs=[vm, vm],
        out_shape=[
            jax.ShapeDtypeStruct((m, w_kv.shape[1]), F32),
            jax.ShapeDtypeStruct((m, w_qx.shape[1]), BF16),
        ],
        compiler_params=pltpu.CompilerParams(vmem_limit_bytes=V7X_VMEM_LIMIT_BYTES),
        name="swa_sample_proj",
    )(x, g_kv, w_kv, g_q, w_qx)


def _swa_sample_attend_kernel(qx_ref, k_new_ref, v_new_ref, k_buf_ref, v_buf_ref, bias_ref,
                              sink_ref, k_out_ref, v_out_ref, r_ref):
    n_buf = k_buf_ref.shape[1]
    k_out_ref[:, 0:n_buf - 1, :] = k_buf_ref[:, 1:n_buf, :]
    v_out_ref[:, 0:n_buf - 1, :] = v_buf_ref[:, 1:n_buf, :]
    k_out_ref[:, n_buf - 1:n_buf, :] = k_new_ref[...]
    v_out_ref[:, n_buf - 1:n_buf, :] = v_new_ref[...]
    k = k_out_ref[...].astype(BF16)
    v = v_out_ref[...].astype(BF16)
    s = jnp.einsum("bhc,brc->bhr", qx_ref[...], k, preferred_element_type=F32)
    s = s + bias_ref[...][None]
    sink = sink_ref[...][None]
    m = jnp.maximum(jnp.max(s, axis=-1, keepdims=True), sink)
    p = jnp.exp(s - m)
    denom = jnp.sum(p, axis=-1, keepdims=True) + jnp.exp(sink - m)
    r = jnp.einsum("bhr,brc->bhc", p.astype(BF16), v, preferred_element_type=F32)
    r_ref[...] = (r / denom).astype(BF16)


def _swa_sample_attend(qx, k_new, v_new, k_buf, v_buf, bias_row, sink_col, tb):
    m, n_buf, kvd = k_buf.shape
    n_heads = qx.shape[1]
    blk3 = lambda i: (i, 0, 0)
    return pl.pallas_call(
        _swa_sample_attend_kernel,
        grid=(m // tb,),
        in_specs=[
            pl.BlockSpec((tb, n_heads, kvd), blk3),
            pl.BlockSpec((tb, 1, kvd), blk3),
            pl.BlockSpec((tb, 1, kvd), blk3),
            pl.BlockSpec((tb, n_buf, kvd), blk3),
            pl.BlockSpec((tb, n_buf, kvd), blk3),
            _resident(bias_row.shape),
            _resident(sink_col.shape),
        ],
        out_specs=[
            pl.BlockSpec((tb, n_buf, kvd), blk3),
            pl.BlockSpec((tb, n_buf, kvd), blk3),
            pl.BlockSpec((tb, n_heads, kvd), blk3),
        ],
        out_shape=[
            jax.ShapeDtypeStruct((m, n_buf, kvd), F32),
            jax.ShapeDtypeStruct((m, n_buf, kvd), F32),
            jax.ShapeDtypeStruct((m, n_heads, kvd), BF16),
        ],
        compiler_params=_cparams("arbitrary"),
        name="swa_sample_attend",
    )(qx, k_new, v_new, k_buf, v_buf, bias_row, sink_col)


def _swa_sample_out_kernel(x_ref, r_ref, w_ox_ref, y_ref):
    y_ref[...] = x_ref[...] + _dot(r_ref[...], w_ox_ref[...])


def _swa_sample_out(x, r_flat, w_ox):
    vm = pl.BlockSpec(memory_space=pltpu.VMEM)
    return pl.pallas_call(
        _swa_sample_out_kernel,
        in_specs=[vm] * 3,
        out_specs=vm,
        out_shape=jax.ShapeDtypeStruct(x.shape, F32),
        compiler_params=pltpu.CompilerParams(vmem_limit_bytes=V7X_VMEM_LIMIT_BYTES),
        name="swa_sample_out",
    )(x, r_flat, w_ox)


def _expand_heads(w_q, w_out):
    d = w_q.shape[0]
    hd = B_HEAD_DIM
    n_heads = w_q.shape[1] // hd
    group = n_heads // B_KV_HEADS
    slot = (jnp.arange(n_heads) // group)[:, None] == jnp.arange(B_KV_HEADS)[None, :]
    w_qx = jnp.where(slot[None, :, :, None], w_q.reshape(d, n_heads, 1, hd), 0)
    w_ox = jnp.where(slot[:, :, None, None], w_out.reshape(n_heads, 1, hd, -1), 0)
    kvd = B_KV_HEADS * hd
    return w_qx.reshape(d, n_heads * kvd), w_ox.reshape(n_heads * kvd, -1)


def kernel(x_prompt, x_sample, state_hgrn, cache_k_win, cache_v_win, w_a_in, a_lb, a_gnorm, w_a_out,
           g_mix, g_mlp, g_kv, w_kv, w_b_q, b_sink, w_b_out, rel_bias, w_up, w_down, g_final):
    bsz, seq, d = x_prompt.shape
    n_dec = x_sample.shape[0]
    assert x_sample.shape[1] == 1, "the sample group decodes one token per sequence"
    n_a = w_a_in.shape[0]
    n_b = w_b_q.shape[0]
    assert n_a == 1 and n_b == 1, "depth-2 trunk: one HGRN2 layer, then one attention layer"
    n_buf = cache_k_win.shape[1]
    assert n_buf == WINDOW and seq % WINDOW == 0
    kvd = B_KV_HEADS * B_HEAD_DIM
    n_heads = d // B_HEAD_DIM

    bf = lambda w: w.astype(BF16)
    row = lambda g: g.reshape(1, -1)
    w_in, w_ao = bf(w_a_in[0]), bf(w_a_out[0])
    w_kvb, w_q, w_bo = bf(w_kv), bf(w_b_q[0]), bf(w_b_out[0])
    w_qx, w_ox = _expand_heads(w_q, w_bo)
    w_up_b, w_down_b = bf(w_up), bf(w_down)
    band_bias = _band_bias(rel_bias)
    sink = b_sink[0].reshape(1, n_heads)

    x, s_prompt = _hgrn_prompt(x_prompt, w_in, a_lb, row(g_mix[0]), row(a_gnorm[0]), w_ao,
                               layer=0, tile=256)
    x = _mlp(x.reshape(bsz * seq, d), row(g_mlp[0]), w_up_b[0], w_down_b[0], None, tile=512)
    x, k_prompt, v_prompt = _swa_prompt(x.reshape(bsz, seq, d), row(g_kv), w_kvb, row(g_mix[1]),
                                        w_q.T, sink, band_bias, w_bo, tile=512)
    y_prompt = _mlp(x.reshape(bsz * seq, d), row(g_mlp[1]), w_up_b[1], w_down_b[1],
                    row(g_final), tile=512).reshape(bsz, seq, d)

    xs = x_sample.reshape(n_dec, d)
    ft, qt, v, g = _hgrn_sample_gates(xs, w_in, a_lb, row(g_mix[0]), layer=0)
    s_sample, o = _hgrn_sample_state(ft, qt, v, state_hgrn[0], tb=8)
    xs = _hgrn_sample_out(xs, o, g, row(a_gnorm[0]), w_ao)
    xs = _mlp(xs, row(g_mlp[0]), w_up_b[0], w_down_b[0], None, tile=n_dec)
    kv_new, qx = _swa_sample_proj(xs, row(g_kv), w_kvb, row(g_mix[1]), w_qx)
    bias_row = band_bias[:, 1:n_buf + 1, 0]
    k_sample, v_sample, r = _swa_sample_attend(
        qx.reshape(n_dec, n_heads, kvd),
        kv_new[:, :kvd].reshape(n_dec, 1, kvd), kv_new[:, kvd:].reshape(n_dec, 1, kvd),
        cache_k_win.reshape(n_dec, n_buf, kvd), cache_v_win.reshape(n_dec, n_buf, kvd),
        bias_row, sink.reshape(n_heads, 1), tb=8)
    xs = _swa_sample_out(xs, r.reshape(n_dec, n_heads * kvd), w_ox)
    y_sample = _mlp(xs, row(g_mlp[1]), w_up_b[1], w_down_b[1], row(g_final),
                    tile=n_dec).reshape(n_dec, 1, d)

    cache_shape = (-1, WINDOW, B_KV_HEADS, B_HEAD_DIM)
    return (y_prompt, y_sample, s_prompt[None], s_sample[None],
            k_prompt.reshape(cache_shape), v_prompt.reshape(cache_shape),
            k_sample.reshape(cache_shape), v_sample.reshape(cache_shape))
```

```python
import functools
import math

import jax
import jax.numpy as jnp
from jax import lax
from jax.experimental import pallas as pl
from jax.experimental.pallas import tpu as pltpu

F32 = jnp.float32
BF16 = jnp.bfloat16

EPS = 1e-6
NEG = -1e30
A_HEADS = 8
A_CHUNK = 64
B_HEAD_DIM = 64
B_KV_HEADS = 4
WINDOW = 128
N_BUCKETS = 32
MAX_DISTANCE = 128

V7X_VMEM_LIMIT_BYTES = 56 * 1024 * 1024


def _cparams(*semantics):
    return pltpu.CompilerParams(dimension_semantics=semantics,
                                vmem_limit_bytes=V7X_VMEM_LIMIT_BYTES)


def _resident(shape):
    nd = len(shape)
    return pl.BlockSpec(shape, lambda *_: (0,) * nd, pipeline_mode=pl.Buffered(1))


def _resident_layer(shape, layer):
    nd = len(shape)
    return pl.BlockSpec((None,) + tuple(shape[1:]), lambda *_: (layer,) + (0,) * (nd - 1),
                        pipeline_mode=pl.Buffered(1))


def _rms_scale(x):
    return x * lax.rsqrt(jnp.mean(x * x, axis=-1, keepdims=True) + EPS)


def _sigmoid(x):
    return 1.0 / (1.0 + jnp.exp(-x))


def _dot(a, b):
    return jnp.dot(a, b, preferred_element_type=F32)


def _dot_nt(a, b):
    return lax.dot_general(a, b, (((1,), (1,)), ((), ())), preferred_element_type=F32)


def _dot_tn(a, b):
    return lax.dot_general(a, b, (((0,), (0,)), ((), ())), preferred_element_type=F32)


def _lower_bound(a_lb, layer):
    m = jnp.max(a_lb, axis=0, keepdims=True)
    e = jnp.exp(a_lb - m)
    return jnp.sum(e[: layer + 1], axis=0, keepdims=True) / jnp.sum(e, axis=0, keepdims=True)


def _forget_gate(f_raw, lb):
    return lb + (1.0 - lb) * _sigmoid(f_raw)


def _hgrn_prompt_kernel(layer, x_ref, w_in_ref, a_lb_ref, g_mix_ref, g_norm_ref, w_out_ref,
                        y_ref, s_out_ref,
                        st_ref, proj_ref, lf_ref, kk_ref, b_ref, qi_ref, ki_ref, qn_ref, ks_ref, vb_ref,
                        dec_ref, att_ref, o_ref):
    t = pl.program_id(1)
    tile, d = x_ref.shape[1], x_ref.shape[2]
    dk = d // A_HEADS
    c = A_CHUNK
    n_chunks = tile // c

    @pl.when(t == 0)
    def _():
        st_ref[...] = jnp.zeros_like(st_ref)
        dec_ref[...] = jnp.zeros_like(dec_ref)

    x = x_ref[0]
    h = (_rms_scale(x) * g_mix_ref[...]).astype(BF16)
    proj_ref[...] = _dot(h, w_in_ref[...])

    lb = _lower_bound(a_lb_ref[...], layer)
    for ci in range(n_chunks):
        rows = slice(ci * c, (ci + 1) * c)
        forget = _forget_gate(proj_ref[rows, d:2 * d], lb)
        kk_ref[rows, :] = 1.0 - forget
        logf = jnp.log(forget)
        hi = logf.astype(BF16)
        r1 = logf - hi.astype(F32)
        mid = r1.astype(BF16)
        lf_ref[0, rows, :] = hi
        lf_ref[1, rows, :] = mid
        lf_ref[2, rows, :] = (r1 - mid.astype(F32)).astype(BF16)

    row = lax.broadcasted_iota(jnp.int32, (tile, tile), 0)
    col = lax.broadcasted_iota(jnp.int32, (tile, tile), 1)
    tri = jnp.where((row >= col) & (row // c == col // c), 1.0, 0.0).astype(BF16)
    b_ref[...] = _dot(tri, lf_ref[0]) + _dot(tri, lf_ref[1]) + _dot(tri, lf_ref[2])

    for ci in range(n_chunks):
        rows = slice(ci * c, (ci + 1) * c)
        b = b_ref[rows, :]
        b_mid = b_ref[ci * c + c // 2 - 1:ci * c + c // 2, :]
        b_last = b_ref[ci * c + c - 1:ci * c + c, :]
        q = proj_ref[rows, 0:d]
        k = kk_ref[rows, :]
        qi_ref[rows, :] = (q * jnp.exp(b - b_mid)).astype(BF16)
        ki_ref[rows, :] = (k * jnp.exp(b_mid - b)).astype(BF16)
        qn_ref[rows, :] = (q * jnp.exp(b)).astype(BF16)
        ks_ref[rows, :] = (k * jnp.exp(b_last - b)).astype(BF16)
        vb_ref[rows, :] = proj_ref[rows, 2 * d:3 * d].astype(BF16)
        decay = jnp.exp(b_last)
        for hd in range(A_HEADS):
            dec_ref[ci * A_HEADS + hd:ci * A_HEADS + hd + 1, :] = decay[:, hd * dk:(hd + 1) * dk]

    crow = lax.broadcasted_iota(jnp.int32, (c, c), 0)
    ccol = lax.broadcasted_iota(jnp.int32, (c, c), 1)
    causal = crow >= ccol
    for ci in range(n_chunks):
        rows = slice(ci * c, (ci + 1) * c)
        for hd in range(A_HEADS):
            ls = slice(hd * dk, (hd + 1) * dk)
            att = jnp.where(causal, _dot_nt(qi_ref[rows, ls], ki_ref[rows, ls]), 0.0)
            att_ref[ci * A_HEADS + hd] = att.astype(BF16)

    dec_t = dec_ref[...].T
    g_norm = g_norm_ref[...]
    for ci in range(n_chunks):
        rows = slice(ci * c, (ci + 1) * c)
        for hd in range(A_HEADS):
            ls = slice(hd * dk, (hd + 1) * dk)
            j = ci * A_HEADS + hd
            st = st_ref[hd]
            lhs = jnp.concatenate([qn_ref[rows, ls], att_ref[j]], axis=1)
            rhs = jnp.concatenate([st.astype(BF16), vb_ref[rows, ls]], axis=0)
            o = _dot(lhs, rhs)
            st_ref[hd] = st * dec_t[:, j:j + 1] + _dot_tn(ks_ref[rows, ls], vb_ref[rows, ls])
            o_ref[rows, ls] = _rms_scale(o) * g_norm

    g = proj_ref[:, 3 * d:4 * d]
    a = (o_ref[...] * (g * _sigmoid(g))).astype(BF16)
    y_ref[0] = x + _dot(a, w_out_ref[...])

    @pl.when(t == pl.num_programs(1) - 1)
    def _():
        s_out_ref[0] = st_ref[...]


def _hgrn_prompt(x, w_in, a_lb, g_mix, g_norm, w_out, layer, tile):
    bsz, seq, d = x.shape
    dk = d // A_HEADS
    tile = min(tile, seq)
    return pl.pallas_call(
        functools.partial(_hgrn_prompt_kernel, layer),
        grid=(bsz, seq // tile),
        in_specs=[
            pl.BlockSpec((1, tile, d), lambda b, t: (b, t, 0)),
            _resident(w_in.shape),
            _resident(a_lb.shape),
            _resident(g_mix.shape),
            _resident(g_norm.shape),
            _resident(w_out.shape),
        ],
        out_specs=[
            pl.BlockSpec((1, tile, d), lambda b, t: (b, t, 0)),
            pl.BlockSpec((1, A_HEADS, dk, dk), lambda b, t: (b, 0, 0, 0)),
        ],
        out_shape=[
            jax.ShapeDtypeStruct((bsz, seq, d), F32),
            jax.ShapeDtypeStruct((bsz, A_HEADS, dk, dk), F32),
        ],
        scratch_shapes=[
            pltpu.VMEM((A_HEADS, dk, dk), F32),
            pltpu.VMEM((tile, 4 * d), F32),
            pltpu.VMEM((3, tile, d), BF16),
            pltpu.VMEM((tile, d), F32),
            pltpu.VMEM((tile, d), F32),
            pltpu.VMEM((tile, d), BF16),
            pltpu.VMEM((tile, d), BF16),
            pltpu.VMEM((tile, d), BF16),
            pltpu.VMEM((tile, d), BF16),
            pltpu.VMEM((tile, d), BF16),
            pltpu.VMEM((dk, dk), F32),
            pltpu.VMEM((tile // A_CHUNK * A_HEADS, A_CHUNK, A_CHUNK), BF16),
            pltpu.VMEM((tile, d), F32),
        ],
        compiler_params=_cparams("arbitrary", "arbitrary"),
        name="hgrn_prompt",
    )(x, w_in, a_lb, g_mix, g_norm, w_out)


def _mlp_kernel(ff_chunk, has_final, x_ref, g_ref, w_up_ref, w_down_ref, *rest):
    if has_final:
        g_final_ref, y_ref = rest
    else:
        (y_ref,) = rest
    x = x_ref[...]
    h = (_rms_scale(x) * g_ref[...]).astype(BF16)
    acc = x
    for j in range(w_up_ref.shape[1] // ff_chunk):
        cols = slice(j * ff_chunk, (j + 1) * ff_chunk)
        u = jnp.maximum(_dot(h, w_up_ref[:, cols]), 0.0)
        acc = acc + _dot((u * u).astype(BF16), w_down_ref[cols, :])
    if has_final:
        acc = _rms_scale(acc) * g_final_ref[...]
    y_ref[...] = acc


def _mlp(x, g, w_up, w_down, layer, g_final, tile, ff_chunk=1024):
    m, d = x.shape
    tile = min(tile, m)
    has_final = g_final is not None
    in_specs = [
        pl.BlockSpec((tile, d), lambda i: (i, 0)),
        _resident(g.shape),
        _resident_layer(w_up.shape, layer),
        _resident_layer(w_down.shape, layer),
    ]
    args = [x, g, w_up, w_down]
    if has_final:
        in_specs.append(_resident(g_final.shape))
        args.append(g_final)
    return pl.pallas_call(
        functools.partial(_mlp_kernel, ff_chunk, has_final),
        grid=(m // tile,),
        in_specs=in_specs,
        out_specs=pl.BlockSpec((tile, d), lambda i: (i, 0)),
        out_shape=jax.ShapeDtypeStruct((m, d), F32),
        compiler_params=_cparams("arbitrary"),
        name="mlp_final" if has_final else "mlp",
    )(*args)


def _t5_bucket(dist):
    n = jnp.maximum(dist, 0)
    max_exact = N_BUCKETS // 2
    nf = jnp.maximum(n, 1).astype(F32)
    large = max_exact + (jnp.log(nf / max_exact) / math.log(MAX_DISTANCE / max_exact)
                         * (N_BUCKETS - max_exact)).astype(jnp.int32)
    large = jnp.minimum(large, N_BUCKETS - 1)
    return jnp.where(n < max_exact, n, large)


def _band_bias_kernel(rel_bias_ref, out_ref):
    n_heads, _, w = out_ref.shape
    kj = lax.broadcasted_iota(jnp.int32, (2 * w, w), 0)
    qi = lax.broadcasted_iota(jnp.int32, (2 * w, w), 1)
    dist = qi + w - kj
    bucket = _t5_bucket(dist)
    valid = (dist >= 0) & (dist < WINDOW)
    for hd in range(n_heads):
        bias = jnp.zeros((2 * w, w), F32)
        for bk in range(N_BUCKETS):
            bias = jnp.where(bucket == bk, rel_bias_ref[bk, hd], bias)
        out_ref[hd] = jnp.where(valid, bias, NEG)


def _band_bias(rel_bias):
    n_heads = rel_bias.shape[1]
    return pl.pallas_call(
        _band_bias_kernel,
        in_specs=[pl.BlockSpec(memory_space=pltpu.SMEM)],
        out_specs=pl.BlockSpec(memory_space=pltpu.VMEM),
        out_shape=jax.ShapeDtypeStruct((n_heads, 2 * WINDOW, WINDOW), F32),
        name="band_bias",
    )(rel_bias)


def _col_max(x):
    return jnp.max(x, axis=0, keepdims=True)


def _col_sum(x):
    return jnp.sum(x, axis=0, keepdims=True)


def _swa_prompt_kernel(x_ref, g_kv_ref, w_kv_ref, g_q_ref, w_qt_ref, sink_ref, bias_ref, w_out_ref,
                       y_ref, k_out_ref, v_out_ref, k_ref, vt_ref, qt_ref, at_ref, s_ref, p_ref):
    t = pl.program_id(1)
    tile, d = x_ref.shape[1], x_ref.shape[2]
    w = WINDOW
    hd = B_HEAD_DIM
    n_heads = d // hd
    group = n_heads // B_KV_HEADS
    kvd = B_KV_HEADS * hd
    scale = 1.0 / math.sqrt(hd)

    @pl.when(t == 0)
    def _():
        k_ref[:, 0:w, :] = jnp.zeros((B_KV_HEADS, w, hd), BF16)
        vt_ref[:, 0:w] = jnp.zeros((kvd, w), BF16)

    @pl.when(t > 0)
    def _():
        k_ref[:, 0:w, :] = k_ref[:, tile:tile + w, :]
        vt_ref[:, 0:w] = vt_ref[:, tile:tile + w]

    x = x_ref[0]
    xn = _rms_scale(x)
    kv = _dot((xn * g_kv_ref[...]).astype(BF16), w_kv_ref[...])
    for kh in range(B_KV_HEADS):
        k_ref[kh, w:w + tile, :] = kv[:, kh * hd:(kh + 1) * hd].astype(BF16)
    vt_ref[:, w:w + tile] = kv[:, kvd:2 * kvd].T.astype(BF16)
    h_q = (xn * g_q_ref[...]).astype(BF16)
    qt_ref[...] = (_dot_nt(w_qt_ref[...], h_q) * scale).astype(BF16)

    @pl.when(t == pl.num_programs(1) - 1)
    def _():
        k_out_ref[0] = kv[tile - w:, 0:kvd].T
        v_out_ref[0] = kv[tile - w:, kvd:2 * kvd].T

    first = t == 0
    sc = 64
    rc = 16
    units = [(blk, kh) for blk in range(tile // w) for kh in range(B_KV_HEADS)]

    def scores(u):
        blk, kh = units[u]
        cols = slice(blk * w, (blk + 1) * w)
        heads = range(kh * group, (kh + 1) * group)
        q4 = jnp.concatenate([qt_ref[h * hd:(h + 1) * hd, cols] for h in heads], axis=1)
        m = None
        for r in range(2 * w // sc):
            s = _dot(k_ref[kh, blk * w + r * sc:blk * w + (r + 1) * sc, :], q4)
            s = s + jnp.concatenate([bias_ref[h, r * sc:(r + 1) * sc, :] for h in heads], axis=1)
            if blk == 0 and r * sc < w:
                s = jnp.where(first, NEG, s)
            s_ref[u % 2, r * sc:(r + 1) * sc, :] = s
            for i in range(sc // 8):
                part = s[i * 8:(i + 1) * 8, :]
                m = part if m is None else jnp.maximum(m, part)
        return m

    def softmax(u, m8):
        blk, kh = units[u]
        heads = range(kh * group, (kh + 1) * group)
        sink = jnp.concatenate([jnp.full((1, w), sink_ref[0, h], F32) for h in heads], axis=1)
        m = jnp.maximum(_col_max(m8), sink)
        acc = jnp.zeros((rc, group * w), F32)
        for r in range(2 * w // rc):
            p = jnp.exp(s_ref[u % 2, r * rc:(r + 1) * rc, :] - m)
            acc = acc + p
            p_ref[u % 2, r * rc:(r + 1) * rc, :] = p.astype(BF16)
        return _col_sum(acc) + jnp.exp(sink - m)

    def weighted_values(u, denom):
        blk, kh = units[u]
        cols = slice(blk * w, (blk + 1) * w)
        keys = slice(blk * w, (blk + 2) * w)
        o = _dot(vt_ref[kh * hd:(kh + 1) * hd, keys], p_ref[u % 2]) / denom
        for g in range(group):
            h = kh * group + g
            at_ref[h * hd:(h + 1) * hd, cols] = o[:, g * w:(g + 1) * w].astype(BF16)

    m_next = scores(0)
    denom_prev = None
    for u in range(len(units)):
        m_cur = m_next
        if u + 1 < len(units):
            m_next = scores(u + 1)
        denom = softmax(u, m_cur)
        if u > 0:
            weighted_values(u - 1, denom_prev)
        denom_prev = denom
    weighted_values(len(units) - 1, denom_prev)

    y_ref[0] = x + _dot_tn(at_ref[...], w_out_ref[...])


def _swa_prompt(x, g_kv, w_kv, g_q, w_qt, sink, band_bias, w_out, tile):
    bsz, seq, d = x.shape
    tile = min(tile, seq)
    kvd = w_kv.shape[1] // 2
    w = WINDOW
    return pl.pallas_call(
        _swa_prompt_kernel,
        grid=(bsz, seq // tile),
        in_specs=[
            pl.BlockSpec((1, tile, d), lambda b, t: (b, t, 0)),
            _resident(g_kv.shape),
            _resident(w_kv.shape),
            _resident(g_q.shape),
            _resident(w_qt.shape),
            pl.BlockSpec(memory_space=pltpu.SMEM),
            _resident(band_bias.shape),
            _resident(w_out.shape),
        ],
        out_specs=[
            pl.BlockSpec((1, tile, d), lambda b, t: (b, t, 0)),
            pl.BlockSpec((1, kvd, w), lambda b, t: (b, 0, 0)),
            pl.BlockSpec((1, kvd, w), lambda b, t: (b, 0, 0)),
        ],
        out_shape=[
            jax.ShapeDtypeStruct((bsz, seq, d), F32),
            jax.ShapeDtypeStruct((bsz, kvd, w), F32),
            jax.ShapeDtypeStruct((bsz, kvd, w), F32),
        ],
        scratch_shapes=[
            pltpu.VMEM((B_KV_HEADS, tile + w, B_HEAD_DIM), BF16),
            pltpu.VMEM((kvd, tile + w), BF16),
            pltpu.VMEM((d, tile), BF16),
            pltpu.VMEM((d, tile), BF16),
            pltpu.VMEM((2, 2 * w, d // B_KV_HEADS * w // B_HEAD_DIM), F32),
            pltpu.VMEM((2, 2 * w, d // B_KV_HEADS * w // B_HEAD_DIM), BF16),
        ],
        compiler_params=_cparams("arbitrary", "arbitrary"),
        name="swa_prompt",
    )(x, g_kv, w_kv, g_q, w_qt, sink, band_bias, w_out)


def _hgrn_sample_gates_kernel(layer, x_ref, w_in_ref, a_lb_ref, g_mix_ref,
                              ft_ref, q_ref, v_ref, g_ref):
    d = x_ref.shape[1]
    dk = d // A_HEADS
    h = (_rms_scale(x_ref[...]) * g_mix_ref[...]).astype(BF16)
    proj = _dot(h, w_in_ref[...])
    forget = _forget_gate(proj[:, d:2 * d], _lower_bound(a_lb_ref[...], layer))
    q_ref[...] = proj[:, 0:d]
    v_ref[...] = proj[:, 2 * d:3 * d]
    g_ref[...] = proj[:, 3 * d:4 * d]
    for hd in range(A_HEADS):
        ft_ref[hd] = forget[:, hd * dk:(hd + 1) * dk].T


def _hgrn_sample_gates(x, w_in, a_lb, g_mix, layer):
    m, d = x.shape
    dk = d // A_HEADS
    vm = pl.BlockSpec(memory_space=pltpu.VMEM)
    return pl.pallas_call(
        functools.partial(_hgrn_sample_gates_kernel, layer),
        in_specs=[vm, vm, vm, vm],
        out_specs=[vm, vm, vm, vm],
        out_shape=[
            jax.ShapeDtypeStruct((A_HEADS, dk, m), F32),
            jax.ShapeDtypeStruct((m, d), F32),
            jax.ShapeDtypeStruct((m, d), F32),
            jax.ShapeDtypeStruct((m, d), F32),
        ],
        compiler_params=pltpu.CompilerParams(vmem_limit_bytes=V7X_VMEM_LIMIT_BYTES),
        name="hgrn_sample_gates",
    )(x, w_in, a_lb, g_mix)


def _hgrn_sample_state_kernel(ft_ref, q_ref, v_ref, s0_ref, s_ref, o_ref):
    i = pl.program_id(0)
    tb = s0_ref.shape[0]
    dk = s0_ref.shape[2]
    m = ft_ref.shape[2]
    lane = lax.broadcasted_iota(jnp.int32, (dk, m), 1)
    for tk in range(tb):
        pick = lane == i * tb + tk
        for hd in range(A_HEADS):
            ls = slice(hd * dk, (hd + 1) * dk)
            f = jnp.sum(jnp.where(pick, ft_ref[hd], 0.0), axis=1, keepdims=True)
            v = v_ref[tk:tk + 1, ls]
            s_new = v + f * (s0_ref[tk, hd] - v)
            s_ref[tk, hd] = s_new
            o = _dot(q_ref[:, ls].astype(BF16), s_new.astype(BF16))
            o_ref[tk:tk + 1, ls] = o[tk:tk + 1]


def _hgrn_sample_state(ft, q, v, s0, tb):
    m, d = v.shape
    dk = d // A_HEADS
    return pl.pallas_call(
        _hgrn_sample_state_kernel,
        grid=(m // tb,),
        in_specs=[
            _resident(ft.shape),
            pl.BlockSpec((tb, d), lambda i: (i, 0)),
            pl.BlockSpec((tb, d), lambda i: (i, 0)),
            pl.BlockSpec((tb, A_HEADS, dk, dk), lambda i: (i, 0, 0, 0)),
        ],
        out_specs=[
            pl.BlockSpec((tb, A_HEADS, dk, dk), lambda i: (i, 0, 0, 0)),
            pl.BlockSpec((tb, d), lambda i: (i, 0)),
        ],
        out_shape=[
            jax.ShapeDtypeStruct(s0.shape, F32),
            jax.ShapeDtypeStruct((m, d), F32),
        ],
        compiler_params=_cparams("arbitrary"),
        name="hgrn_sample_state",
    )(ft, q, v, s0)


def _hgrn_sample_out_kernel(x_ref, o_ref, g_ref, g_norm_ref, w_out_ref, y_ref):
    d = x_ref.shape[1]
    dk = d // A_HEADS
    g = g_ref[...]
    gate = g * _sigmoid(g)
    parts = []
    for hd in range(A_HEADS):
        ls = slice(hd * dk, (hd + 1) * dk)
        parts.append(_rms_scale(o_ref[:, ls]) * g_norm_ref[...] * gate[:, ls])
    a = jnp.concatenate(parts, axis=1).astype(BF16)
    y_ref[...] = x_ref[...] + _dot(a, w_out_ref[...])


def _hgrn_sample_out(x, o, g, g_norm, w_out):
    vm = pl.BlockSpec(memory_space=pltpu.VMEM)
    return pl.pallas_call(
        _hgrn_sample_out_kernel,
        in_specs=[vm] * 5,
        out_specs=vm,
        out_shape=jax.ShapeDtypeStruct(x.shape, F32),
        compiler_params=pltpu.CompilerParams(vmem_limit_bytes=V7X_VMEM_LIMIT_BYTES),
        name="hgrn_sample_out",
    )(x, o, g, g_norm, w_out)


def _swa_sample_proj_kernel(x_ref, g_kv_ref, w_kvt_ref, g_q_ref, w_q_ref, kvt_ref, q_ref):
    scale = 1.0 / math.sqrt(B_HEAD_DIM)
    xn = _rms_scale(x_ref[...])
    kvt_ref[...] = _dot_nt(w_kvt_ref[...], (xn * g_kv_ref[...]).astype(BF16))
    q_ref[...] = (_dot((xn * g_q_ref[...]).astype(BF16), w_q_ref[...]) * scale).astype(BF16)


def _swa_sample_proj(x, g_kv, w_kvt, g_q, w_q):
    m = x.shape[0]
    vm = pl.BlockSpec(memory_space=pltpu.VMEM)
    return pl.pallas_call(
        _swa_sample_proj_kernel,
        in_specs=[vm] * 5,
        out_specs=[vm, vm],
        out_shape=[
            jax.ShapeDtypeStruct((w_kvt.shape[0], m), F32),
            jax.ShapeDtypeStruct((m, w_q.shape[1]), BF16),
        ],
        compiler_params=pltpu.CompilerParams(vmem_limit_bytes=V7X_VMEM_LIMIT_BYTES),
        name="swa_sample_proj",
    )(x, g_kv, w_kvt, g_q, w_q)


def _swa_sample_attend_kernel(q_ref, kvt_new_ref, kt_buf_ref, vt_buf_ref, bias_ref, sink_ref,
                              kt_out_ref, vt_out_ref, a_ref):
    i = pl.program_id(0)
    tb, kvd, n_buf = kt_buf_ref.shape
    hd = B_HEAD_DIM
    n_heads = q_ref.shape[0] // tb
    group = n_heads // B_KV_HEADS
    m_tok = kvt_new_ref.shape[1]
    tok_lane = lax.broadcasted_iota(jnp.int32, (2 * kvd, m_tok), 1)
    key_lane = lax.broadcasted_iota(jnp.int32, (kvd, n_buf), 1)
    for tk in range(tb):
        new_col = jnp.sum(jnp.where(tok_lane == i * tb + tk, kvt_new_ref[...], 0.0),
                          axis=1, keepdims=True)
        for buf_ref, out_ref, rows in ((kt_buf_ref, kt_out_ref, slice(0, kvd)),
                                       (vt_buf_ref, vt_out_ref, slice(kvd, 2 * kvd))):
            slid = pltpu.roll(buf_ref[tk], n_buf - 1, 1)
            out_ref[tk] = jnp.where(key_lane == n_buf - 1, new_col[rows], slid)

    e_row = lax.broadcasted_iota(jnp.int32, (hd, kvd), 0)
    e_col = lax.broadcasted_iota(jnp.int32, (hd, kvd), 1)
    spread = jnp.where(e_col % hd == e_row, 1.0, 0.0).astype(BF16)
    head = lax.broadcasted_iota(jnp.int32, (tb, n_heads, kvd), 1)
    slot = lax.broadcasted_iota(jnp.int32, (tb, n_heads, kvd), 2)
    own_slot = head // group == slot // hd
    qx = _dot(q_ref[...], spread).reshape(tb, n_heads, kvd)
    qx = jnp.where(own_slot, qx, 0.0).astype(BF16)

    kt = kt_out_ref[...].astype(BF16)
    vt = vt_out_ref[...].astype(BF16)
    s = jnp.einsum("bhc,bcr->bhr", qx, kt, preferred_element_type=F32)
    s = s + bias_ref[...][None]
    sink = sink_ref[...][None]
    m = jnp.maximum(jnp.max(s, axis=-1, keepdims=True), sink)
    p = jnp.exp(s - m)
    denom = jnp.sum(p, axis=-1, keepdims=True) + jnp.exp(sink - m)
    r = jnp.einsum("bhr,bcr->bhc", p.astype(BF16), vt, preferred_element_type=F32) / denom
    r = jnp.where(own_slot, r, 0.0).astype(BF16).reshape(tb * n_heads, kvd)
    a_ref[...] = _dot_nt(r, spread).astype(BF16)


def _swa_sample_attend(q_rows, kvt_new, kt_buf, vt_buf, bias_row, sink_col, tb):
    m, kvd, n_buf = kt_buf.shape
    n_heads = q_rows.shape[0] // m
    blk3 = lambda i: (i, 0, 0)
    return pl.pallas_call(
        _swa_sample_attend_kernel,
        grid=(m // tb,),
        in_specs=[
            pl.BlockSpec((tb * n_heads, B_HEAD_DIM), lambda i: (i, 0)),
            _resident(kvt_new.shape),
            pl.BlockSpec((tb, kvd, n_buf), blk3),
            pl.BlockSpec((tb, kvd, n_buf), blk3),
            _resident(bias_row.shape),
            _resident(sink_col.shape),
        ],
        out_specs=[
            pl.BlockSpec((tb, kvd, n_buf), blk3),
            pl.BlockSpec((tb, kvd, n_buf), blk3),
            pl.BlockSpec((tb * n_heads, B_HEAD_DIM), lambda i: (i, 0)),
        ],
        out_shape=[
            jax.ShapeDtypeStruct((m, kvd, n_buf), F32),
            jax.ShapeDtypeStruct((m, kvd, n_buf), F32),
            jax.ShapeDtypeStruct((m * n_heads, B_HEAD_DIM), BF16),
        ],
        compiler_params=_cparams("arbitrary"),
        name="swa_sample_attend",
    )(q_rows, kvt_new, kt_buf, vt_buf, bias_row, sink_col)


def _swa_sample_out_kernel(x_ref, a_ref, w_out_ref, y_ref):
    y_ref[...] = x_ref[...] + _dot(a_ref[...], w_out_ref[...])


def _swa_sample_out(x, a, w_out):
    vm = pl.BlockSpec(memory_space=pltpu.VMEM)
    return pl.pallas_call(
        _swa_sample_out_kernel,
        in_specs=[vm] * 3,
        out_specs=vm,
        out_shape=jax.ShapeDtypeStruct(x.shape, F32),
        compiler_params=pltpu.CompilerParams(vmem_limit_bytes=V7X_VMEM_LIMIT_BYTES),
        name="swa_sample_out",
    )(x, a, w_out)


def _cache_keys_minor(cache):
    b, n, kvh, hd = cache.shape
    return jnp.transpose(cache, (0, 2, 3, 1)).reshape(b, kvh * hd, n)


def _cache_keys_major(cache_t):
    b, kvd, n = cache_t.shape
    return jnp.transpose(cache_t.reshape(b, B_KV_HEADS, kvd // B_KV_HEADS, n), (0, 3, 1, 2))


def kernel(x_prompt, x_sample, state_hgrn, cache_k_win, cache_v_win, w_a_in, a_lb, a_gnorm, w_a_out,
           g_mix, g_mlp, g_kv, w_kv, w_b_q, b_sink, w_b_out, rel_bias, w_up, w_down, g_final):
    bsz, seq, d = x_prompt.shape
    n_dec = x_sample.shape[0]
    assert x_sample.shape[1] == 1, "the sample group decodes one token per sequence"
    n_a = w_a_in.shape[0]
    n_b = w_b_q.shape[0]
    assert n_a == 1 and n_b == 1, "depth-2 trunk: one HGRN2 layer, then one attention layer"
    n_buf = cache_k_win.shape[1]
    assert n_buf == WINDOW and seq % WINDOW == 0
    kvd = B_KV_HEADS * B_HEAD_DIM
    n_heads = d // B_HEAD_DIM

    bf = lambda w: w.astype(BF16)
    row = lambda g: g.reshape(1, -1)
    w_in, w_ao = bf(w_a_in[0]), bf(w_a_out[0])
    w_kvb, w_q, w_bo = bf(w_kv), bf(w_b_q[0]), bf(w_b_out[0])
    w_up_b, w_down_b = bf(w_up), bf(w_down)
    band_bias = _band_bias(rel_bias)
    sink = b_sink[0].reshape(1, n_heads)

    x, s_prompt = _hgrn_prompt(x_prompt, w_in, a_lb, row(g_mix[0]), row(a_gnorm[0]), w_ao,
                               layer=0, tile=256)
    x = _mlp(x.reshape(bsz * seq, d), row(g_mlp[0]), w_up_b, w_down_b, 0, None, tile=512)
    x, kt_prompt, vt_prompt = _swa_prompt(x.reshape(bsz, seq, d), row(g_kv), w_kvb,
                                          row(g_mix[1]), w_q.T, sink, band_bias, w_bo, tile=512)
    y_prompt = _mlp(x.reshape(bsz * seq, d), row(g_mlp[1]), w_up_b, w_down_b, 1,
                    row(g_final), tile=512).reshape(bsz, seq, d)

    xs = x_sample.reshape(n_dec, d)
    ft, q, v, g = _hgrn_sample_gates(xs, w_in, a_lb, row(g_mix[0]), layer=0)
    s_sample, o = _hgrn_sample_state(ft, q, v, state_hgrn[0], tb=8)
    xs = _hgrn_sample_out(xs, o, g, row(a_gnorm[0]), w_ao)
    xs = _mlp(xs, row(g_mlp[0]), w_up_b, w_down_b, 0, None, tile=n_dec)
    kvt_new, q = _swa_sample_proj(xs, row(g_kv), w_kvb.T, row(g_mix[1]), w_q)
    bias_row = band_bias[:, 1:n_buf + 1, 0]
    kt_sample, vt_sample, a = _swa_sample_attend(
        q.reshape(n_dec * n_heads, B_HEAD_DIM), kvt_new,
        _cache_keys_minor(cache_k_win), _cache_keys_minor(cache_v_win),
        bias_row, sink.reshape(n_heads, 1), tb=8)
    xs = _swa_sample_out(xs, a.reshape(n_dec, d), w_bo)
    y_sample = _mlp(xs, row(g_mlp[1]), w_up_b, w_down_b, 1, row(g_final),
                    tile=n_dec).reshape(n_dec, 1, d)

    return (y_prompt, y_sample, s_prompt[None], s_sample[None],
            _cache_keys_major(kt_prompt), _cache_keys_major(vt_prompt),
            _cache_keys_major(kt_sample), _cache_keys_major(vt_sample))
```

```python
import functools
import math

import jax
import jax.numpy as jnp
from jax import lax
from jax.experimental import pallas as pl
from jax.experimental.pallas import tpu as pltpu

F32 = jnp.float32
BF16 = jnp.bfloat16

EPS = 1e-6
NEG = -1e30
A_HEADS = 8
A_CHUNK = 64
B_HEAD_DIM = 64
B_KV_HEADS = 4
WINDOW = 128
N_BUCKETS = 32
MAX_DISTANCE = 128

V7X_VMEM_LIMIT_BYTES = 56 * 1024 * 1024
V7X_MXU_DIM = 256


def _cparams(*semantics):
    return pltpu.CompilerParams(dimension_semantics=semantics,
                                vmem_limit_bytes=V7X_VMEM_LIMIT_BYTES)


def _resident(shape):
    nd = len(shape)
    return pl.BlockSpec(shape, lambda *_: (0,) * nd, pipeline_mode=pl.Buffered(1))


def _resident_layer(shape, layer):
    nd = len(shape)
    return pl.BlockSpec((None,) + tuple(shape[1:]), lambda *_: (layer,) + (0,) * (nd - 1),
                        pipeline_mode=pl.Buffered(1))


def _rms_scale(x):
    return x * lax.rsqrt(jnp.mean(x * x, axis=-1, keepdims=True) + EPS)


def _sigmoid(x):
    return 1.0 / (1.0 + jnp.exp(-x))


def _dot(a, b):
    return jnp.dot(a, b, preferred_element_type=F32)


def _dot_nt(a, b):
    return lax.dot_general(a, b, (((1,), (1,)), ((), ())), preferred_element_type=F32)


def _dot_tn(a, b):
    return lax.dot_general(a, b, (((0,), (0,)), ((), ())), preferred_element_type=F32)


def _lower_bound(a_lb, layer):
    m = jnp.max(a_lb, axis=0, keepdims=True)
    e = jnp.exp(a_lb - m)
    return jnp.sum(e[: layer + 1], axis=0, keepdims=True) / jnp.sum(e, axis=0, keepdims=True)


def _forget_gate(f_raw, lb):
    return lb + (1.0 - lb) * _sigmoid(f_raw)


def _hgrn_prompt_kernel(layer, x_ref, w_in_ref, a_lb_ref, g_mix_ref, g_norm_ref, w_out_ref,
                        y_ref, s_out_ref,
                        st_ref, proj_ref, lf_ref, kk_ref, b_ref, qi_ref, ki_ref, qn_ref, ks_ref, vb_ref,
                        dec_ref, att_ref, o_ref):
    t = pl.program_id(1)
    tile, d = x_ref.shape[1], x_ref.shape[2]
    dk = d // A_HEADS
    c = A_CHUNK
    n_chunks = tile // c

    @pl.when(t == 0)
    def _():
        st_ref[...] = jnp.zeros_like(st_ref)
        dec_ref[...] = jnp.zeros_like(dec_ref)

    x = x_ref[0]
    h = (_rms_scale(x) * g_mix_ref[...]).astype(BF16)
    proj_ref[...] = _dot(h, w_in_ref[...])

    lb = _lower_bound(a_lb_ref[...], layer)
    for ci in range(n_chunks):
        rows = slice(ci * c, (ci + 1) * c)
        forget = _forget_gate(proj_ref[rows, d:2 * d], lb)
        kk_ref[rows, :] = 1.0 - forget
        logf = jnp.log(forget)
        hi = logf.astype(BF16)
        r1 = logf - hi.astype(F32)
        mid = r1.astype(BF16)
        lf_ref[0, rows, :] = hi
        lf_ref[1, rows, :] = mid
        lf_ref[2, rows, :] = (r1 - mid.astype(F32)).astype(BF16)

    span = min(tile, V7X_MXU_DIM)
    row = lax.broadcasted_iota(jnp.int32, (span, span), 0)
    col = lax.broadcasted_iota(jnp.int32, (span, span), 1)
    tri = jnp.where((row >= col) & (row // c == col // c), 1.0, 0.0).astype(BF16)
    for r0 in range(0, tile, span):
        rows = slice(r0, r0 + span)
        b_ref[rows, :] = (_dot(tri, lf_ref[0, rows, :]) + _dot(tri, lf_ref[1, rows, :])
                          + _dot(tri, lf_ref[2, rows, :]))

    for ci in range(n_chunks):
        rows = slice(ci * c, (ci + 1) * c)
        b = b_ref[rows, :]
        b_mid = b_ref[ci * c + c // 2 - 1:ci * c + c // 2, :]
        b_last = b_ref[ci * c + c - 1:ci * c + c, :]
        q = proj_ref[rows, 0:d]
        k = kk_ref[rows, :]
        qi_ref[rows, :] = (q * jnp.exp(b - b_mid)).astype(BF16)
        ki_ref[rows, :] = (k * jnp.exp(b_mid - b)).astype(BF16)
        qn_ref[rows, :] = (q * jnp.exp(b)).astype(BF16)
        ks_ref[rows, :] = (k * jnp.exp(b_last - b)).astype(BF16)
        vb_ref[rows, :] = proj_ref[rows, 2 * d:3 * d].astype(BF16)
        decay = jnp.exp(b_last)
        for hd in range(A_HEADS):
            dec_ref[ci * A_HEADS + hd:ci * A_HEADS + hd + 1, :] = decay[:, hd * dk:(hd + 1) * dk]

    crow = lax.broadcasted_iota(jnp.int32, (c, c), 0)
    ccol = lax.broadcasted_iota(jnp.int32, (c, c), 1)
    causal = crow >= ccol
    for ci in range(n_chunks):
        rows = slice(ci * c, (ci + 1) * c)
        for hd in range(A_HEADS):
            ls = slice(hd * dk, (hd + 1) * dk)
            att = jnp.where(causal, _dot_nt(qi_ref[rows, ls], ki_ref[rows, ls]), 0.0)
            att_ref[ci * A_HEADS + hd] = att.astype(BF16)

    dec_t = dec_ref[...].T
    g_norm = g_norm_ref[...]
    for ci in range(n_chunks):
        rows = slice(ci * c, (ci + 1) * c)
        for hd in range(A_HEADS):
            ls = slice(hd * dk, (hd + 1) * dk)
            j = ci * A_HEADS + hd
            st = st_ref[hd]
            lhs = jnp.concatenate([qn_ref[rows, ls], att_ref[j]], axis=1)
            rhs = jnp.concatenate([st.astype(BF16), vb_ref[rows, ls]], axis=0)
            o = _dot(lhs, rhs)
            st_ref[hd] = st * dec_t[:, j:j + 1] + _dot_tn(ks_ref[rows, ls], vb_ref[rows, ls])
            o_ref[rows, ls] = _rms_scale(o) * g_norm

    g = proj_ref[:, 3 * d:4 * d]
    a = (o_ref[...] * (g * _sigmoid(g))).astype(BF16)
    y_ref[0] = x + _dot(a, w_out_ref[...])

    @pl.when(t == pl.num_programs(1) - 1)
    def _():
        s_out_ref[0] = st_ref[...]


def _hgrn_prompt(x, w_in, a_lb, g_mix, g_norm, w_out, layer, tile):
    bsz, seq, d = x.shape
    dk = d // A_HEADS
    tile = min(tile, seq)
    return pl.pallas_call(
        functools.partial(_hgrn_prompt_kernel, layer),
        grid=(bsz, seq // tile),
        in_specs=[
            pl.BlockSpec((1, tile, d), lambda b, t: (b, t, 0)),
            _resident(w_in.shape),
            _resident(a_lb.shape),
            _resident(g_mix.shape),
            _resident(g_norm.shape),
            _resident(w_out.shape),
        ],
        out_specs=[
            pl.BlockSpec((1, tile, d), lambda b, t: (b, t, 0)),
            pl.BlockSpec((1, A_HEADS, dk, dk), lambda b, t: (b, 0, 0, 0)),
        ],
        out_shape=[
            jax.ShapeDtypeStruct((bsz, seq, d), F32),
            jax.ShapeDtypeStruct((bsz, A_HEADS, dk, dk), F32),
        ],
        scratch_shapes=[
            pltpu.VMEM((A_HEADS, dk, dk), F32),
            pltpu.VMEM((tile, 4 * d), F32),
            pltpu.VMEM((3, tile, d), BF16),
            pltpu.VMEM((tile, d), F32),
            pltpu.VMEM((tile, d), F32),
            pltpu.VMEM((tile, d), BF16),
            pltpu.VMEM((tile, d), BF16),
            pltpu.VMEM((tile, d), BF16),
            pltpu.VMEM((tile, d), BF16),
            pltpu.VMEM((tile, d), BF16),
            pltpu.VMEM((dk, dk), F32),
            pltpu.VMEM((tile // A_CHUNK * A_HEADS, A_CHUNK, A_CHUNK), BF16),
            pltpu.VMEM((tile, d), F32),
        ],
        compiler_params=_cparams("arbitrary", "arbitrary"),
        name="hgrn_prompt",
    )(x, w_in, a_lb, g_mix, g_norm, w_out)


def _mlp_kernel(ff_chunk, has_final, x_ref, g_ref, w_up_ref, w_down_ref, *rest):
    if has_final:
        g_final_ref, y_ref = rest
    else:
        (y_ref,) = rest
    x = x_ref[...]
    h = (_rms_scale(x) * g_ref[...]).astype(BF16)
    acc = x
    for j in range(w_up_ref.shape[1] // ff_chunk):
        cols = slice(j * ff_chunk, (j + 1) * ff_chunk)
        u = jnp.maximum(_dot(h, w_up_ref[:, cols]), 0.0)
        acc = acc + _dot((u * u).astype(BF16), w_down_ref[cols, :])
    if has_final:
        acc = _rms_scale(acc) * g_final_ref[...]
    y_ref[...] = acc


def _mlp(x, g, w_up, w_down, layer, g_final, tile, ff_chunk=1024):
    m, d = x.shape
    tile = min(tile, m)
    has_final = g_final is not None
    in_specs = [
        pl.BlockSpec((tile, d), lambda i: (i, 0)),
        _resident(g.shape),
        _resident_layer(w_up.shape, layer),
        _resident_layer(w_down.shape, layer),
    ]
    args = [x, g, w_up, w_down]
    if has_final:
        in_specs.append(_resident(g_final.shape))
        args.append(g_final)
    return pl.pallas_call(
        functools.partial(_mlp_kernel, ff_chunk, has_final),
        grid=(m // tile,),
        in_specs=in_specs,
        out_specs=pl.BlockSpec((tile, d), lambda i: (i, 0)),
        out_shape=jax.ShapeDtypeStruct((m, d), F32),
        compiler_params=_cparams("arbitrary"),
        name="mlp_final" if has_final else "mlp",
    )(*args)


def _t5_bucket(dist):
    n = jnp.maximum(dist, 0)
    max_exact = N_BUCKETS // 2
    nf = jnp.maximum(n, 1).astype(F32)
    large = max_exact + (jnp.log(nf / max_exact) / math.log(MAX_DISTANCE / max_exact)
                         * (N_BUCKETS - max_exact)).astype(jnp.int32)
    large = jnp.minimum(large, N_BUCKETS - 1)
    return jnp.where(n < max_exact, n, large)


def _band_bias_kernel(rel_bias_ref, out_ref):
    n_heads, _, w = out_ref.shape
    kj = lax.broadcasted_iota(jnp.int32, (2 * w, w), 0)
    qi = lax.broadcasted_iota(jnp.int32, (2 * w, w), 1)
    dist = qi + w - kj
    bucket = _t5_bucket(dist)
    valid = (dist >= 0) & (dist < WINDOW)
    for hd in range(n_heads):
        bias = jnp.zeros((2 * w, w), F32)
        for bk in range(N_BUCKETS):
            bias = jnp.where(bucket == bk, rel_bias_ref[bk, hd], bias)
        out_ref[hd] = jnp.where(valid, bias, NEG)


def _band_bias(rel_bias):
    n_heads = rel_bias.shape[1]
    return pl.pallas_call(
        _band_bias_kernel,
        in_specs=[pl.BlockSpec(memory_space=pltpu.SMEM)],
        out_specs=pl.BlockSpec(memory_space=pltpu.VMEM),
        out_shape=jax.ShapeDtypeStruct((n_heads, 2 * WINDOW, WINDOW), F32),
        name="band_bias",
    )(rel_bias)


def _col_max(x):
    return jnp.max(x, axis=0, keepdims=True)


def _col_sum(x):
    return jnp.sum(x, axis=0, keepdims=True)


def _swa_prompt_kernel(x_ref, g_kv_ref, w_kv_ref, g_q_ref, w_qt_ref, sink_ref, bias_ref, w_out_ref,
                       y_ref, k_out_ref, v_out_ref, k_ref, vt_ref, qt_ref, at_ref, s_ref, p_ref):
    t = pl.program_id(1)
    tile, d = x_ref.shape[1], x_ref.shape[2]
    w = WINDOW
    hd = B_HEAD_DIM
    n_heads = d // hd
    group = n_heads // B_KV_HEADS
    kvd = B_KV_HEADS * hd
    scale = 1.0 / math.sqrt(hd)

    @pl.when(t == 0)
    def _():
        k_ref[:, 0:w, :] = jnp.zeros((B_KV_HEADS, w, hd), BF16)
        vt_ref[:, 0:w] = jnp.zeros((kvd, w), BF16)

    @pl.when(t > 0)
    def _():
        k_ref[:, 0:w, :] = k_ref[:, tile:tile + w, :]
        vt_ref[:, 0:w] = vt_ref[:, tile:tile + w]

    x = x_ref[0]
    xn = _rms_scale(x)
    kv = _dot((xn * g_kv_ref[...]).astype(BF16), w_kv_ref[...])
    for kh in range(B_KV_HEADS):
        k_ref[kh, w:w + tile, :] = kv[:, kh * hd:(kh + 1) * hd].astype(BF16)
    vt_ref[:, w:w + tile] = kv[:, kvd:2 * kvd].T.astype(BF16)
    h_q = (xn * g_q_ref[...]).astype(BF16)
    qt_ref[...] = (_dot_nt(w_qt_ref[...], h_q) * scale).astype(BF16)

    @pl.when(t == pl.num_programs(1) - 1)
    def _():
        k_out_ref[0] = kv[tile - w:, 0:kvd].T
        v_out_ref[0] = kv[tile - w:, kvd:2 * kvd].T

    first = t == 0
    sc = 64
    rc = 16
    units = [(blk, kh) for blk in range(tile // w) for kh in range(B_KV_HEADS)]

    def scores(u):
        blk, kh = units[u]
        cols = slice(blk * w, (blk + 1) * w)
        heads = range(kh * group, (kh + 1) * group)
        q4 = jnp.concatenate([qt_ref[h * hd:(h + 1) * hd, cols] for h in heads], axis=1)
        m = None
        for r in range(2 * w // sc):
            s = _dot(k_ref[kh, blk * w + r * sc:blk * w + (r + 1) * sc, :], q4)
            s = s + jnp.concatenate([bias_ref[h, r * sc:(r + 1) * sc, :] for h in heads], axis=1)
            if blk == 0 and r * sc < w:
                s = jnp.where(first, NEG, s)
            s_ref[u % 2, r * sc:(r + 1) * sc, :] = s
            for i in range(sc // 8):
                part = s[i * 8:(i + 1) * 8, :]
                m = part if m is None else jnp.maximum(m, part)
        return m

    def softmax(u, m8):
        blk, kh = units[u]
        heads = range(kh * group, (kh + 1) * group)
        sink = jnp.concatenate([jnp.full((1, w), sink_ref[0, h], F32) for h in heads], axis=1)
        m = jnp.maximum(_col_max(m8), sink)
        acc = jnp.zeros((rc, group * w), F32)
        for r in range(2 * w // rc):
            p = jnp.exp(s_ref[u % 2, r * rc:(r + 1) * rc, :] - m)
            acc = acc + p
            p_ref[u % 2, r * rc:(r + 1) * rc, :] = p.astype(BF16)
        return _col_sum(acc) + jnp.exp(sink - m)

    def weighted_values(u, denom):
        blk, kh = units[u]
        cols = slice(blk * w, (blk + 1) * w)
        keys = slice(blk * w, (blk + 2) * w)
        o = _dot(vt_ref[kh * hd:(kh + 1) * hd, keys], p_ref[u % 2]) / denom
        for g in range(group):
            h = kh * group + g
            at_ref[h * hd:(h + 1) * hd, cols] = o[:, g * w:(g + 1) * w].astype(BF16)

    m_next = scores(0)
    denom_prev = None
    for u in range(len(units)):
        m_cur = m_next
        if u + 1 < len(units):
            m_next = scores(u + 1)
        denom = softmax(u, m_cur)
        if u > 0:
            weighted_values(u - 1, denom_prev)
        denom_prev = denom
    weighted_values(len(units) - 1, denom_prev)

    y_ref[0] = x + _dot_tn(at_ref[...], w_out_ref[...])


def _swa_prompt(x, g_kv, w_kv, g_q, w_qt, sink, band_bias, w_out, tile):
    bsz, seq, d = x.shape
    tile = min(tile, seq)
    kvd = w_kv.shape[1] // 2
    w = WINDOW
    return pl.pallas_call(
        _swa_prompt_kernel,
        grid=(bsz, seq // tile),
        in_specs=[
            pl.BlockSpec((1, tile, d), lambda b, t: (b, t, 0)),
            _resident(g_kv.shape),
            _resident(w_kv.shape),
            _resident(g_q.shape),
            _resident(w_qt.shape),
            pl.BlockSpec(memory_space=pltpu.SMEM),
            _resident(band_bias.shape),
            _resident(w_out.shape),
        ],
        out_specs=[
            pl.BlockSpec((1, tile, d), lambda b, t: (b, t, 0)),
            pl.BlockSpec((1, kvd, w), lambda b, t: (b, 0, 0)),
            pl.BlockSpec((1, kvd, w), lambda b, t: (b, 0, 0)),
        ],
        out_shape=[
            jax.ShapeDtypeStruct((bsz, seq, d), F32),
            jax.ShapeDtypeStruct((bsz, kvd, w), F32),
            jax.ShapeDtypeStruct((bsz, kvd, w), F32),
        ],
        scratch_shapes=[
            pltpu.VMEM((B_KV_HEADS, tile + w, B_HEAD_DIM), BF16),
            pltpu.VMEM((kvd, tile + w), BF16),
            pltpu.VMEM((d, tile), BF16),
            pltpu.VMEM((d, tile), BF16),
            pltpu.VMEM((2, 2 * w, d // B_KV_HEADS * w // B_HEAD_DIM), F32),
            pltpu.VMEM((2, 2 * w, d // B_KV_HEADS * w // B_HEAD_DIM), BF16),
        ],
        compiler_params=_cparams("arbitrary", "arbitrary"),
        name="swa_prompt",
    )(x, g_kv, w_kv, g_q, w_qt, sink, band_bias, w_out)


def _hgrn_sample_gates_kernel(layer, x_ref, w_in_ref, a_lb_ref, g_mix_ref,
                              ft_ref, q_ref, v_ref, g_ref):
    d = x_ref.shape[1]
    dk = d // A_HEADS
    h = (_rms_scale(x_ref[...]) * g_mix_ref[...]).astype(BF16)
    proj = _dot(h, w_in_ref[...])
    forget = _forget_gate(proj[:, d:2 * d], _lower_bound(a_lb_ref[...], layer))
    q_ref[...] = proj[:, 0:d]
    v_ref[...] = proj[:, 2 * d:3 * d]
    g_ref[...] = proj[:, 3 * d:4 * d]
    for hd in range(A_HEADS):
        ft_ref[hd] = forget[:, hd * dk:(hd + 1) * dk].T


def _hgrn_sample_gates(x, w_in, a_lb, g_mix, layer):
    m, d = x.shape
    dk = d // A_HEADS
    vm = pl.BlockSpec(memory_space=pltpu.VMEM)
    return pl.pallas_call(
        functools.partial(_hgrn_sample_gates_kernel, layer),
        in_specs=[vm, vm, vm, vm],
        out_specs=[vm, vm, vm, vm],
        out_shape=[
            jax.ShapeDtypeStruct((A_HEADS, dk, m), F32),
            jax.ShapeDtypeStruct((m, d), F32),
            jax.ShapeDtypeStruct((m, d), F32),
            jax.ShapeDtypeStruct((m, d), F32),
        ],
        compiler_params=pltpu.CompilerParams(vmem_limit_bytes=V7X_VMEM_LIMIT_BYTES),
        name="hgrn_sample_gates",
    )(x, w_in, a_lb, g_mix)


def _hgrn_sample_state_kernel(ft_ref, q_ref, v_ref, s0_ref, s_ref, o_ref):
    i = pl.program_id(0)
    tb = s0_ref.shape[0]
    dk = s0_ref.shape[2]
    m = ft_ref.shape[2]
    lane = lax.broadcasted_iota(jnp.int32, (dk, m), 1)
    for tk in range(tb):
        pick = lane == i * tb + tk
        for hd in range(A_HEADS):
            ls = slice(hd * dk, (hd + 1) * dk)
            f = jnp.sum(jnp.where(pick, ft_ref[hd], 0.0), axis=1, keepdims=True)
            v = v_ref[tk:tk + 1, ls]
            s_new = v + f * (s0_ref[tk, hd] - v)
            s_ref[tk, hd] = s_new
            o = _dot(q_ref[:, ls].astype(BF16), s_new.astype(BF16))
            o_ref[tk:tk + 1, ls] = o[tk:tk + 1]


def _hgrn_sample_state(ft, q, v, s0, tb):
    m, d = v.shape
    dk = d // A_HEADS
    return pl.pallas_call(
        _hgrn_sample_state_kernel,
        grid=(m // tb,),
        in_specs=[
            _resident(ft.shape),
            pl.BlockSpec((tb, d), lambda i: (i, 0)),
            pl.BlockSpec((tb, d), lambda i: (i, 0)),
            pl.BlockSpec((tb, A_HEADS, dk, dk), lambda i: (i, 0, 0, 0)),
        ],
        out_specs=[
            pl.BlockSpec((tb, A_HEADS, dk, dk), lambda i: (i, 0, 0, 0)),
            pl.BlockSpec((tb, d), lambda i: (i, 0)),
        ],
        out_shape=[
            jax.ShapeDtypeStruct(s0.shape, F32),
            jax.ShapeDtypeStruct((m, d), F32),
        ],
        compiler_params=_cparams("arbitrary"),
        name="hgrn_sample_state",
    )(ft, q, v, s0)


def _hgrn_sample_out_kernel(x_ref, o_ref, g_ref, g_norm_ref, w_out_ref, y_ref):
    d = x_ref.shape[1]
    dk = d // A_HEADS
    g = g_ref[...]
    gate = g * _sigmoid(g)
    parts = []
    for hd in range(A_HEADS):
        ls = slice(hd * dk, (hd + 1) * dk)
        parts.append(_rms_scale(o_ref[:, ls]) * g_norm_ref[...] * gate[:, ls])
    a = jnp.concatenate(parts, axis=1).astype(BF16)
    y_ref[...] = x_ref[...] + _dot(a, w_out_ref[...])


def _hgrn_sample_out(x, o, g, g_norm, w_out):
    vm = pl.BlockSpec(memory_space=pltpu.VMEM)
    return pl.pallas_call(
        _hgrn_sample_out_kernel,
        in_specs=[vm] * 5,
        out_specs=vm,
        out_shape=jax.ShapeDtypeStruct(x.shape, F32),
        compiler_params=pltpu.CompilerParams(vmem_limit_bytes=V7X_VMEM_LIMIT_BYTES),
        name="hgrn_sample_out",
    )(x, o, g, g_norm, w_out)


def _swa_sample_proj_kernel(x_ref, g_kv_ref, w_kvt_ref, g_q_ref, w_q_ref, kvt_ref, q_ref):
    scale = 1.0 / math.sqrt(B_HEAD_DIM)
    xn = _rms_scale(x_ref[...])
    kvt_ref[...] = _dot_nt(w_kvt_ref[...], (xn * g_kv_ref[...]).astype(BF16))
    q_ref[...] = (_dot((xn * g_q_ref[...]).astype(BF16), w_q_ref[...]) * scale).astype(BF16)


def _swa_sample_proj(x, g_kv, w_kvt, g_q, w_q):
    m = x.shape[0]
    vm = pl.BlockSpec(memory_space=pltpu.VMEM)
    return pl.pallas_call(
        _swa_sample_proj_kernel,
        in_specs=[vm] * 5,
        out_specs=[vm, vm],
        out_shape=[
            jax.ShapeDtypeStruct((w_kvt.shape[0], m), F32),
            jax.ShapeDtypeStruct((m, w_q.shape[1]), BF16),
        ],
        compiler_params=pltpu.CompilerParams(vmem_limit_bytes=V7X_VMEM_LIMIT_BYTES),
        name="swa_sample_proj",
    )(x, g_kv, w_kvt, g_q, w_q)


def _swa_sample_attend_kernel(q_ref, kvt_new_ref, kt_buf_ref, vt_buf_ref, bias_ref, sink_ref,
                              kt_out_ref, vt_out_ref, a_ref):
    i = pl.program_id(0)
    tb, kvd, n_buf = kt_buf_ref.shape
    hd = B_HEAD_DIM
    n_heads = q_ref.shape[0] // tb
    group = n_heads // B_KV_HEADS
    m_tok = kvt_new_ref.shape[1]
    tok_lane = lax.broadcasted_iota(jnp.int32, (2 * kvd, m_tok), 1)
    key_lane = lax.broadcasted_iota(jnp.int32, (kvd, n_buf), 1)
    for tk in range(tb):
        new_col = jnp.sum(jnp.where(tok_lane == i * tb + tk, kvt_new_ref[...], 0.0),
                          axis=1, keepdims=True)
        for buf_ref, out_ref, rows in ((kt_buf_ref, kt_out_ref, slice(0, kvd)),
                                       (vt_buf_ref, vt_out_ref, slice(kvd, 2 * kvd))):
            slid = pltpu.roll(buf_ref[tk], n_buf - 1, 1)
            out_ref[tk] = jnp.where(key_lane == n_buf - 1, new_col[rows], slid)

    e_row = lax.broadcasted_iota(jnp.int32, (hd, kvd), 0)
    e_col = lax.broadcasted_iota(jnp.int32, (hd, kvd), 1)
    spread = jnp.where(e_col % hd == e_row, 1.0, 0.0).astype(BF16)
    head = lax.broadcasted_iota(jnp.int32, (tb, n_heads, kvd), 1)
    slot = lax.broadcasted_iota(jnp.int32, (tb, n_heads, kvd), 2)
    own_slot = head // group == slot // hd
    qx = _dot(q_ref[...], spread).reshape(tb, n_heads, kvd)
    qx = jnp.where(own_slot, qx, 0.0).astype(BF16)

    kt = kt_out_ref[...].astype(BF16)
    vt = vt_out_ref[...].astype(BF16)
    s = jnp.einsum("bhc,bcr->bhr", qx, kt, preferred_element_type=F32)
    s = s + bias_ref[...][None]
    sink = sink_ref[...][None]
    m = jnp.maximum(jnp.max(s, axis=-1, keepdims=True), sink)
    p = jnp.exp(s - m)
    denom = jnp.sum(p, axis=-1, keepdims=True) + jnp.exp(sink - m)
    r = jnp.einsum("bhr,bcr->bhc", p.astype(BF16), vt, preferred_element_type=F32) / denom
    r = jnp.where(own_slot, r, 0.0).astype(BF16).reshape(tb * n_heads, kvd)
    a_ref[...] = _dot_nt(r, spread).astype(BF16)


def _swa_sample_attend(q_rows, kvt_new, kt_buf, vt_buf, bias_row, sink_col, tb):
    m, kvd, n_buf = kt_buf.shape
    n_heads = q_rows.shape[0] // m
    blk3 = lambda i: (i, 0, 0)
    return pl.pallas_call(
        _swa_sample_attend_kernel,
        grid=(m // tb,),
        in_specs=[
            pl.BlockSpec((tb * n_heads, B_HEAD_DIM), lambda i: (i, 0)),
            _resident(kvt_new.shape),
            pl.BlockSpec((tb, kvd, n_buf), blk3),
            pl.BlockSpec((tb, kvd, n_buf), blk3),
            _resident(bias_row.shape),
            _resident(sink_col.shape),
        ],
        out_specs=[
            pl.BlockSpec((tb, kvd, n_buf), blk3),
            pl.BlockSpec((tb, kvd, n_buf), blk3),
            pl.BlockSpec((tb * n_heads, B_HEAD_DIM), lambda i: (i, 0)),
        ],
        out_shape=[
            jax.ShapeDtypeStruct((m, kvd, n_buf), F32),
            jax.ShapeDtypeStruct((m, kvd, n_buf), F32),
            jax.ShapeDtypeStruct((m * n_heads, B_HEAD_DIM), BF16),
        ],
        compiler_params=_cparams("arbitrary"),
        name="swa_sample_attend",
    )(q_rows, kvt_new, kt_buf, vt_buf, bias_row, sink_col)


def _swa_sample_out_kernel(x_ref, a_ref, w_out_ref, y_ref):
    y_ref[...] = x_ref[...] + _dot(a_ref[...], w_out_ref[...])


def _swa_sample_out(x, a, w_out):
    vm = pl.BlockSpec(memory_space=pltpu.VMEM)
    return pl.pallas_call(
        _swa_sample_out_kernel,
        in_specs=[vm] * 3,
        out_specs=vm,
        out_shape=jax.ShapeDtypeStruct(x.shape, F32),
        compiler_params=pltpu.CompilerParams(vmem_limit_bytes=V7X_VMEM_LIMIT_BYTES),
        name="swa_sample_out",
    )(x, a, w_out)


def _cache_keys_minor(cache):
    b, n, kvh, hd = cache.shape
    return jnp.transpose(cache, (0, 2, 3, 1)).reshape(b, kvh * hd, n)


def _cache_keys_major(cache_t):
    b, kvd, n = cache_t.shape
    return jnp.transpose(cache_t.reshape(b, B_KV_HEADS, kvd // B_KV_HEADS, n), (0, 3, 1, 2))


def kernel(x_prompt, x_sample, state_hgrn, cache_k_win, cache_v_win, w_a_in, a_lb, a_gnorm, w_a_out,
           g_mix, g_mlp, g_kv, w_kv, w_b_q, b_sink, w_b_out, rel_bias, w_up, w_down, g_final):
    bsz, seq, d = x_prompt.shape
    n_dec = x_sample.shape[0]
    assert x_sample.shape[1] == 1, "the sample group decodes one token per sequence"
    n_a = w_a_in.shape[0]
    n_b = w_b_q.shape[0]
    assert n_a == 1 and n_b == 1, "depth-2 trunk: one HGRN2 layer, then one attention layer"
    n_buf = cache_k_win.shape[1]
    assert n_buf == WINDOW and seq % WINDOW == 0
    kvd = B_KV_HEADS * B_HEAD_DIM
    n_heads = d // B_HEAD_DIM

    bf = lambda w: w.astype(BF16)
    row = lambda g: g.reshape(1, -1)
    w_in, w_ao = bf(w_a_in[0]), bf(w_a_out[0])
    w_kvb, w_q, w_bo = bf(w_kv), bf(w_b_q[0]), bf(w_b_out[0])
    w_up_b, w_down_b = bf(w_up), bf(w_down)
    band_bias = _band_bias(rel_bias)
    sink = b_sink[0].reshape(1, n_heads)

    x, s_prompt = _hgrn_prompt(x_prompt, w_in, a_lb, row(g_mix[0]), row(a_gnorm[0]), w_ao,
                               layer=0, tile=512)
    x = _mlp(x.reshape(bsz * seq, d), row(g_mlp[0]), w_up_b, w_down_b, 0, None, tile=512)
    x, kt_prompt, vt_prompt = _swa_prompt(x.reshape(bsz, seq, d), row(g_kv), w_kvb,
                                          row(g_mix[1]), w_q.T, sink, band_bias, w_bo, tile=512)
    y_prompt = _mlp(x.reshape(bsz * seq, d), row(g_mlp[1]), w_up_b, w_down_b, 1,
                    row(g_final), tile=512).reshape(bsz, seq, d)

    xs = x_sample.reshape(n_dec, d)
    ft, q, v, g = _hgrn_sample_gates(xs, w_in, a_lb, row(g_mix[0]), layer=0)
    s_sample, o = _hgrn_sample_state(ft, q, v, state_hgrn[0], tb=8)
    xs = _hgrn_sample_out(xs, o, g, row(a_gnorm[0]), w_ao)
    xs = _mlp(xs, row(g_mlp[0]), w_up_b, w_down_b, 0, None, tile=n_dec)
    kvt_new, q = _swa_sample_proj(xs, row(g_kv), w_kvb.T, row(g_mix[1]), w_q)
    bias_row = band_bias[:, 1:n_buf + 1, 0]
    kt_sample, vt_sample, a = _swa_sample_attend(
        q.reshape(n_dec * n_heads, B_HEAD_DIM), kvt_new,
        _cache_keys_minor(cache_k_win), _cache_keys_minor(cache_v_win),
        bias_row, sink.reshape(n_heads, 1), tb=16)
    xs = _swa_sample_out(xs, a.reshape(n_dec, d), w_bo)
    y_sample = _mlp(xs, row(g_mlp[1]), w_up_b, w_down_b, 1, row(g_final),
                    tile=n_dec).reshape(n_dec, 1, d)

    return (y_prompt, y_sample, s_prompt[None], s_sample[None],
            _cache_keys_major(kt_prompt), _cache_keys_major(vt_prompt),
            _cache_keys_major(kt_sample), _cache_keys_major(vt_sample))
```

```python
import functools
import math

import jax
import jax.numpy as jnp
from jax import lax
from jax.experimental import pallas as pl
from jax.experimental.pallas import tpu as pltpu

F32 = jnp.float32
BF16 = jnp.bfloat16

EPS = 1e-6
NEG = -1e30
LOG2E = math.log2(math.e)
A_HEADS = 8
A_CHUNK = 64
B_HEAD_DIM = 64
B_KV_HEADS = 4
WINDOW = 128
N_BUCKETS = 32
MAX_DISTANCE = 128

V7X_VMEM_LIMIT_BYTES = 56 * 1024 * 1024
V7X_MXU_DIM = 256


def _cparams(*semantics):
    return pltpu.CompilerParams(dimension_semantics=semantics,
                                vmem_limit_bytes=V7X_VMEM_LIMIT_BYTES)


def _resident(shape):
    nd = len(shape)
    return pl.BlockSpec(shape, lambda *_: (0,) * nd, pipeline_mode=pl.Buffered(1))


def _resident_layer(shape, layer):
    nd = len(shape)
    return pl.BlockSpec((None,) + tuple(shape[1:]), lambda *_: (layer,) + (0,) * (nd - 1),
                        pipeline_mode=pl.Buffered(1))


def _rms_scale(x):
    return x * lax.rsqrt(jnp.mean(x * x, axis=-1, keepdims=True) + EPS)


def _sigmoid(x):
    return 1.0 / (1.0 + jnp.exp(-x))


def _dot(a, b):
    return jnp.dot(a, b, preferred_element_type=F32)


def _dot_nt(a, b):
    return lax.dot_general(a, b, (((1,), (1,)), ((), ())), preferred_element_type=F32)


def _dot_tn(a, b):
    return lax.dot_general(a, b, (((0,), (0,)), ((), ())), preferred_element_type=F32)


def _lower_bound(a_lb, layer):
    m = jnp.max(a_lb, axis=0, keepdims=True)
    e = jnp.exp(a_lb - m)
    return jnp.sum(e[: layer + 1], axis=0, keepdims=True) / jnp.sum(e, axis=0, keepdims=True)


def _forget_gate(f_raw, lb):
    return lb + (1.0 - lb) * _sigmoid(f_raw)


def _hgrn_prompt_kernel(layer, x_ref, w_in_ref, a_lb_ref, g_mix_ref, g_norm_ref, w_out_ref,
                        y_ref, s_out_ref,
                        st_ref, proj_ref, lf_ref, kk_ref, b_ref, qi_ref, ki_ref, qn_ref, ks_ref, vb_ref,
                        dec_ref, att_ref, a_ref):
    t = pl.program_id(1)
    tile, d = x_ref.shape[1], x_ref.shape[2]
    dk = d // A_HEADS
    c = A_CHUNK
    n_chunks = tile // c

    @pl.when(t == 0)
    def _():
        st_ref[...] = jnp.zeros_like(st_ref)
        dec_ref[...] = jnp.zeros_like(dec_ref)

    x = x_ref[0]
    h = (_rms_scale(x) * g_mix_ref[...]).astype(BF16)
    lb = _lower_bound(a_lb_ref[...], layer)
    g_norm = g_norm_ref[...]

    span = min(tile, V7X_MXU_DIM)
    row = lax.broadcasted_iota(jnp.int32, (span, span), 0)
    col = lax.broadcasted_iota(jnp.int32, (span, span), 1)
    tri = jnp.where((row >= col) & (row // c == col // c), 1.0, 0.0).astype(BF16)
    crow = lax.broadcasted_iota(jnp.int32, (c, c), 0)
    ccol = lax.broadcasted_iota(jnp.int32, (c, c), 1)
    causal = crow >= ccol

    proj_ref[...] = _dot(h, w_in_ref[...])

    for ci in range(n_chunks):
        rows = slice(ci * c, (ci + 1) * c)
        forget = _forget_gate(proj_ref[rows, d:2 * d], lb)
        kk_ref[rows, :] = 1.0 - forget
        logf = jnp.log(forget)
        hi = logf.astype(BF16)
        r1 = logf - hi.astype(F32)
        mid = r1.astype(BF16)
        lf_ref[0, rows, :] = hi
        lf_ref[1, rows, :] = mid
        lf_ref[2, rows, :] = (r1 - mid.astype(F32)).astype(BF16)

    for r0 in range(0, tile, span):
        rows = slice(r0, r0 + span)
        b_ref[rows, :] = (_dot(tri, lf_ref[0, rows, :]) + _dot(tri, lf_ref[1, rows, :])
                          + _dot(tri, lf_ref[2, rows, :]))

    for ci in range(n_chunks):
        rows = slice(ci * c, (ci + 1) * c)
        b = b_ref[rows, :]
        b_mid = b_ref[ci * c + c // 2 - 1:ci * c + c // 2, :]
        b_last = b_ref[ci * c + c - 1:ci * c + c, :]
        q = proj_ref[rows, 0:d]
        k = kk_ref[rows, :]
        qi_ref[rows, :] = (q * jnp.exp(b - b_mid)).astype(BF16)
        ki_ref[rows, :] = (k * jnp.exp(b_mid - b)).astype(BF16)
        qn_ref[rows, :] = (q * jnp.exp(b)).astype(BF16)
        ks_ref[rows, :] = (k * jnp.exp(b_last - b)).astype(BF16)
        vb_ref[rows, :] = proj_ref[rows, 2 * d:3 * d].astype(BF16)
        decay = jnp.exp(b_last)
        for hd in range(A_HEADS):
            dec_ref[ci * A_HEADS + hd:ci * A_HEADS + hd + 1, :] = decay[:, hd * dk:(hd + 1) * dk]

    for ci in range(n_chunks):
        rows = slice(ci * c, (ci + 1) * c)
        for hd in range(A_HEADS):
            ls = slice(hd * dk, (hd + 1) * dk)
            att = jnp.where(causal, _dot_nt(qi_ref[rows, ls], ki_ref[rows, ls]), 0.0)
            att_ref[ci * A_HEADS + hd] = att.astype(BF16)

    dec_t = dec_ref[...].T
    for ci in range(n_chunks):
        rows = slice(ci * c, (ci + 1) * c)
        for hd in range(A_HEADS):
            ls = slice(hd * dk, (hd + 1) * dk)
            j = ci * A_HEADS + hd
            st = st_ref[hd]
            lhs = jnp.concatenate([qn_ref[rows, ls], att_ref[j]], axis=1)
            rhs = jnp.concatenate([st.astype(BF16), vb_ref[rows, ls]], axis=0)
            o = _dot(lhs, rhs)
            st_ref[hd] = st * dec_t[:, j:j + 1] + _dot_tn(ks_ref[rows, ls], vb_ref[rows, ls])
            g = proj_ref[rows, 3 * d + hd * dk:3 * d + (hd + 1) * dk]
            a_ref[rows, ls] = (_rms_scale(o) * g_norm * (g * _sigmoid(g))).astype(BF16)

    y_ref[0] = x + _dot(a_ref[...], w_out_ref[...])

    @pl.when(t == pl.num_programs(1) - 1)
    def _():
        s_out_ref[0] = st_ref[...]


def _hgrn_prompt(x, w_in, a_lb, g_mix, g_norm, w_out, layer, tile):
    bsz, seq, d = x.shape
    dk = d // A_HEADS
    tile = min(tile, seq)
    return pl.pallas_call(
        functools.partial(_hgrn_prompt_kernel, layer),
        grid=(bsz, seq // tile),
        in_specs=[
            pl.BlockSpec((1, tile, d), lambda b, t: (b, t, 0)),
            _resident(w_in.shape),
            _resident(a_lb.shape),
            _resident(g_mix.shape),
            _resident(g_norm.shape),
            _resident(w_out.shape),
        ],
        out_specs=[
            pl.BlockSpec((1, tile, d), lambda b, t: (b, t, 0)),
            pl.BlockSpec((1, A_HEADS, dk, dk), lambda b, t: (b, 0, 0, 0)),
        ],
        out_shape=[
            jax.ShapeDtypeStruct((bsz, seq, d), F32),
            jax.ShapeDtypeStruct((bsz, A_HEADS, dk, dk), F32),
        ],
        scratch_shapes=[
            pltpu.VMEM((A_HEADS, dk, dk), F32),
            pltpu.VMEM((tile, 4 * d), F32),
            pltpu.VMEM((3, tile, d), BF16),
            pltpu.VMEM((tile, d), F32),
            pltpu.VMEM((tile, d), F32),
            pltpu.VMEM((tile, d), BF16),
            pltpu.VMEM((tile, d), BF16),
            pltpu.VMEM((tile, d), BF16),
            pltpu.VMEM((tile, d), BF16),
            pltpu.VMEM((tile, d), BF16),
            pltpu.VMEM((dk, dk), F32),
            pltpu.VMEM((tile // A_CHUNK * A_HEADS, A_CHUNK, A_CHUNK), BF16),
            pltpu.VMEM((tile, d), BF16),
        ],
        compiler_params=_cparams("arbitrary", "arbitrary"),
        name="hgrn_prompt",
    )(x, w_in, a_lb, g_mix, g_norm, w_out)


def _mlp_kernel(ff_chunk, has_final, x_ref, g_ref, w_up_ref, w_down_ref, *rest):
    if has_final:
        g_final_ref, y_ref = rest
    else:
        (y_ref,) = rest
    x = x_ref[...]
    h = (_rms_scale(x) * g_ref[...]).astype(BF16)
    acc = x
    for j in range(w_up_ref.shape[1] // ff_chunk):
        cols = slice(j * ff_chunk, (j + 1) * ff_chunk)
        u = jnp.maximum(_dot(h, w_up_ref[:, cols]), 0.0)
        acc = acc + _dot((u * u).astype(BF16), w_down_ref[cols, :])
    if has_final:
        acc = _rms_scale(acc) * g_final_ref[...]
    y_ref[...] = acc


def _mlp(x, g, w_up, w_down, layer, g_final, tile, ff_chunk=1024):
    m, d = x.shape
    tile = min(tile, m)
    has_final = g_final is not None
    in_specs = [
        pl.BlockSpec((tile, d), lambda i: (i, 0)),
        _resident(g.shape),
        _resident_layer(w_up.shape, layer),
        _resident_layer(w_down.shape, layer),
    ]
    args = [x, g, w_up, w_down]
    if has_final:
        in_specs.append(_resident(g_final.shape))
        args.append(g_final)
    return pl.pallas_call(
        functools.partial(_mlp_kernel, ff_chunk, has_final),
        grid=(m // tile,),
        in_specs=in_specs,
        out_specs=pl.BlockSpec((tile, d), lambda i: (i, 0)),
        out_shape=jax.ShapeDtypeStruct((m, d), F32),
        compiler_params=_cparams("arbitrary"),
        name="mlp_final" if has_final else "mlp",
    )(*args)


def _t5_bucket(dist):
    n = jnp.maximum(dist, 0)
    max_exact = N_BUCKETS // 2
    nf = jnp.maximum(n, 1).astype(F32)
    large = max_exact + (jnp.log(nf / max_exact) / math.log(MAX_DISTANCE / max_exact)
                         * (N_BUCKETS - max_exact)).astype(jnp.int32)
    large = jnp.minimum(large, N_BUCKETS - 1)
    return jnp.where(n < max_exact, n, large)


def _band_bias_kernel(rel_bias_ref, out_ref):
    n_heads, _, w = out_ref.shape
    kj = lax.broadcasted_iota(jnp.int32, (2 * w, w), 0)
    qi = lax.broadcasted_iota(jnp.int32, (2 * w, w), 1)
    dist = qi + w - kj
    bucket = _t5_bucket(dist)
    valid = (dist >= 0) & (dist < WINDOW)
    for hd in range(n_heads):
        bias = jnp.zeros((2 * w, w), F32)
        for bk in range(N_BUCKETS):
            bias = jnp.where(bucket == bk, rel_bias_ref[bk, hd], bias)
        out_ref[hd] = jnp.where(valid, bias * LOG2E, NEG)


def _band_bias(rel_bias):
    n_heads = rel_bias.shape[1]
    return pl.pallas_call(
        _band_bias_kernel,
        in_specs=[pl.BlockSpec(memory_space=pltpu.SMEM)],
        out_specs=pl.BlockSpec(memory_space=pltpu.VMEM),
        out_shape=jax.ShapeDtypeStruct((n_heads, 2 * WINDOW, WINDOW), F32),
        name="band_bias",
    )(rel_bias)


def _col_max(x):
    return jnp.max(x, axis=0, keepdims=True)


def _col_sum(x):
    return jnp.sum(x, axis=0, keepdims=True)


def _swa_prompt_kernel(x_ref, g_kv_ref, w_kv_ref, g_q_ref, w_qt_ref, sink_ref, bias_ref, w_out_ref,
                       y_ref, k_out_ref, v_out_ref, k_ref, vt_ref, qt_ref, at_ref, s_ref, p_ref):
    t = pl.program_id(1)
    tile, d = x_ref.shape[1], x_ref.shape[2]
    w = WINDOW
    hd = B_HEAD_DIM
    n_heads = d // hd
    group = n_heads // B_KV_HEADS
    kvd = B_KV_HEADS * hd
    scale = 1.0 / math.sqrt(hd)

    @pl.when(t == 0)
    def _():
        k_ref[:, 0:w, :] = jnp.zeros((B_KV_HEADS, w, hd), BF16)
        vt_ref[:, 0:w] = jnp.zeros((kvd, w), BF16)

    @pl.when(t > 0)
    def _():
        k_ref[:, 0:w, :] = k_ref[:, tile:tile + w, :]
        vt_ref[:, 0:w] = vt_ref[:, tile:tile + w]

    x = x_ref[0]
    xn = _rms_scale(x)
    kv = _dot((xn * g_kv_ref[...]).astype(BF16), w_kv_ref[...])
    for kh in range(B_KV_HEADS):
        k_ref[kh, w:w + tile, :] = kv[:, kh * hd:(kh + 1) * hd].astype(BF16)
    vt_ref[:, w:w + tile] = kv[:, kvd:2 * kvd].T.astype(BF16)
    h_q = (xn * g_q_ref[...]).astype(BF16)
    qt_ref[...] = (_dot_nt(w_qt_ref[...], h_q) * (scale * LOG2E)).astype(BF16)

    @pl.when(t == pl.num_programs(1) - 1)
    def _():
        k_out_ref[0] = kv[tile - w:, 0:kvd].T
        v_out_ref[0] = kv[tile - w:, kvd:2 * kvd].T

    first = t == 0
    sc = 64
    rc = 16
    units = [(blk, kh) for blk in range(tile // w) for kh in range(B_KV_HEADS)]

    def scores(u):
        blk, kh = units[u]
        cols = slice(blk * w, (blk + 1) * w)
        heads = range(kh * group, (kh + 1) * group)
        q4 = jnp.concatenate([qt_ref[h * hd:(h + 1) * hd, cols] for h in heads], axis=1)
        m = None
        for r in range(2 * w // sc):
            s = _dot(k_ref[kh, blk * w + r * sc:blk * w + (r + 1) * sc, :], q4)
            s = s + jnp.concatenate([bias_ref[h, r * sc:(r + 1) * sc, :] for h in heads], axis=1)
            if blk == 0 and r * sc < w:
                s = jnp.where(first, NEG, s)
            s_ref[u % 2, r * sc:(r + 1) * sc, :] = s
            for i in range(sc // 8):
                part = s[i * 8:(i + 1) * 8, :]
                m = part if m is None else jnp.maximum(m, part)
        return m

    def softmax(u, m8):
        blk, kh = units[u]
        heads = range(kh * group, (kh + 1) * group)
        sink = jnp.concatenate([jnp.full((1, w), sink_ref[0, h] * LOG2E, F32) for h in heads],
                               axis=1)
        m = jnp.maximum(_col_max(m8), sink)
        for r in range(2 * w // rc):
            p = jnp.exp2(s_ref[u % 2, r * rc:(r + 1) * rc, :] - m)
            p_ref[u % 2, r * rc:(r + 1) * rc, :] = p.astype(BF16)
        return jnp.exp2(sink - m)

    ones_rows = jnp.ones((rc, 2 * w), BF16)

    def weighted_values(u, sink_term):
        blk, kh = units[u]
        cols = slice(blk * w, (blk + 1) * w)
        keys = slice(blk * w, (blk + 2) * w)
        vt1 = jnp.concatenate([vt_ref[kh * hd:(kh + 1) * hd, keys], ones_rows], axis=0)
        o = _dot(vt1, p_ref[u % 2])
        o = o[0:hd] / (o[hd:hd + 1] + sink_term)
        for g in range(group):
            h = kh * group + g
            at_ref[h * hd:(h + 1) * hd, cols] = o[:, g * w:(g + 1) * w].astype(BF16)

    m_next = scores(0)
    denom_prev = None
    for u in range(len(units)):
        m_cur = m_next
        if u + 1 < len(units):
            m_next = scores(u + 1)
        denom = softmax(u, m_cur)
        if u > 0:
            weighted_values(u - 1, denom_prev)
        denom_prev = denom
    weighted_values(len(units) - 1, denom_prev)

    y_ref[0] = x + _dot_tn(at_ref[...], w_out_ref[...])


def _swa_prompt(x, g_kv, w_kv, g_q, w_qt, sink, band_bias, w_out, tile):
    bsz, seq, d = x.shape
    tile = min(tile, seq)
    kvd = w_kv.shape[1] // 2
    w = WINDOW
    return pl.pallas_call(
        _swa_prompt_kernel,
        grid=(bsz, seq // tile),
        in_specs=[
            pl.BlockSpec((1, tile, d), lambda b, t: (b, t, 0)),
            _resident(g_kv.shape),
            _resident(w_kv.shape),
            _resident(g_q.shape),
            _resident(w_qt.shape),
            pl.BlockSpec(memory_space=pltpu.SMEM),
            _resident(band_bias.shape),
            _resident(w_out.shape),
        ],
        out_specs=[
            pl.BlockSpec((1, tile, d), lambda b, t: (b, t, 0)),
            pl.BlockSpec((1, kvd, w), lambda b, t: (b, 0, 0)),
            pl.BlockSpec((1, kvd, w), lambda b, t: (b, 0, 0)),
        ],
        out_shape=[
            jax.ShapeDtypeStruct((bsz, seq, d), F32),
            jax.ShapeDtypeStruct((bsz, kvd, w), F32),
            jax.ShapeDtypeStruct((bsz, kvd, w), F32),
        ],
        scratch_shapes=[
            pltpu.VMEM((B_KV_HEADS, tile + w, B_HEAD_DIM), BF16),
            pltpu.VMEM((kvd, tile + w), BF16),
            pltpu.VMEM((d, tile), BF16),
            pltpu.VMEM((d, tile), BF16),
            pltpu.VMEM((2, 2 * w, d // B_KV_HEADS * w // B_HEAD_DIM), F32),
            pltpu.VMEM((2, 2 * w, d // B_KV_HEADS * w // B_HEAD_DIM), BF16),
        ],
        compiler_params=_cparams("arbitrary", "arbitrary"),
        name="swa_prompt",
    )(x, g_kv, w_kv, g_q, w_qt, sink, band_bias, w_out)


def _hgrn_sample_gates_kernel(layer, x_ref, w_in_ref, a_lb_ref, g_mix_ref,
                              ft_ref, q_ref, v_ref, g_ref):
    d = x_ref.shape[1]
    dk = d // A_HEADS
    h = (_rms_scale(x_ref[...]) * g_mix_ref[...]).astype(BF16)
    proj = _dot(h, w_in_ref[...])
    forget = _forget_gate(proj[:, d:2 * d], _lower_bound(a_lb_ref[...], layer))
    q_ref[...] = proj[:, 0:d]
    v_ref[...] = proj[:, 2 * d:3 * d]
    g_ref[...] = proj[:, 3 * d:4 * d]
    for hd in range(A_HEADS):
        ft_ref[hd] = forget[:, hd * dk:(hd + 1) * dk].T


def _hgrn_sample_gates(x, w_in, a_lb, g_mix, layer):
    m, d = x.shape
    dk = d // A_HEADS
    vm = pl.BlockSpec(memory_space=pltpu.VMEM)
    return pl.pallas_call(
        functools.partial(_hgrn_sample_gates_kernel, layer),
        in_specs=[vm, vm, vm, vm],
        out_specs=[vm, vm, vm, vm],
        out_shape=[
            jax.ShapeDtypeStruct((A_HEADS, dk, m), F32),
            jax.ShapeDtypeStruct((m, d), F32),
            jax.ShapeDtypeStruct((m, d), F32),
            jax.ShapeDtypeStruct((m, d), F32),
        ],
        compiler_params=pltpu.CompilerParams(vmem_limit_bytes=V7X_VMEM_LIMIT_BYTES),
        name="hgrn_sample_gates",
    )(x, w_in, a_lb, g_mix)


def _hgrn_sample_state_kernel(ft_ref, q_ref, v_ref, s0_ref, s_ref, o_ref):
    i = pl.program_id(0)
    tb = s0_ref.shape[0]
    dk = s0_ref.shape[2]
    m = ft_ref.shape[2]
    lane = lax.broadcasted_iota(jnp.int32, (dk, m), 1)
    for tk in range(tb):
        pick = lane == i * tb + tk
        for hd in range(A_HEADS):
            ls = slice(hd * dk, (hd + 1) * dk)
            f = jnp.sum(jnp.where(pick, ft_ref[hd], 0.0), axis=1, keepdims=True)
            v = v_ref[tk:tk + 1, ls]
            s_new = v + f * (s0_ref[tk, hd] - v)
            s_ref[tk, hd] = s_new
            o = _dot(q_ref[:, ls].astype(BF16), s_new.astype(BF16))
            o_ref[tk:tk + 1, ls] = o[tk:tk + 1]


def _hgrn_sample_state(ft, q, v, s0, tb):
    m, d = v.shape
    dk = d // A_HEADS
    return pl.pallas_call(
        _hgrn_sample_state_kernel,
        grid=(m // tb,),
        in_specs=[
            _resident(ft.shape),
            pl.BlockSpec((tb, d), lambda i: (i, 0)),
            pl.BlockSpec((tb, d), lambda i: (i, 0)),
            pl.BlockSpec((tb, A_HEADS, dk, dk), lambda i: (i, 0, 0, 0)),
        ],
        out_specs=[
            pl.BlockSpec((tb, A_HEADS, dk, dk), lambda i: (i, 0, 0, 0)),
            pl.BlockSpec((tb, d), lambda i: (i, 0)),
        ],
        out_shape=[
            jax.ShapeDtypeStruct(s0.shape, F32),
            jax.ShapeDtypeStruct((m, d), F32),
        ],
        compiler_params=_cparams("arbitrary"),
        name="hgrn_sample_state",
    )(ft, q, v, s0)


def _hgrn_sample_out_kernel(x_ref, o_ref, g_ref, g_norm_ref, w_out_ref, y_ref):
    d = x_ref.shape[1]
    dk = d // A_HEADS
    g = g_ref[...]
    gate = g * _sigmoid(g)
    parts = []
    for hd in range(A_HEADS):
        ls = slice(hd * dk, (hd + 1) * dk)
        parts.append(_rms_scale(o_ref[:, ls]) * g_norm_ref[...] * gate[:, ls])
    a = jnp.concatenate(parts, axis=1).astype(BF16)
    y_ref[...] = x_ref[...] + _dot(a, w_out_ref[...])


def _hgrn_sample_out(x, o, g, g_norm, w_out):
    vm = pl.BlockSpec(memory_space=pltpu.VMEM)
    return pl.pallas_call(
        _hgrn_sample_out_kernel,
        in_specs=[vm] * 5,
        out_specs=vm,
        out_shape=jax.ShapeDtypeStruct(x.shape, F32),
        compiler_params=pltpu.CompilerParams(vmem_limit_bytes=V7X_VMEM_LIMIT_BYTES),
        name="hgrn_sample_out",
    )(x, o, g, g_norm, w_out)


def _swa_sample_proj_kernel(x_ref, g_kv_ref, w_kvt_ref, g_q_ref, w_q_ref, kvt_ref, q_ref):
    scale = 1.0 / math.sqrt(B_HEAD_DIM)
    xn = _rms_scale(x_ref[...])
    kvt_ref[...] = _dot_nt(w_kvt_ref[...], (xn * g_kv_ref[...]).astype(BF16))
    q_ref[...] = (_dot((xn * g_q_ref[...]).astype(BF16), w_q_ref[...])
                  * (scale * LOG2E)).astype(BF16)


def _swa_sample_proj(x, g_kv, w_kvt, g_q, w_q):
    m = x.shape[0]
    vm = pl.BlockSpec(memory_space=pltpu.VMEM)
    return pl.pallas_call(
        _swa_sample_proj_kernel,
        in_specs=[vm] * 5,
        out_specs=[vm, vm],
        out_shape=[
            jax.ShapeDtypeStruct((w_kvt.shape[0], m), F32),
            jax.ShapeDtypeStruct((m, w_q.shape[1]), BF16),
        ],
        compiler_params=pltpu.CompilerParams(vmem_limit_bytes=V7X_VMEM_LIMIT_BYTES),
        name="swa_sample_proj",
    )(x, g_kv, w_kvt, g_q, w_q)


def _swa_sample_attend_kernel(q_ref, kvt_new_ref, kt_buf_ref, vt_buf_ref, bias_ref, sink_ref,
                              kt_out_ref, vt_out_ref, a_ref):
    i = pl.program_id(0)
    tb, kvd, n_buf = kt_buf_ref.shape
    hd = B_HEAD_DIM
    n_heads = q_ref.shape[0] // tb
    group = n_heads // B_KV_HEADS
    m_tok = kvt_new_ref.shape[1]
    tok_lane = lax.broadcasted_iota(jnp.int32, (2 * kvd, m_tok), 1)
    key_lane = lax.broadcasted_iota(jnp.int32, (kvd, n_buf), 1)
    for tk in range(tb):
        new_col = jnp.sum(jnp.where(tok_lane == i * tb + tk, kvt_new_ref[...], 0.0),
                          axis=1, keepdims=True)
        for buf_ref, out_ref, rows in ((kt_buf_ref, kt_out_ref, slice(0, kvd)),
                                       (vt_buf_ref, vt_out_ref, slice(kvd, 2 * kvd))):
            slid = pltpu.roll(buf_ref[tk], n_buf - 1, 1)
            out_ref[tk] = jnp.where(key_lane == n_buf - 1, new_col[rows], slid)

    e_row = lax.broadcasted_iota(jnp.int32, (hd, kvd), 0)
    e_col = lax.broadcasted_iota(jnp.int32, (hd, kvd), 1)
    spread = jnp.where(e_col % hd == e_row, 1.0, 0.0).astype(BF16)
    head = lax.broadcasted_iota(jnp.int32, (tb, n_heads, kvd), 1)
    slot = lax.broadcasted_iota(jnp.int32, (tb, n_heads, kvd), 2)
    own_slot = head // group == slot // hd
    qx = _dot(q_ref[...], spread).reshape(tb, n_heads, kvd)
    qx = jnp.where(own_slot, qx, 0.0).astype(BF16)

    kt = kt_out_ref[...].astype(BF16)
    vt = vt_out_ref[...].astype(BF16)
    s = jnp.einsum("bhc,bcr->bhr", qx, kt, preferred_element_type=F32)
    s = s + bias_ref[...][None]
    sink = sink_ref[...][None] * LOG2E
    m = jnp.maximum(jnp.max(s, axis=-1, keepdims=True), sink)
    p = jnp.exp2(s - m)
    denom = jnp.sum(p, axis=-1, keepdims=True) + jnp.exp2(sink - m)
    r = jnp.einsum("bhr,bcr->bhc", p.astype(BF16), vt, preferred_element_type=F32) / denom
    r = jnp.where(own_slot, r, 0.0).astype(BF16).reshape(tb * n_heads, kvd)
    a_ref[...] = _dot_nt(r, spread).astype(BF16)


def _swa_sample_attend(q_rows, kvt_new, kt_buf, vt_buf, bias_row, sink_col, tb):
    m, kvd, n_buf = kt_buf.shape
    n_heads = q_rows.shape[0] // m
    blk3 = lambda i: (i, 0, 0)
    return pl.pallas_call(
        _swa_sample_attend_kernel,
        grid=(m // tb,),
        in_specs=[
            pl.BlockSpec((tb * n_heads, B_HEAD_DIM), lambda i: (i, 0)),
            _resident(kvt_new.shape),
            pl.BlockSpec((tb, kvd, n_buf), blk3),
            pl.BlockSpec((tb, kvd, n_buf), blk3),
            _resident(bias_row.shape),
            _resident(sink_col.shape),
        ],
        out_specs=[
            pl.BlockSpec((tb, kvd, n_buf), blk3),
            pl.BlockSpec((tb, kvd, n_buf), blk3),
            pl.BlockSpec((tb * n_heads, B_HEAD_DIM), lambda i: (i, 0)),
        ],
        out_shape=[
            jax.ShapeDtypeStruct((m, kvd, n_buf), F32),
            jax.ShapeDtypeStruct((m, kvd, n_buf), F32),
            jax.ShapeDtypeStruct((m * n_heads, B_HEAD_DIM), BF16),
        ],
        compiler_params=_cparams("arbitrary"),
        name="swa_sample_attend",
    )(q_rows, kvt_new, kt_buf, vt_buf, bias_row, sink_col)


def _swa_sample_out_kernel(x_ref, a_ref, w_out_ref, y_ref):
    y_ref[...] = x_ref[...] + _dot(a_ref[...], w_out_ref[...])


def _swa_sample_out(x, a, w_out):
    vm = pl.BlockSpec(memory_space=pltpu.VMEM)
    return pl.pallas_call(
        _swa_sample_out_kernel,
        in_specs=[vm] * 3,
        out_specs=vm,
        out_shape=jax.ShapeDtypeStruct(x.shape, F32),
        compiler_params=pltpu.CompilerParams(vmem_limit_bytes=V7X_VMEM_LIMIT_BYTES),
        name="swa_sample_out",
    )(x, a, w_out)


def _cache_keys_minor(cache):
    b, n, kvh, hd = cache.shape
    return jnp.transpose(cache, (0, 2, 3, 1)).reshape(b, kvh * hd, n)


def _cache_keys_major(cache_t):
    b, kvd, n = cache_t.shape
    return jnp.transpose(cache_t.reshape(b, B_KV_HEADS, kvd // B_KV_HEADS, n), (0, 3, 1, 2))


def kernel(x_prompt, x_sample, state_hgrn, cache_k_win, cache_v_win, w_a_in, a_lb, a_gnorm, w_a_out,
           g_mix, g_mlp, g_kv, w_kv, w_b_q, b_sink, w_b_out, rel_bias, w_up, w_down, g_final):
    bsz, seq, d = x_prompt.shape
    n_dec = x_sample.shape[0]
    assert x_sample.shape[1] == 1, "the sample group decodes one token per sequence"
    n_a = w_a_in.shape[0]
    n_b = w_b_q.shape[0]
    assert n_a == 1 and n_b == 1, "depth-2 trunk: one HGRN2 layer, then one attention layer"
    n_buf = cache_k_win.shape[1]
    assert n_buf == WINDOW and seq % WINDOW == 0
    kvd = B_KV_HEADS * B_HEAD_DIM
    n_heads = d // B_HEAD_DIM

    bf = lambda w: w.astype(BF16)
    row = lambda g: g.reshape(1, -1)
    w_in, w_ao = bf(w_a_in[0]), bf(w_a_out[0])
    w_kvb, w_q, w_bo = bf(w_kv), bf(w_b_q[0]), bf(w_b_out[0])
    w_up_b, w_down_b = bf(w_up), bf(w_down)
    band_bias = _band_bias(rel_bias)
    sink = b_sink[0].reshape(1, n_heads)

    x, s_prompt = _hgrn_prompt(x_prompt, w_in, a_lb, row(g_mix[0]), row(a_gnorm[0]), w_ao,
                               layer=0, tile=512)
    x = _mlp(x.reshape(bsz * seq, d), row(g_mlp[0]), w_up_b, w_down_b, 0, None, tile=512)
    x, kt_prompt, vt_prompt = _swa_prompt(x.reshape(bsz, seq, d), row(g_kv), w_kvb,
                                          row(g_mix[1]), w_q.T, sink, band_bias, w_bo, tile=512)
    y_prompt = _mlp(x.reshape(bsz * seq, d), row(g_mlp[1]), w_up_b, w_down_b, 1,
                    row(g_final), tile=512).reshape(bsz, seq, d)

    xs = x_sample.reshape(n_dec, d)
    ft, q, v, g = _hgrn_sample_gates(xs, w_in, a_lb, row(g_mix[0]), layer=0)
    s_sample, o = _hgrn_sample_state(ft, q, v, state_hgrn[0], tb=8)
    xs = _hgrn_sample_out(xs, o, g, row(a_gnorm[0]), w_ao)
    xs = _mlp(xs, row(g_mlp[0]), w_up_b, w_down_b, 0, None, tile=n_dec)
    kvt_new, q = _swa_sample_proj(xs, row(g_kv), w_kvb.T, row(g_mix[1]), w_q)
    bias_row = band_bias[:, 1:n_buf + 1, 0]
    kt_sample, vt_sample, a = _swa_sample_attend(
        q.reshape(n_dec * n_heads, B_HEAD_DIM), kvt_new,
        _cache_keys_minor(cache_k_win), _cache_keys_minor(cache_v_win),
        bias_row, sink.reshape(n_heads, 1), tb=16)
    xs = _swa_sample_out(xs, a.reshape(n_dec, d), w_bo)
    y_sample = _mlp(xs, row(g_mlp[1]), w_up_b, w_down_b, 1, row(g_final),
                    tile=n_dec).reshape(n_dec, 1, d)

    return (y_prompt, y_sample, s_prompt[None], s_sample[None],
            _cache_keys_major(kt_prompt), _cache_keys_major(vt_prompt),
            _cache_keys_major(kt_sample), _cache_keys_major(vt_sample))
```

```python
import functools
import math
from typing import Callable, NamedTuple

import jax
import jax.numpy as jnp
from jax import lax
from jax.experimental import pallas as pl
from jax.experimental.pallas import tpu as pltpu

F32 = jnp.float32
BF16 = jnp.bfloat16

EPS = 1e-6
NEG = -1e30
LOG2E = math.log2(math.e)
A_HEADS = 8
A_CHUNK = 64
B_HEAD_DIM = 64
B_KV_HEADS = 4
WINDOW = 128
N_BUCKETS = 32
MAX_DISTANCE = 128

V7X_VMEM_LIMIT_BYTES = 56 * 1024 * 1024
V7X_MXU_DIM = 256


def _cparams(*semantics):
    return pltpu.CompilerParams(dimension_semantics=semantics,
                                vmem_limit_bytes=V7X_VMEM_LIMIT_BYTES)


def _resident(shape):
    nd = len(shape)
    return pl.BlockSpec(shape, lambda *_: (0,) * nd, pipeline_mode=pl.Buffered(1))


def _resident_layer(shape, layer):
    nd = len(shape)
    return pl.BlockSpec((None,) + tuple(shape[1:]), lambda *_: (layer,) + (0,) * (nd - 1),
                        pipeline_mode=pl.Buffered(1))


def _rms_scale(x):
    return x * lax.rsqrt(jnp.mean(x * x, axis=-1, keepdims=True) + EPS)


def _sigmoid(x):
    return 1.0 / (1.0 + jnp.exp(-x))


def _dot(a, b):
    return jnp.dot(a, b, preferred_element_type=F32)


def _dot_nt(a, b):
    return lax.dot_general(a, b, (((1,), (1,)), ((), ())), preferred_element_type=F32)


def _dot_tn(a, b):
    return lax.dot_general(a, b, (((0,), (0,)), ((), ())), preferred_element_type=F32)


def _lower_bound(a_lb, layer):
    m = jnp.max(a_lb, axis=0, keepdims=True)
    e = jnp.exp(a_lb - m)
    return jnp.sum(e[: layer + 1], axis=0, keepdims=True) / jnp.sum(e, axis=0, keepdims=True)


def _forget_gate(f_raw, lb):
    return lb + (1.0 - lb) * _sigmoid(f_raw)


def _hgrn_prompt_kernel(layer, x_ref, w_in_ref, a_lb_ref, g_mix_ref, g_norm_ref, w_out_ref,
                        y_ref, s_out_ref,
                        st_ref, proj_ref, lf_ref, kk_ref, b_ref, qi_ref, ki_ref, qn_ref, ks_ref, vb_ref,
                        dec_ref, att_ref, a_ref):
    t = pl.program_id(1)
    tile, d = x_ref.shape[1], x_ref.shape[2]
    dk = d // A_HEADS
    c = A_CHUNK
    n_chunks = tile // c

    @pl.when(t == 0)
    def _():
        st_ref[...] = jnp.zeros_like(st_ref)
        dec_ref[...] = jnp.zeros_like(dec_ref)

    x = x_ref[0]
    h = (_rms_scale(x) * g_mix_ref[...]).astype(BF16)
    lb = _lower_bound(a_lb_ref[...], layer)
    g_norm = g_norm_ref[...]

    span = min(tile, V7X_MXU_DIM)
    row = lax.broadcasted_iota(jnp.int32, (span, span), 0)
    col = lax.broadcasted_iota(jnp.int32, (span, span), 1)
    tri = jnp.where((row >= col) & (row // c == col // c), 1.0, 0.0).astype(BF16)
    crow = lax.broadcasted_iota(jnp.int32, (c, c), 0)
    ccol = lax.broadcasted_iota(jnp.int32, (c, c), 1)
    causal = crow >= ccol

    proj_ref[...] = _dot(h, w_in_ref[...])

    for ci in range(n_chunks):
        rows = slice(ci * c, (ci + 1) * c)
        forget = _forget_gate(proj_ref[rows, d:2 * d], lb)
        kk_ref[rows, :] = 1.0 - forget
        logf = jnp.log(forget)
        hi = logf.astype(BF16)
        r1 = logf - hi.astype(F32)
        mid = r1.astype(BF16)
        lf_ref[0, rows, :] = hi
        lf_ref[1, rows, :] = mid
        lf_ref[2, rows, :] = (r1 - mid.astype(F32)).astype(BF16)

    for r0 in range(0, tile, span):
        rows = slice(r0, r0 + span)
        b_ref[rows, :] = (_dot(tri, lf_ref[0, rows, :]) + _dot(tri, lf_ref[1, rows, :])
                          + _dot(tri, lf_ref[2, rows, :]))

    for ci in range(n_chunks):
        rows = slice(ci * c, (ci + 1) * c)
        b = b_ref[rows, :]
        b_mid = b_ref[ci * c + c // 2 - 1:ci * c + c // 2, :]
        b_last = b_ref[ci * c + c - 1:ci * c + c, :]
        q = proj_ref[rows, 0:d]
        k = kk_ref[rows, :]
        qi_ref[rows, :] = (q * jnp.exp(b - b_mid)).astype(BF16)
        ki_ref[rows, :] = (k * jnp.exp(b_mid - b)).astype(BF16)
        qn_ref[rows, :] = (q * jnp.exp(b)).astype(BF16)
        ks_ref[rows, :] = (k * jnp.exp(b_last - b)).astype(BF16)
        vb_ref[rows, :] = proj_ref[rows, 2 * d:3 * d].astype(BF16)
        decay = jnp.exp(b_last)
        for hd in range(A_HEADS):
            dec_ref[ci * A_HEADS + hd:ci * A_HEADS + hd + 1, :] = decay[:, hd * dk:(hd + 1) * dk]

    for ci in range(n_chunks):
        rows = slice(ci * c, (ci + 1) * c)
        for hd in range(A_HEADS):
            ls = slice(hd * dk, (hd + 1) * dk)
            att = jnp.where(causal, _dot_nt(qi_ref[rows, ls], ki_ref[rows, ls]), 0.0)
            att_ref[ci * A_HEADS + hd] = att.astype(BF16)

    dec_t = dec_ref[...].T
    for ci in range(n_chunks):
        rows = slice(ci * c, (ci + 1) * c)
        for hd in range(A_HEADS):
            ls = slice(hd * dk, (hd + 1) * dk)
            j = ci * A_HEADS + hd
            st = st_ref[hd]
            lhs = jnp.concatenate([qn_ref[rows, ls], att_ref[j]], axis=1)
            rhs = jnp.concatenate([st.astype(BF16), vb_ref[rows, ls]], axis=0)
            o = _dot(lhs, rhs)
            st_ref[hd] = st * dec_t[:, j:j + 1] + _dot_tn(ks_ref[rows, ls], vb_ref[rows, ls])
            g = proj_ref[rows, 3 * d + hd * dk:3 * d + (hd + 1) * dk]
            a_ref[rows, ls] = (_rms_scale(o) * g_norm * (g * _sigmoid(g))).astype(BF16)

    y_ref[0] = x + _dot(a_ref[...], w_out_ref[...])

    @pl.when(t == pl.num_programs(1) - 1)
    def _():
        s_out_ref[0] = st_ref[...]


def _hgrn_prompt(x, w_in, a_lb, g_mix, g_norm, w_out, layer, tile):
    bsz, seq, d = x.shape
    dk = d // A_HEADS
    tile = min(tile, seq)
    return pl.pallas_call(
        functools.partial(_hgrn_prompt_kernel, layer),
        grid=(bsz, seq // tile),
        in_specs=[
            pl.BlockSpec((1, tile, d), lambda b, t: (b, t, 0)),
            _resident(w_in.shape),
            _resident(a_lb.shape),
            _resident(g_mix.shape),
            _resident(g_norm.shape),
            _resident(w_out.shape),
        ],
        out_specs=[
            pl.BlockSpec((1, tile, d), lambda b, t: (b, t, 0)),
            pl.BlockSpec((1, A_HEADS, dk, dk), lambda b, t: (b, 0, 0, 0)),
        ],
        out_shape=[
            jax.ShapeDtypeStruct((bsz, seq, d), F32),
            jax.ShapeDtypeStruct((bsz, A_HEADS, dk, dk), F32),
        ],
        scratch_shapes=[
            pltpu.VMEM((A_HEADS, dk, dk), F32),
            pltpu.VMEM((tile, 4 * d), F32),
            pltpu.VMEM((3, tile, d), BF16),
            pltpu.VMEM((tile, d), F32),
            pltpu.VMEM((tile, d), F32),
            pltpu.VMEM((tile, d), BF16),
            pltpu.VMEM((tile, d), BF16),
            pltpu.VMEM((tile, d), BF16),
            pltpu.VMEM((tile, d), BF16),
            pltpu.VMEM((tile, d), BF16),
            pltpu.VMEM((dk, dk), F32),
            pltpu.VMEM((tile // A_CHUNK * A_HEADS, A_CHUNK, A_CHUNK), BF16),
            pltpu.VMEM((tile, d), BF16),
        ],
        compiler_params=_cparams("arbitrary", "arbitrary"),
        name="hgrn_prompt",
    )(x, w_in, a_lb, g_mix, g_norm, w_out)


class _Rider(NamedTuple):
    name: str
    items: Callable
    n_items: int
    args: tuple
    in_specs: tuple
    out_shape: tuple
    out_specs: tuple


def _mlp_kernel(ff_chunk, has_final, rider, x_ref, g_ref, w_up_ref, w_down_ref, *rest):
    rest = list(rest)
    g_final_ref = rest.pop(0) if has_final else None
    n_in = len(rider.args) if rider else 0
    y_ref = rest[n_in]
    side = rider.items(*rest[:n_in], *rest[n_in + 1:]) if rider else iter(())
    n_chunks = w_up_ref.shape[1] // ff_chunk
    x = x_ref[...]
    h = (_rms_scale(x) * g_ref[...]).astype(BF16)
    acc = x
    for j in range(n_chunks):
        cols = slice(j * ff_chunk, (j + 1) * ff_chunk)
        u = jnp.maximum(_dot(h, w_up_ref[:, cols]), 0.0)
        acc = acc + _dot((u * u).astype(BF16), w_down_ref[cols, :])
        for _ in range(rider.n_items // n_chunks if rider else 0):
            next(side, None)
    for _ in side:
        pass
    if has_final:
        acc = _rms_scale(acc) * g_final_ref[...]
    y_ref[...] = acc


def _mlp(x, g, w_up, w_down, layer, g_final, tile, rider=None, ff_chunk=1024):
    m, d = x.shape
    tile = min(tile, m)
    has_final = g_final is not None
    in_specs = [
        pl.BlockSpec((tile, d), lambda i: (i, 0)),
        _resident(g.shape),
        _resident_layer(w_up.shape, layer),
        _resident_layer(w_down.shape, layer),
    ]
    args = [x, g, w_up, w_down]
    if has_final:
        in_specs.append(_resident(g_final.shape))
        args.append(g_final)
    out_specs = [pl.BlockSpec((tile, d), lambda i: (i, 0))]
    out_shape = [jax.ShapeDtypeStruct((m, d), F32)]
    name = "mlp_final" if has_final else "mlp"
    if rider:
        in_specs += list(rider.in_specs)
        args += list(rider.args)
        out_specs += list(rider.out_specs)
        out_shape += list(rider.out_shape)
        name += "_with_" + rider.name
    outs = pl.pallas_call(
        functools.partial(_mlp_kernel, ff_chunk, has_final, rider),
        grid=(m // tile,),
        in_specs=in_specs,
        out_specs=out_specs,
        out_shape=out_shape,
        compiler_params=_cparams("arbitrary"),
        name=name,
    )(*args)
    return outs[0], outs[1:]


def _t5_bucket(dist):
    n = jnp.maximum(dist, 0)
    max_exact = N_BUCKETS // 2
    nf = jnp.maximum(n, 1).astype(F32)
    large = max_exact + (jnp.log(nf / max_exact) / math.log(MAX_DISTANCE / max_exact)
                         * (N_BUCKETS - max_exact)).astype(jnp.int32)
    large = jnp.minimum(large, N_BUCKETS - 1)
    return jnp.where(n < max_exact, n, large)


def _band_bias_kernel(rel_bias_ref, out_ref):
    n_heads, _, w = out_ref.shape
    kj = lax.broadcasted_iota(jnp.int32, (2 * w, w), 0)
    qi = lax.broadcasted_iota(jnp.int32, (2 * w, w), 1)
    dist = qi + w - kj
    bucket = _t5_bucket(dist)
    valid = (dist >= 0) & (dist < WINDOW)
    for hd in range(n_heads):
        bias = jnp.zeros((2 * w, w), F32)
        for bk in range(N_BUCKETS):
            bias = jnp.where(bucket == bk, rel_bias_ref[bk, hd], bias)
        out_ref[hd] = jnp.where(valid, bias * LOG2E, NEG)


def _band_bias(rel_bias):
    n_heads = rel_bias.shape[1]
    return pl.pallas_call(
        _band_bias_kernel,
        in_specs=[pl.BlockSpec(memory_space=pltpu.SMEM)],
        out_specs=pl.BlockSpec(memory_space=pltpu.VMEM),
        out_shape=jax.ShapeDtypeStruct((n_heads, 2 * WINDOW, WINDOW), F32),
        name="band_bias",
    )(rel_bias)


def _col_max(x):
    return jnp.max(x, axis=0, keepdims=True)


def _col_sum(x):
    return jnp.sum(x, axis=0, keepdims=True)


def _swa_prompt_kernel(x_ref, g_kv_ref, w_kv_ref, g_q_ref, w_qt_ref, sink_ref, bias_ref, w_out_ref,
                       y_ref, k_out_ref, v_out_ref, k_ref, vt_ref, qt_ref, at_ref, s_ref, p_ref):
    t = pl.program_id(1)
    tile, d = x_ref.shape[1], x_ref.shape[2]
    w = WINDOW
    hd = B_HEAD_DIM
    n_heads = d // hd
    group = n_heads // B_KV_HEADS
    kvd = B_KV_HEADS * hd
    scale = 1.0 / math.sqrt(hd)

    @pl.when(t == 0)
    def _():
        k_ref[:, 0:w, :] = jnp.zeros((B_KV_HEADS, w, hd), BF16)
        vt_ref[:, 0:w] = jnp.zeros((kvd, w), BF16)

    @pl.when(t > 0)
    def _():
        k_ref[:, 0:w, :] = k_ref[:, tile:tile + w, :]
        vt_ref[:, 0:w] = vt_ref[:, tile:tile + w]

    x = x_ref[0]
    xn = _rms_scale(x)
    kv = _dot((xn * g_kv_ref[...]).astype(BF16), w_kv_ref[...])
    for kh in range(B_KV_HEADS):
        k_ref[kh, w:w + tile, :] = kv[:, kh * hd:(kh + 1) * hd].astype(BF16)
    vt_ref[:, w:w + tile] = kv[:, kvd:2 * kvd].T.astype(BF16)
    h_q = (xn * g_q_ref[...]).astype(BF16)
    qt_ref[...] = (_dot_nt(w_qt_ref[...], h_q) * (scale * LOG2E)).astype(BF16)

    @pl.when(t == pl.num_programs(1) - 1)
    def _():
        k_out_ref[0] = kv[tile - w:, 0:kvd].T
        v_out_ref[0] = kv[tile - w:, kvd:2 * kvd].T

    first = t == 0
    sc = 64
    rc = 16
    units = [(blk, kh) for blk in range(tile // w) for kh in range(B_KV_HEADS)]

    def scores(u):
        blk, kh = units[u]
        cols = slice(blk * w, (blk + 1) * w)
        heads = range(kh * group, (kh + 1) * group)
        q4 = jnp.concatenate([qt_ref[h * hd:(h + 1) * hd, cols] for h in heads], axis=1)
        m = None
        for r in range(2 * w // sc):
            s = _dot(k_ref[kh, blk * w + r * sc:blk * w + (r + 1) * sc, :], q4)
            s = s + jnp.concatenate([bias_ref[h, r * sc:(r + 1) * sc, :] for h in heads], axis=1)
            if blk == 0 and r * sc < w:
                s = jnp.where(first, NEG, s)
            s_ref[u % 2, r * sc:(r + 1) * sc, :] = s
            for i in range(sc // 8):
                part = s[i * 8:(i + 1) * 8, :]
                m = part if m is None else jnp.maximum(m, part)
        return m

    def softmax(u, m8):
        blk, kh = units[u]
        heads = range(kh * group, (kh + 1) * group)
        sink = jnp.concatenate([jnp.full((1, w), sink_ref[0, h] * LOG2E, F32) for h in heads],
                               axis=1)
        m = jnp.maximum(_col_max(m8), sink)
        for r in range(2 * w // rc):
            p = jnp.exp2(s_ref[u % 2, r * rc:(r + 1) * rc, :] - m)
            p_ref[u % 2, r * rc:(r + 1) * rc, :] = p.astype(BF16)
        return jnp.exp2(sink - m)

    ones_rows = jnp.ones((rc, 2 * w), BF16)

    def weighted_values(u, sink_term):
        blk, kh = units[u]
        cols = slice(blk * w, (blk + 1) * w)
        keys = slice(blk * w, (blk + 2) * w)
        vt1 = jnp.concatenate([vt_ref[kh * hd:(kh + 1) * hd, keys], ones_rows], axis=0)
        o = _dot(vt1, p_ref[u % 2])
        o = o[0:hd] / (o[hd:hd + 1] + sink_term)
        for g in range(group):
            h = kh * group + g
            at_ref[h * hd:(h + 1) * hd, cols] = o[:, g * w:(g + 1) * w].astype(BF16)

    m_next = scores(0)
    denom_prev = None
    for u in range(len(units)):
        m_cur = m_next
        if u + 1 < len(units):
            m_next = scores(u + 1)
        denom = softmax(u, m_cur)
        if u > 0:
            weighted_values(u - 1, denom_prev)
        denom_prev = denom
    weighted_values(len(units) - 1, denom_prev)

    y_ref[0] = x + _dot_tn(at_ref[...], w_out_ref[...])


def _swa_prompt(x, g_kv, w_kv, g_q, w_qt, sink, band_bias, w_out, tile):
    bsz, seq, d = x.shape
    tile = min(tile, seq)
    kvd = w_kv.shape[1] // 2
    w = WINDOW
    return pl.pallas_call(
        _swa_prompt_kernel,
        grid=(bsz, seq // tile),
        in_specs=[
            pl.BlockSpec((1, tile, d), lambda b, t: (b, t, 0)),
            _resident(g_kv.shape),
            _resident(w_kv.shape),
            _resident(g_q.shape),
            _resident(w_qt.shape),
            pl.BlockSpec(memory_space=pltpu.SMEM),
            _resident(band_bias.shape),
            _resident(w_out.shape),
        ],
        out_specs=[
            pl.BlockSpec((1, tile, d), lambda b, t: (b, t, 0)),
            pl.BlockSpec((1, kvd, w), lambda b, t: (b, 0, 0)),
            pl.BlockSpec((1, kvd, w), lambda b, t: (b, 0, 0)),
        ],
        out_shape=[
            jax.ShapeDtypeStruct((bsz, seq, d), F32),
            jax.ShapeDtypeStruct((bsz, kvd, w), F32),
            jax.ShapeDtypeStruct((bsz, kvd, w), F32),
        ],
        scratch_shapes=[
            pltpu.VMEM((B_KV_HEADS, tile + w, B_HEAD_DIM), BF16),
            pltpu.VMEM((kvd, tile + w), BF16),
            pltpu.VMEM((d, tile), BF16),
            pltpu.VMEM((d, tile), BF16),
            pltpu.VMEM((2, 2 * w, d // B_KV_HEADS * w // B_HEAD_DIM), F32),
            pltpu.VMEM((2, 2 * w, d // B_KV_HEADS * w // B_HEAD_DIM), BF16),
        ],
        compiler_params=_cparams("arbitrary", "arbitrary"),
        name="swa_prompt",
    )(x, g_kv, w_kv, g_q, w_qt, sink, band_bias, w_out)


def _hgrn_sample_gates_kernel(layer, x_ref, w_in_ref, a_lb_ref, g_mix_ref,
                              ft_ref, q_ref, v_ref, g_ref):
    d = x_ref.shape[1]
    dk = d // A_HEADS
    h = (_rms_scale(x_ref[...]) * g_mix_ref[...]).astype(BF16)
    proj = _dot(h, w_in_ref[...])
    forget = _forget_gate(proj[:, d:2 * d], _lower_bound(a_lb_ref[...], layer))
    q_ref[...] = proj[:, 0:d]
    v_ref[...] = proj[:, 2 * d:3 * d]
    g_ref[...] = proj[:, 3 * d:4 * d]
    for hd in range(A_HEADS):
        ft_ref[hd] = forget[:, hd * dk:(hd + 1) * dk].T


def _hgrn_sample_gates(x, w_in, a_lb, g_mix, layer):
    m, d = x.shape
    dk = d // A_HEADS
    vm = pl.BlockSpec(memory_space=pltpu.VMEM)
    return pl.pallas_call(
        functools.partial(_hgrn_sample_gates_kernel, layer),
        in_specs=[vm, vm, vm, vm],
        out_specs=[vm, vm, vm, vm],
        out_shape=[
            jax.ShapeDtypeStruct((A_HEADS, dk, m), F32),
            jax.ShapeDtypeStruct((m, d), F32),
            jax.ShapeDtypeStruct((m, d), F32),
            jax.ShapeDtypeStruct((m, d), F32),
        ],
        compiler_params=pltpu.CompilerParams(vmem_limit_bytes=V7X_VMEM_LIMIT_BYTES),
        name="hgrn_sample_gates",
    )(x, w_in, a_lb, g_mix)


def _hgrn_sample_state_items(ft_ref, q_ref, v_ref, s0_ref, s_ref, o_ref):
    i = pl.program_id(0)
    tb = s0_ref.shape[0]
    dk = s0_ref.shape[2]
    m = ft_ref.shape[2]
    lane = lax.broadcasted_iota(jnp.int32, (dk, m), 1)
    for tk in range(tb):
        pick = lane == i * tb + tk
        for hd in range(A_HEADS):
            ls = slice(hd * dk, (hd + 1) * dk)
            f = jnp.sum(jnp.where(pick, ft_ref[hd], 0.0), axis=1, keepdims=True)
            v = v_ref[0, tk:tk + 1, ls]
            s_new = v + f * (s0_ref[tk, hd] - v)
            s_ref[tk, hd] = s_new
            o = _dot(q_ref[0, :, ls].astype(BF16), s_new.astype(BF16))
            o_ref[0, tk:tk + 1, ls] = o[tk:tk + 1]
            yield


def _hgrn_sample_state_rider(ft, q, v, s0, n_steps):
    m, d = v.shape
    dk = d // A_HEADS
    assert m % n_steps == 0, "the sample group must split evenly over the host call's steps"
    tb = m // n_steps
    rows3 = pl.BlockSpec((1, tb, d), lambda i: (i, 0, 0))
    state = pl.BlockSpec((tb, A_HEADS, dk, dk), lambda i: (i, 0, 0, 0))
    return _Rider(
        name="hgrn_sample_state",
        items=_hgrn_sample_state_items,
        n_items=tb * A_HEADS,
        args=(ft, q.reshape(n_steps, tb, d), v.reshape(n_steps, tb, d), s0),
        in_specs=(_resident(ft.shape), rows3, rows3, state),
        out_shape=(jax.ShapeDtypeStruct(s0.shape, F32),
                   jax.ShapeDtypeStruct((n_steps, tb, d), F32)),
        out_specs=(state, rows3),
    )


def _hgrn_sample_out_kernel(x_ref, o_ref, g_ref, g_norm_ref, w_out_ref, y_ref):
    d = x_ref.shape[1]
    dk = d // A_HEADS
    g = g_ref[...]
    gate = g * _sigmoid(g)
    parts = []
    for hd in range(A_HEADS):
        ls = slice(hd * dk, (hd + 1) * dk)
        parts.append(_rms_scale(o_ref[:, ls]) * g_norm_ref[...] * gate[:, ls])
    a = jnp.concatenate(parts, axis=1).astype(BF16)
    y_ref[...] = x_ref[...] + _dot(a, w_out_ref[...])


def _hgrn_sample_out(x, o, g, g_norm, w_out):
    vm = pl.BlockSpec(memory_space=pltpu.VMEM)
    return pl.pallas_call(
        _hgrn_sample_out_kernel,
        in_specs=[vm] * 5,
        out_specs=vm,
        out_shape=jax.ShapeDtypeStruct(x.shape, F32),
        compiler_params=pltpu.CompilerParams(vmem_limit_bytes=V7X_VMEM_LIMIT_BYTES),
        name="hgrn_sample_out",
    )(x, o, g, g_norm, w_out)


def _swa_sample_proj_kernel(x_ref, g_kv_ref, w_kvt_ref, g_q_ref, w_q_ref, kvt_ref, q_ref):
    scale = 1.0 / math.sqrt(B_HEAD_DIM)
    xn = _rms_scale(x_ref[...])
    kvt_ref[...] = _dot_nt(w_kvt_ref[...], (xn * g_kv_ref[...]).astype(BF16))
    q_ref[...] = (_dot((xn * g_q_ref[...]).astype(BF16), w_q_ref[...])
                  * (scale * LOG2E)).astype(BF16)


def _swa_sample_proj(x, g_kv, w_kvt, g_q, w_q):
    m = x.shape[0]
    vm = pl.BlockSpec(memory_space=pltpu.VMEM)
    return pl.pallas_call(
        _swa_sample_proj_kernel,
        in_specs=[vm] * 5,
        out_specs=[vm, vm],
        out_shape=[
            jax.ShapeDtypeStruct((w_kvt.shape[0], m), F32),
            jax.ShapeDtypeStruct((m, w_q.shape[1]), BF16),
        ],
        compiler_params=pltpu.CompilerParams(vmem_limit_bytes=V7X_VMEM_LIMIT_BYTES),
        name="swa_sample_proj",
    )(x, g_kv, w_kvt, g_q, w_q)


def _swa_sample_attend_items(q_ref, kvt_new_ref, kt_buf_ref, vt_buf_ref, bias_ref, sink_ref,
                             kt_out_ref, vt_out_ref, a_ref):
    i = pl.program_id(0)
    tb, kvd, n_buf = kt_buf_ref.shape
    hd = B_HEAD_DIM
    n_heads = q_ref.shape[0] // tb
    group = n_heads // B_KV_HEADS
    m_tok = kvt_new_ref.shape[1]
    tok_lane = lax.broadcasted_iota(jnp.int32, (2 * kvd, m_tok), 1)
    key_lane = lax.broadcasted_iota(jnp.int32, (kvd, n_buf), 1)
    for tk in range(tb):
        new_col = jnp.sum(jnp.where(tok_lane == i * tb + tk, kvt_new_ref[...], 0.0),
                          axis=1, keepdims=True)
        for buf_ref, out_ref, rows in ((kt_buf_ref, kt_out_ref, slice(0, kvd)),
                                       (vt_buf_ref, vt_out_ref, slice(kvd, 2 * kvd))):
            slid = pltpu.roll(buf_ref[tk], n_buf - 1, 1)
            out_ref[tk] = jnp.where(key_lane == n_buf - 1, new_col[rows], slid)
        yield

    e_row = lax.broadcasted_iota(jnp.int32, (hd, kvd), 0)
    e_col = lax.broadcasted_iota(jnp.int32, (hd, kvd), 1)
    spread = jnp.where(e_col % hd == e_row, 1.0, 0.0).astype(BF16)
    head = lax.broadcasted_iota(jnp.int32, (tb, n_heads, kvd), 1)
    slot = lax.broadcasted_iota(jnp.int32, (tb, n_heads, kvd), 2)
    own_slot = head // group == slot // hd
    qx = _dot(q_ref[...], spread).reshape(tb, n_heads, kvd)
    qx = jnp.where(own_slot, qx, 0.0).astype(BF16)
    yield

    kt = kt_out_ref[...].astype(BF16)
    vt = vt_out_ref[...].astype(BF16)
    s = jnp.einsum("bhc,bcr->bhr", qx, kt, preferred_element_type=F32)
    s = s + bias_ref[...][None]
    sink = sink_ref[...][None] * LOG2E
    m = jnp.maximum(jnp.max(s, axis=-1, keepdims=True), sink)
    p = jnp.exp2(s - m)
    denom = jnp.sum(p, axis=-1, keepdims=True) + jnp.exp2(sink - m)
    yield
    r = jnp.einsum("bhr,bcr->bhc", p.astype(BF16), vt, preferred_element_type=F32) / denom
    r = jnp.where(own_slot, r, 0.0).astype(BF16).reshape(tb * n_heads, kvd)
    a_ref[...] = _dot_nt(r, spread).astype(BF16)
    yield


def _swa_sample_attend_rider(q_rows, kvt_new, kt_buf, vt_buf, bias_row, sink_col, n_steps):
    m, kvd, n_buf = kt_buf.shape
    n_heads = q_rows.shape[0] // m
    assert m % n_steps == 0, "the sample group must split evenly over the host call's steps"
    tb = m // n_steps
    cache = pl.BlockSpec((tb, kvd, n_buf), lambda i: (i, 0, 0))
    head_rows = pl.BlockSpec((tb * n_heads, B_HEAD_DIM), lambda i: (i, 0))
    return _Rider(
        name="swa_sample_attend",
        items=_swa_sample_attend_items,
        n_items=tb + 3,
        args=(q_rows, kvt_new, kt_buf, vt_buf, bias_row, sink_col),
        in_specs=(head_rows, _resident(kvt_new.shape), cache, cache,
                  _resident(bias_row.shape), _resident(sink_col.shape)),
        out_shape=(jax.ShapeDtypeStruct((m, kvd, n_buf), F32),
                   jax.ShapeDtypeStruct((m, kvd, n_buf), F32),
                   jax.ShapeDtypeStruct((m * n_heads, B_HEAD_DIM), BF16)),
        out_specs=(cache, cache, head_rows),
    )


def _swa_sample_out_kernel(x_ref, a_ref, w_out_ref, y_ref):
    y_ref[...] = x_ref[...] + _dot(a_ref[...], w_out_ref[...])


def _swa_sample_out(x, a, w_out):
    vm = pl.BlockSpec(memory_space=pltpu.VMEM)
    return pl.pallas_call(
        _swa_sample_out_kernel,
        in_specs=[vm] * 3,
        out_specs=vm,
        out_shape=jax.ShapeDtypeStruct(x.shape, F32),
        compiler_params=pltpu.CompilerParams(vmem_limit_bytes=V7X_VMEM_LIMIT_BYTES),
        name="swa_sample_out",
    )(x, a, w_out)


def _cache_keys_minor(cache):
    b, n, kvh, hd = cache.shape
    return jnp.transpose(cache, (0, 2, 3, 1)).reshape(b, kvh * hd, n)


def _cache_keys_major(cache_t):
    b, kvd, n = cache_t.shape
    return jnp.transpose(cache_t.reshape(b, B_KV_HEADS, kvd // B_KV_HEADS, n), (0, 3, 1, 2))


def kernel(x_prompt, x_sample, state_hgrn, cache_k_win, cache_v_win, w_a_in, a_lb, a_gnorm, w_a_out,
           g_mix, g_mlp, g_kv, w_kv, w_b_q, b_sink, w_b_out, rel_bias, w_up, w_down, g_final):
    bsz, seq, d = x_prompt.shape
    n_dec = x_sample.shape[0]
    assert x_sample.shape[1] == 1, "the sample group decodes one token per sequence"
    n_a = w_a_in.shape[0]
    n_b = w_b_q.shape[0]
    assert n_a == 1 and n_b == 1, "depth-2 trunk: one HGRN2 layer, then one attention layer"
    n_buf = cache_k_win.shape[1]
    assert n_buf == WINDOW and seq % WINDOW == 0
    kvd = B_KV_HEADS * B_HEAD_DIM
    n_heads = d // B_HEAD_DIM

    bf = lambda w: w.astype(BF16)
    row = lambda g: g.reshape(1, -1)
    w_in, w_ao = bf(w_a_in[0]), bf(w_a_out[0])
    w_kvb, w_q, w_bo = bf(w_kv), bf(w_b_q[0]), bf(w_b_out[0])
    w_up_b, w_down_b = bf(w_up), bf(w_down)
    band_bias = _band_bias(rel_bias)
    sink = b_sink[0].reshape(1, n_heads)

    mlp_tile = min(512, bsz * seq)
    mlp_steps = bsz * seq // mlp_tile
    xs = x_sample.reshape(n_dec, d)

    ft, q, v, g = _hgrn_sample_gates(xs, w_in, a_lb, row(g_mix[0]), layer=0)
    x, s_prompt = _hgrn_prompt(x_prompt, w_in, a_lb, row(g_mix[0]), row(a_gnorm[0]), w_ao,
                               layer=0, tile=512)
    x, (s_sample, o) = _mlp(
        x.reshape(bsz * seq, d), row(g_mlp[0]), w_up_b, w_down_b, 0, None, tile=mlp_tile,
        rider=_hgrn_sample_state_rider(ft, q, v, state_hgrn[0], mlp_steps))
    xs = _hgrn_sample_out(xs, o.reshape(n_dec, d), g, row(a_gnorm[0]), w_ao)
    xs, _ = _mlp(xs, row(g_mlp[0]), w_up_b, w_down_b, 0, None, tile=n_dec)

    kvt_new, q = _swa_sample_proj(xs, row(g_kv), w_kvb.T, row(g_mix[1]), w_q)
    x, kt_prompt, vt_prompt = _swa_prompt(x.reshape(bsz, seq, d), row(g_kv), w_kvb,
                                          row(g_mix[1]), w_q.T, sink, band_bias, w_bo, tile=512)
    bias_row = band_bias[:, 1:n_buf + 1, 0]
    y_prompt, (kt_sample, vt_sample, a) = _mlp(
        x.reshape(bsz * seq, d), row(g_mlp[1]), w_up_b, w_down_b, 1, row(g_final),
        tile=mlp_tile,
        rider=_swa_sample_attend_rider(
            q.reshape(n_dec * n_heads, B_HEAD_DIM), kvt_new,
            _cache_keys_minor(cache_k_win), _cache_keys_minor(cache_v_win),
            bias_row, sink.reshape(n_heads, 1), mlp_steps))
    y_prompt = y_prompt.reshape(bsz, seq, d)
    xs = _swa_sample_out(xs, a.reshape(n_dec, d), w_bo)
    y_sample, _ = _mlp(xs, row(g_mlp[1]), w_up_b, w_down_b, 1, row(g_final), tile=n_dec)
    y_sample = y_sample.reshape(n_dec, 1, d)

    return (y_prompt, y_sample, s_prompt[None], s_sample[None],
            _cache_keys_major(kt_prompt), _cache_keys_major(vt_prompt),
            _cache_keys_major(kt_sample), _cache_keys_major(vt_sample))
```

```python
import functools
import math
from typing import Callable, NamedTuple

import jax
import jax.numpy as jnp
from jax import lax
from jax.experimental import pallas as pl
from jax.experimental.pallas import tpu as pltpu

F32 = jnp.float32
BF16 = jnp.bfloat16

EPS = 1e-6
NEG = -1e30
LOG2E = math.log2(math.e)
A_HEADS = 8
A_CHUNK = 64
B_HEAD_DIM = 64
B_KV_HEADS = 4
WINDOW = 128
N_BUCKETS = 32
MAX_DISTANCE = 128

V7X_VMEM_LIMIT_BYTES = 56 * 1024 * 1024
V7X_MXU_DIM = 256


def _cparams(*semantics):
    return pltpu.CompilerParams(dimension_semantics=semantics,
                                vmem_limit_bytes=V7X_VMEM_LIMIT_BYTES)


def _resident(shape):
    nd = len(shape)
    return pl.BlockSpec(shape, lambda *_: (0,) * nd, pipeline_mode=pl.Buffered(1))


def _resident_layer(shape, layer):
    nd = len(shape)
    return pl.BlockSpec((None,) + tuple(shape[1:]), lambda *_: (layer,) + (0,) * (nd - 1),
                        pipeline_mode=pl.Buffered(1))


def _rms_scale(x):
    return x * lax.rsqrt(jnp.mean(x * x, axis=-1, keepdims=True) + EPS)


def _sigmoid(x):
    return 1.0 / (1.0 + jnp.exp(-x))


def _dot(a, b):
    return jnp.dot(a, b, preferred_element_type=F32)


def _dot_nt(a, b):
    return lax.dot_general(a, b, (((1,), (1,)), ((), ())), preferred_element_type=F32)


def _dot_tn(a, b):
    return lax.dot_general(a, b, (((0,), (0,)), ((), ())), preferred_element_type=F32)


def _lower_bound(a_lb, layer):
    m = jnp.max(a_lb, axis=0, keepdims=True)
    e = jnp.exp(a_lb - m)
    return jnp.sum(e[: layer + 1], axis=0, keepdims=True) / jnp.sum(e, axis=0, keepdims=True)


def _forget_gate(f_raw, lb):
    return lb + (1.0 - lb) * _sigmoid(f_raw)


def _hgrn_prompt_kernel(layer, rider, x_ref, w_in_ref, a_lb_ref, g_mix_ref, g_norm_ref,
                        w_out_ref, *rest):
    n_in = len(rider.args) if rider else 0
    n_out = len(rider.out_shape) if rider else 0
    y_ref, s_out_ref = rest[n_in:n_in + 2]
    (st_ref, proj_ref, lf_ref, kk_ref, b_ref, qi_ref, ki_ref, qn_ref, ks_ref, vb_ref,
     dec_ref, att_ref, a_ref) = rest[n_in + 2 + n_out:]
    if rider:
        for _ in rider.items(*rest[:n_in], *rest[n_in + 2:n_in + 2 + n_out]):
            pass
    t = pl.program_id(1)
    tile, d = x_ref.shape[1], x_ref.shape[2]
    dk = d // A_HEADS
    c = A_CHUNK
    n_chunks = tile // c

    @pl.when(t == 0)
    def _():
        st_ref[...] = jnp.zeros_like(st_ref)
        dec_ref[...] = jnp.zeros_like(dec_ref)

    x = x_ref[0]
    h = (_rms_scale(x) * g_mix_ref[...]).astype(BF16)
    lb = _lower_bound(a_lb_ref[...], layer)
    g_norm = g_norm_ref[...]

    span = min(tile, V7X_MXU_DIM)
    row = lax.broadcasted_iota(jnp.int32, (span, span), 0)
    col = lax.broadcasted_iota(jnp.int32, (span, span), 1)
    tri = jnp.where((row >= col) & (row // c == col // c), 1.0, 0.0).astype(BF16)
    crow = lax.broadcasted_iota(jnp.int32, (c, c), 0)
    ccol = lax.broadcasted_iota(jnp.int32, (c, c), 1)
    causal = crow >= ccol

    proj_ref[...] = _dot(h, w_in_ref[...])

    for ci in range(n_chunks):
        rows = slice(ci * c, (ci + 1) * c)
        forget = _forget_gate(proj_ref[rows, d:2 * d], lb)
        kk_ref[rows, :] = 1.0 - forget
        logf = jnp.log(forget)
        hi = logf.astype(BF16)
        r1 = logf - hi.astype(F32)
        mid = r1.astype(BF16)
        lf_ref[0, rows, :] = hi
        lf_ref[1, rows, :] = mid
        lf_ref[2, rows, :] = (r1 - mid.astype(F32)).astype(BF16)

    for r0 in range(0, tile, span):
        rows = slice(r0, r0 + span)
        b_ref[rows, :] = (_dot(tri, lf_ref[0, rows, :]) + _dot(tri, lf_ref[1, rows, :])
                          + _dot(tri, lf_ref[2, rows, :]))

    for ci in range(n_chunks):
        rows = slice(ci * c, (ci + 1) * c)
        b = b_ref[rows, :]
        b_mid = b_ref[ci * c + c // 2 - 1:ci * c + c // 2, :]
        b_last = b_ref[ci * c + c - 1:ci * c + c, :]
        q = proj_ref[rows, 0:d]
        k = kk_ref[rows, :]
        qi_ref[rows, :] = (q * jnp.exp(b - b_mid)).astype(BF16)
        ki_ref[rows, :] = (k * jnp.exp(b_mid - b)).astype(BF16)
        qn_ref[rows, :] = (q * jnp.exp(b)).astype(BF16)
        ks_ref[rows, :] = (k * jnp.exp(b_last - b)).astype(BF16)
        vb_ref[rows, :] = proj_ref[rows, 2 * d:3 * d].astype(BF16)
        decay = jnp.exp(b_last)
        for hd in range(A_HEADS):
            dec_ref[ci * A_HEADS + hd:ci * A_HEADS + hd + 1, :] = decay[:, hd * dk:(hd + 1) * dk]

    for ci in range(n_chunks):
        rows = slice(ci * c, (ci + 1) * c)
        for hd in range(A_HEADS):
            ls = slice(hd * dk, (hd + 1) * dk)
            att = jnp.where(causal, _dot_nt(qi_ref[rows, ls], ki_ref[rows, ls]), 0.0)
            att_ref[ci * A_HEADS + hd] = att.astype(BF16)

    dec_t = dec_ref[...].T
    for ci in range(n_chunks):
        rows = slice(ci * c, (ci + 1) * c)
        for hd in range(A_HEADS):
            ls = slice(hd * dk, (hd + 1) * dk)
            j = ci * A_HEADS + hd
            st = st_ref[hd]
            lhs = jnp.concatenate([qn_ref[rows, ls], att_ref[j]], axis=1)
            rhs = jnp.concatenate([st.astype(BF16), vb_ref[rows, ls]], axis=0)
            o = _dot(lhs, rhs)
            st_ref[hd] = st * dec_t[:, j:j + 1] + _dot_tn(ks_ref[rows, ls], vb_ref[rows, ls])
            g = proj_ref[rows, 3 * d + hd * dk:3 * d + (hd + 1) * dk]
            a_ref[rows, ls] = (_rms_scale(o) * g_norm * (g * _sigmoid(g))).astype(BF16)

    y_ref[0] = x + _dot(a_ref[...], w_out_ref[...])

    @pl.when(t == pl.num_programs(1) - 1)
    def _():
        s_out_ref[0] = st_ref[...]


def _cast_items(*refs):
    n = len(refs) // 2
    for src_ref, dst_ref in zip(refs[:n], refs[n:]):
        dst_ref[...] = src_ref[...].astype(dst_ref.dtype)
        yield


def _bf16_cast_rider(arrays, n_outer, n_inner):
    n_steps = n_outer * n_inner
    flat = tuple(a.reshape(-1, a.shape[-1]) for a in arrays)
    specs = []
    for a in flat:
        rows = a.shape[0] // n_steps
        assert rows * n_steps == a.shape[0] and rows % 16 == 0, a.shape
        specs.append(pl.BlockSpec((rows, a.shape[1]), lambda b, t: (b * n_inner + t, 0)))
    return _Rider(
        name="bf16_cast",
        items=_cast_items,
        n_items=len(flat),
        args=flat,
        in_specs=tuple(specs),
        out_shape=tuple(jax.ShapeDtypeStruct(a.shape, BF16) for a in flat),
        out_specs=tuple(specs),
    )


def _hgrn_prompt(x, w_in, a_lb, g_mix, g_norm, w_out, layer, tile, make_rider=None):
    bsz, seq, d = x.shape
    dk = d // A_HEADS
    tile = min(tile, seq)
    rider = make_rider(bsz, seq // tile) if make_rider else None
    outs = pl.pallas_call(
        functools.partial(_hgrn_prompt_kernel, layer, rider),
        grid=(bsz, seq // tile),
        in_specs=[
            pl.BlockSpec((1, tile, d), lambda b, t: (b, t, 0)),
            _resident(w_in.shape),
            _resident(a_lb.shape),
            _resident(g_mix.shape),
            _resident(g_norm.shape),
            _resident(w_out.shape),
        ] + list(rider.in_specs if rider else ()),
        out_specs=[
            pl.BlockSpec((1, tile, d), lambda b, t: (b, t, 0)),
            pl.BlockSpec((1, A_HEADS, dk, dk), lambda b, t: (b, 0, 0, 0)),
        ] + list(rider.out_specs if rider else ()),
        out_shape=[
            jax.ShapeDtypeStruct((bsz, seq, d), F32),
            jax.ShapeDtypeStruct((bsz, A_HEADS, dk, dk), F32),
        ] + list(rider.out_shape if rider else ()),
        scratch_shapes=[
            pltpu.VMEM((A_HEADS, dk, dk), F32),
            pltpu.VMEM((tile, 4 * d), F32),
            pltpu.VMEM((3, tile, d), BF16),
            pltpu.VMEM((tile, d), F32),
            pltpu.VMEM((tile, d), F32),
            pltpu.VMEM((tile, d), BF16),
            pltpu.VMEM((tile, d), BF16),
            pltpu.VMEM((tile, d), BF16),
            pltpu.VMEM((tile, d), BF16),
            pltpu.VMEM((tile, d), BF16),
            pltpu.VMEM((dk, dk), F32),
            pltpu.VMEM((tile // A_CHUNK * A_HEADS, A_CHUNK, A_CHUNK), BF16),
            pltpu.VMEM((tile, d), BF16),
        ],
        compiler_params=_cparams("arbitrary", "arbitrary"),
        name="hgrn_prompt_with_" + rider.name if rider else "hgrn_prompt",
    )(x, w_in, a_lb, g_mix, g_norm, w_out, *(rider.args if rider else ()))
    return outs[0], outs[1], outs[2:]


class _Rider(NamedTuple):
    name: str
    items: Callable
    n_items: int
    args: tuple
    in_specs: tuple
    out_shape: tuple
    out_specs: tuple


def _mlp_kernel(ff_chunk, has_final, rider, x_ref, g_ref, w_up_ref, w_down_ref, *rest):
    rest = list(rest)
    g_final_ref = rest.pop(0) if has_final else None
    n_in = len(rider.args) if rider else 0
    y_ref = rest[n_in]
    side = rider.items(*rest[:n_in], *rest[n_in + 1:]) if rider else iter(())
    n_chunks = w_up_ref.shape[1] // ff_chunk
    x = x_ref[...]
    h = (_rms_scale(x) * g_ref[...]).astype(BF16)
    acc = x
    for j in range(n_chunks):
        cols = slice(j * ff_chunk, (j + 1) * ff_chunk)
        u = jnp.maximum(_dot(h, w_up_ref[:, cols]), 0.0)
        acc = acc + _dot((u * u).astype(BF16), w_down_ref[cols, :])
        for _ in range(rider.n_items // n_chunks if rider else 0):
            next(side, None)
    for _ in side:
        pass
    if has_final:
        acc = _rms_scale(acc) * g_final_ref[...]
    y_ref[...] = acc


def _mlp(x, g, w_up, w_down, layer, g_final, tile, rider=None, ff_chunk=1024):
    m, d = x.shape
    tile = min(tile, m)
    has_final = g_final is not None
    in_specs = [
        pl.BlockSpec((tile, d), lambda i: (i, 0)),
        _resident(g.shape),
        _resident_layer(w_up.shape, layer),
        _resident_layer(w_down.shape, layer),
    ]
    args = [x, g, w_up, w_down]
    if has_final:
        in_specs.append(_resident(g_final.shape))
        args.append(g_final)
    out_specs = [pl.BlockSpec((tile, d), lambda i: (i, 0))]
    out_shape = [jax.ShapeDtypeStruct((m, d), F32)]
    name = "mlp_final" if has_final else "mlp"
    if rider:
        in_specs += list(rider.in_specs)
        args += list(rider.args)
        out_specs += list(rider.out_specs)
        out_shape += list(rider.out_shape)
        name += "_with_" + rider.name
    outs = pl.pallas_call(
        functools.partial(_mlp_kernel, ff_chunk, has_final, rider),
        grid=(m // tile,),
        in_specs=in_specs,
        out_specs=out_specs,
        out_shape=out_shape,
        compiler_params=_cparams("arbitrary"),
        name=name,
    )(*args)
    return outs[0], outs[1:]


def _t5_bucket(dist):
    n = jnp.maximum(dist, 0)
    max_exact = N_BUCKETS // 2
    nf = jnp.maximum(n, 1).astype(F32)
    large = max_exact + (jnp.log(nf / max_exact) / math.log(MAX_DISTANCE / max_exact)
                         * (N_BUCKETS - max_exact)).astype(jnp.int32)
    large = jnp.minimum(large, N_BUCKETS - 1)
    return jnp.where(n < max_exact, n, large)


def _band_bias_kernel(rel_bias_ref, out_ref):
    n_heads, _, w = out_ref.shape
    kj = lax.broadcasted_iota(jnp.int32, (2 * w, w), 0)
    qi = lax.broadcasted_iota(jnp.int32, (2 * w, w), 1)
    dist = qi + w - kj
    bucket = _t5_bucket(dist)
    valid = (dist >= 0) & (dist < WINDOW)
    for hd in range(n_heads):
        bias = jnp.zeros((2 * w, w), F32)
        for bk in range(N_BUCKETS):
            bias = jnp.where(bucket == bk, rel_bias_ref[bk, hd], bias)
        out_ref[hd] = jnp.where(valid, bias * LOG2E, NEG)


def _band_bias(rel_bias):
    n_heads = rel_bias.shape[1]
    return pl.pallas_call(
        _band_bias_kernel,
        in_specs=[pl.BlockSpec(memory_space=pltpu.SMEM)],
        out_specs=pl.BlockSpec(memory_space=pltpu.VMEM),
        out_shape=jax.ShapeDtypeStruct((n_heads, 2 * WINDOW, WINDOW), F32),
        name="band_bias",
    )(rel_bias)


def _col_max(x):
    return jnp.max(x, axis=0, keepdims=True)


def _col_sum(x):
    return jnp.sum(x, axis=0, keepdims=True)


def _swa_prompt_kernel(x_ref, g_kv_ref, w_kv_ref, g_q_ref, w_qt_ref, sink_ref, bias_ref, w_out_ref,
                       y_ref, k_out_ref, v_out_ref, k_ref, vt_ref, qt_ref, at_ref, s_ref, p_ref):
    t = pl.program_id(1)
    tile, d = x_ref.shape[1], x_ref.shape[2]
    w = WINDOW
    hd = B_HEAD_DIM
    n_heads = d // hd
    group = n_heads // B_KV_HEADS
    kvd = B_KV_HEADS * hd
    scale = 1.0 / math.sqrt(hd)

    @pl.when(t == 0)
    def _():
        k_ref[:, 0:w, :] = jnp.zeros((B_KV_HEADS, w, hd), BF16)
        vt_ref[:, 0:w] = jnp.zeros((kvd, w), BF16)

    @pl.when(t > 0)
    def _():
        k_ref[:, 0:w, :] = k_ref[:, tile:tile + w, :]
        vt_ref[:, 0:w] = vt_ref[:, tile:tile + w]

    x = x_ref[0]
    xn = _rms_scale(x)
    kv = _dot((xn * g_kv_ref[...]).astype(BF16), w_kv_ref[...])
    for kh in range(B_KV_HEADS):
        k_ref[kh, w:w + tile, :] = kv[:, kh * hd:(kh + 1) * hd].astype(BF16)
    vt_ref[:, w:w + tile] = kv[:, kvd:2 * kvd].T.astype(BF16)
    h_q = (xn * g_q_ref[...]).astype(BF16)
    qt_ref[...] = (_dot_nt(w_qt_ref[...], h_q) * (scale * LOG2E)).astype(BF16)

    @pl.when(t == pl.num_programs(1) - 1)
    def _():
        k_out_ref[0] = kv[tile - w:, 0:kvd].T
        v_out_ref[0] = kv[tile - w:, kvd:2 * kvd].T

    first = t == 0
    sc = 64
    rc = 16
    units = [(blk, kh) for blk in range(tile // w) for kh in range(B_KV_HEADS)]

    def scores(u):
        blk, kh = units[u]
        cols = slice(blk * w, (blk + 1) * w)
        heads = range(kh * group, (kh + 1) * group)
        q4 = jnp.concatenate([qt_ref[h * hd:(h + 1) * hd, cols] for h in heads], axis=1)
        m = None
        for r in range(2 * w // sc):
            s = _dot(k_ref[kh, blk * w + r * sc:blk * w + (r + 1) * sc, :], q4)
            s = s + jnp.concatenate([bias_ref[h, r * sc:(r + 1) * sc, :] for h in heads], axis=1)
            if blk == 0 and r * sc < w:
                s = jnp.where(first, NEG, s)
            s_ref[u % 2, r * sc:(r + 1) * sc, :] = s
            for i in range(sc // 8):
                part = s[i * 8:(i + 1) * 8, :]
                m = part if m is None else jnp.maximum(m, part)
        return m

    def softmax(u, m8):
        blk, kh = units[u]
        heads = range(kh * group, (kh + 1) * group)
        sink = jnp.concatenate([jnp.full((1, w), sink_ref[0, h] * LOG2E, F32) for h in heads],
                               axis=1)
        m = jnp.maximum(_col_max(m8), sink)
        for r in range(2 * w // rc):
            p = jnp.exp2(s_ref[u % 2, r * rc:(r + 1) * rc, :] - m)
            p_ref[u % 2, r * rc:(r + 1) * rc, :] = p.astype(BF16)
        return jnp.exp2(sink - m)

    ones_rows = jnp.ones((rc, 2 * w), BF16)

    def weighted_values(u, sink_term):
        blk, kh = units[u]
        cols = slice(blk * w, (blk + 1) * w)
        keys = slice(blk * w, (blk + 2) * w)
        vt1 = jnp.concatenate([vt_ref[kh * hd:(kh + 1) * hd, keys], ones_rows], axis=0)
        o = _dot(vt1, p_ref[u % 2])
        o = o[0:hd] / (o[hd:hd + 1] + sink_term)
        for g in range(group):
            h = kh * group + g
            at_ref[h * hd:(h + 1) * hd, cols] = o[:, g * w:(g + 1) * w].astype(BF16)

    m_next = scores(0)
    denom_prev = None
    for u in range(len(units)):
        m_cur = m_next
        if u + 1 < len(units):
            m_next = scores(u + 1)
        denom = softmax(u, m_cur)
        if u > 0:
            weighted_values(u - 1, denom_prev)
        denom_prev = denom
    weighted_values(len(units) - 1, denom_prev)

    y_ref[0] = x + _dot_tn(at_ref[...], w_out_ref[...])


def _swa_prompt(x, g_kv, w_kv, g_q, w_qt, sink, band_bias, w_out, tile):
    bsz, seq, d = x.shape
    tile = min(tile, seq)
    kvd = w_kv.shape[1] // 2
    w = WINDOW
    return pl.pallas_call(
        _swa_prompt_kernel,
        grid=(bsz, seq // tile),
        in_specs=[
            pl.BlockSpec((1, tile, d), lambda b, t: (b, t, 0)),
            _resident(g_kv.shape),
            _resident(w_kv.shape),
            _resident(g_q.shape),
            _resident(w_qt.shape),
            pl.BlockSpec(memory_space=pltpu.SMEM),
            _resident(band_bias.shape),
            _resident(w_out.shape),
        ],
        out_specs=[
            pl.BlockSpec((1, tile, d), lambda b, t: (b, t, 0)),
            pl.BlockSpec((1, kvd, w), lambda b, t: (b, 0, 0)),
            pl.BlockSpec((1, kvd, w), lambda b, t: (b, 0, 0)),
        ],
        out_shape=[
            jax.ShapeDtypeStruct((bsz, seq, d), F32),
            jax.ShapeDtypeStruct((bsz, kvd, w), F32),
            jax.ShapeDtypeStruct((bsz, kvd, w), F32),
        ],
        scratch_shapes=[
            pltpu.VMEM((B_KV_HEADS, tile + w, B_HEAD_DIM), BF16),
            pltpu.VMEM((kvd, tile + w), BF16),
            pltpu.VMEM((d, tile), BF16),
            pltpu.VMEM((d, tile), BF16),
            pltpu.VMEM((2, 2 * w, d // B_KV_HEADS * w // B_HEAD_DIM), F32),
            pltpu.VMEM((2, 2 * w, d // B_KV_HEADS * w // B_HEAD_DIM), BF16),
        ],
        compiler_params=_cparams("arbitrary", "arbitrary"),
        name="swa_prompt",
    )(x, g_kv, w_kv, g_q, w_qt, sink, band_bias, w_out)


def _hgrn_sample_gates_kernel(layer, x_ref, w_in_ref, a_lb_ref, g_mix_ref,
                              ft_ref, q_ref, v_ref, g_ref):
    d = x_ref.shape[1]
    dk = d // A_HEADS
    h = (_rms_scale(x_ref[...]) * g_mix_ref[...]).astype(BF16)
    proj = _dot(h, w_in_ref[...])
    forget = _forget_gate(proj[:, d:2 * d], _lower_bound(a_lb_ref[...], layer))
    q_ref[...] = proj[:, 0:d]
    v_ref[...] = proj[:, 2 * d:3 * d]
    g_ref[...] = proj[:, 3 * d:4 * d]
    for hd in range(A_HEADS):
        ft_ref[hd] = forget[:, hd * dk:(hd + 1) * dk].T


def _hgrn_sample_gates(x, w_in, a_lb, g_mix, layer):
    m, d = x.shape
    dk = d // A_HEADS
    vm = pl.BlockSpec(memory_space=pltpu.VMEM)
    return pl.pallas_call(
        functools.partial(_hgrn_sample_gates_kernel, layer),
        in_specs=[vm, vm, vm, vm],
        out_specs=[vm, vm, vm, vm],
        out_shape=[
            jax.ShapeDtypeStruct((A_HEADS, dk, m), F32),
            jax.ShapeDtypeStruct((m, d), F32),
            jax.ShapeDtypeStruct((m, d), F32),
            jax.ShapeDtypeStruct((m, d), F32),
        ],
        compiler_params=pltpu.CompilerParams(vmem_limit_bytes=V7X_VMEM_LIMIT_BYTES),
        name="hgrn_sample_gates",
    )(x, w_in, a_lb, g_mix)


def _hgrn_sample_state_items(ft_ref, q_ref, v_ref, s0_ref, s_ref, o_ref):
    i = pl.program_id(0)
    tb = s0_ref.shape[0]
    dk = s0_ref.shape[2]
    m = ft_ref.shape[2]
    lane = lax.broadcasted_iota(jnp.int32, (dk, m), 1)
    for tk in range(tb):
        pick = lane == i * tb + tk
        for hd in range(A_HEADS):
            ls = slice(hd * dk, (hd + 1) * dk)
            f = jnp.sum(jnp.where(pick, ft_ref[hd], 0.0), axis=1, keepdims=True)
            v = v_ref[0, tk:tk + 1, ls]
            s_new = v + f * (s0_ref[tk, hd] - v)
            s_ref[tk, hd] = s_new
            o = _dot(q_ref[0, :, ls].astype(BF16), s_new.astype(BF16))
            o_ref[0, tk:tk + 1, ls] = o[tk:tk + 1]
            yield


def _hgrn_sample_state_rider(ft, q, v, s0, n_steps):
    m, d = v.shape
    dk = d // A_HEADS
    assert m % n_steps == 0, "the sample group must split evenly over the host call's steps"
    tb = m // n_steps
    rows3 = pl.BlockSpec((1, tb, d), lambda i: (i, 0, 0))
    state = pl.BlockSpec((tb, A_HEADS, dk, dk), lambda i: (i, 0, 0, 0))
    return _Rider(
        name="hgrn_sample_state",
        items=_hgrn_sample_state_items,
        n_items=tb * A_HEADS,
        args=(ft, q.reshape(n_steps, tb, d), v.reshape(n_steps, tb, d), s0),
        in_specs=(_resident(ft.shape), rows3, rows3, state),
        out_shape=(jax.ShapeDtypeStruct(s0.shape, F32),
                   jax.ShapeDtypeStruct((n_steps, tb, d), F32)),
        out_specs=(state, rows3),
    )


def _hgrn_sample_out_kernel(x_ref, o_ref, g_ref, g_norm_ref, w_out_ref, y_ref):
    d = x_ref.shape[1]
    dk = d // A_HEADS
    g = g_ref[...]
    gate = g * _sigmoid(g)
    parts = []
    for hd in range(A_HEADS):
        ls = slice(hd * dk, (hd + 1) * dk)
        parts.append(_rms_scale(o_ref[:, ls]) * g_norm_ref[...] * gate[:, ls])
    a = jnp.concatenate(parts, axis=1).astype(BF16)
    y_ref[...] = x_ref[...] + _dot(a, w_out_ref[...])


def _hgrn_sample_out(x, o, g, g_norm, w_out):
    vm = pl.BlockSpec(memory_space=pltpu.VMEM)
    return pl.pallas_call(
        _hgrn_sample_out_kernel,
        in_specs=[vm] * 5,
        out_specs=vm,
        out_shape=jax.ShapeDtypeStruct(x.shape, F32),
        compiler_params=pltpu.CompilerParams(vmem_limit_bytes=V7X_VMEM_LIMIT_BYTES),
        name="hgrn_sample_out",
    )(x, o, g, g_norm, w_out)


def _swa_sample_proj_kernel(x_ref, g_kv_ref, w_kvt_ref, g_q_ref, w_q_ref, kvt_ref, q_ref):
    scale = 1.0 / math.sqrt(B_HEAD_DIM)
    xn = _rms_scale(x_ref[...])
    kvt_ref[...] = _dot_nt(w_kvt_ref[...], (xn * g_kv_ref[...]).astype(BF16))
    q_ref[...] = (_dot((xn * g_q_ref[...]).astype(BF16), w_q_ref[...])
                  * (scale * LOG2E)).astype(BF16)


def _swa_sample_proj(x, g_kv, w_kvt, g_q, w_q):
    m = x.shape[0]
    vm = pl.BlockSpec(memory_space=pltpu.VMEM)
    return pl.pallas_call(
        _swa_sample_proj_kernel,
        in_specs=[vm] * 5,
        out_specs=[vm, vm],
        out_shape=[
            jax.ShapeDtypeStruct((w_kvt.shape[0], m), F32),
            jax.ShapeDtypeStruct((m, w_q.shape[1]), BF16),
        ],
        compiler_params=pltpu.CompilerParams(vmem_limit_bytes=V7X_VMEM_LIMIT_BYTES),
        name="swa_sample_proj",
    )(x, g_kv, w_kvt, g_q, w_q)


def _swa_sample_attend_items(q_ref, kvt_new_ref, kt_buf_ref, vt_buf_ref, bias_ref, sink_ref,
                             kt_out_ref, vt_out_ref, a_ref):
    i = pl.program_id(0)
    tb, kvd, n_buf = kt_buf_ref.shape
    hd = B_HEAD_DIM
    n_heads = q_ref.shape[0] // tb
    group = n_heads // B_KV_HEADS
    m_tok = kvt_new_ref.shape[1]
    tok_lane = lax.broadcasted_iota(jnp.int32, (2 * kvd, m_tok), 1)
    key_lane = lax.broadcasted_iota(jnp.int32, (kvd, n_buf), 1)
    for tk in range(tb):
        new_col = jnp.sum(jnp.where(tok_lane == i * tb + tk, kvt_new_ref[...], 0.0),
                          axis=1, keepdims=True)
        for buf_ref, out_ref, rows in ((kt_buf_ref, kt_out_ref, slice(0, kvd)),
                                       (vt_buf_ref, vt_out_ref, slice(kvd, 2 * kvd))):
            slid = pltpu.roll(buf_ref[tk], n_buf - 1, 1)
            out_ref[tk] = jnp.where(key_lane == n_buf - 1, new_col[rows], slid)
        yield

    e_row = lax.broadcasted_iota(jnp.int32, (hd, kvd), 0)
    e_col = lax.broadcasted_iota(jnp.int32, (hd, kvd), 1)
    spread = jnp.where(e_col % hd == e_row, 1.0, 0.0).astype(BF16)
    head = lax.broadcasted_iota(jnp.int32, (tb, n_heads, kvd), 1)
    slot = lax.broadcasted_iota(jnp.int32, (tb, n_heads, kvd), 2)
    own_slot = head // group == slot // hd
    qx = _dot(q_ref[...], spread).reshape(tb, n_heads, kvd)
    qx = jnp.where(own_slot, qx, 0.0).astype(BF16)
    yield

    kt = kt_out_ref[...].astype(BF16)
    vt = vt_out_ref[...].astype(BF16)
    s = jnp.einsum("bhc,bcr->bhr", qx, kt, preferred_element_type=F32)
    s = s + bias_ref[...][None]
    sink = sink_ref[...][None] * LOG2E
    m = jnp.maximum(jnp.max(s, axis=-1, keepdims=True), sink)
    p = jnp.exp2(s - m)
    denom = jnp.sum(p, axis=-1, keepdims=True) + jnp.exp2(sink - m)
    yield
    r = jnp.einsum("bhr,bcr->bhc", p.astype(BF16), vt, preferred_element_type=F32) / denom
    r = jnp.where(own_slot, r, 0.0).astype(BF16).reshape(tb * n_heads, kvd)
    a_ref[...] = _dot_nt(r, spread).astype(BF16)
    yield


def _swa_sample_attend_rider(q_rows, kvt_new, kt_buf, vt_buf, bias_row, sink_col, n_steps):
    m, kvd, n_buf = kt_buf.shape
    n_heads = q_rows.shape[0] // m
    assert m % n_steps == 0, "the sample group must split evenly over the host call's steps"
    tb = m // n_steps
    cache = pl.BlockSpec((tb, kvd, n_buf), lambda i: (i, 0, 0))
    head_rows = pl.BlockSpec((tb * n_heads, B_HEAD_DIM), lambda i: (i, 0))
    return _Rider(
        name="swa_sample_attend",
        items=_swa_sample_attend_items,
        n_items=tb + 3,
        args=(q_rows, kvt_new, kt_buf, vt_buf, bias_row, sink_col),
        in_specs=(head_rows, _resident(kvt_new.shape), cache, cache,
                  _resident(bias_row.shape), _resident(sink_col.shape)),
        out_shape=(jax.ShapeDtypeStruct((m, kvd, n_buf), F32),
                   jax.ShapeDtypeStruct((m, kvd, n_buf), F32),
                   jax.ShapeDtypeStruct((m * n_heads, B_HEAD_DIM), BF16)),
        out_specs=(cache, cache, head_rows),
    )


def _swa_sample_out_kernel(x_ref, a_ref, w_out_ref, y_ref):
    y_ref[...] = x_ref[...] + _dot(a_ref[...], w_out_ref[...])


def _swa_sample_out(x, a, w_out):
    vm = pl.BlockSpec(memory_space=pltpu.VMEM)
    return pl.pallas_call(
        _swa_sample_out_kernel,
        in_specs=[vm] * 3,
        out_specs=vm,
        out_shape=jax.ShapeDtypeStruct(x.shape, F32),
        compiler_params=pltpu.CompilerParams(vmem_limit_bytes=V7X_VMEM_LIMIT_BYTES),
        name="swa_sample_out",
    )(x, a, w_out)


def _cache_keys_minor(cache):
    b, n, kvh, hd = cache.shape
    return jnp.transpose(cache, (0, 2, 3, 1)).reshape(b, kvh * hd, n)


def _cache_keys_major(cache_t):
    b, kvd, n = cache_t.shape
    return jnp.transpose(cache_t.reshape(b, B_KV_HEADS, kvd // B_KV_HEADS, n), (0, 3, 1, 2))


def kernel(x_prompt, x_sample, state_hgrn, cache_k_win, cache_v_win, w_a_in, a_lb, a_gnorm, w_a_out,
           g_mix, g_mlp, g_kv, w_kv, w_b_q, b_sink, w_b_out, rel_bias, w_up, w_down, g_final):
    bsz, seq, d = x_prompt.shape
    n_dec = x_sample.shape[0]
    assert x_sample.shape[1] == 1, "the sample group decodes one token per sequence"
    n_a = w_a_in.shape[0]
    n_b = w_b_q.shape[0]
    assert n_a == 1 and n_b == 1, "depth-2 trunk: one HGRN2 layer, then one attention layer"
    n_buf = cache_k_win.shape[1]
    assert n_buf == WINDOW and seq % WINDOW == 0
    kvd = B_KV_HEADS * B_HEAD_DIM
    n_heads = d // B_HEAD_DIM

    bf = lambda w: w.astype(BF16)
    row = lambda g: g.reshape(1, -1)
    w_in, w_ao = bf(w_a_in[0]), bf(w_a_out[0])
    band_bias = _band_bias(rel_bias)
    sink = b_sink[0].reshape(1, n_heads)

    mlp_tile = min(512, bsz * seq)
    mlp_steps = bsz * seq // mlp_tile
    xs = x_sample.reshape(n_dec, d)

    ft, q, v, g = _hgrn_sample_gates(xs, w_in, a_lb, row(g_mix[0]), layer=0)
    later = (w_up, w_down, w_kv, w_b_q[0], w_b_out[0])
    x, s_prompt, cast = _hgrn_prompt(
        x_prompt, w_in, a_lb, row(g_mix[0]), row(a_gnorm[0]), w_ao, layer=0, tile=512,
        make_rider=functools.partial(_bf16_cast_rider, later))
    w_up_b, w_down_b, w_kvb, w_q, w_bo = (c.reshape(w.shape) for c, w in zip(cast, later))
    x, (s_sample, o) = _mlp(
        x.reshape(bsz * seq, d), row(g_mlp[0]), w_up_b, w_down_b, 0, None, tile=mlp_tile,
        rider=_hgrn_sample_state_rider(ft, q, v, state_hgrn[0], mlp_steps))
    xs = _hgrn_sample_out(xs, o.reshape(n_dec, d), g, row(a_gnorm[0]), w_ao)
    xs, _ = _mlp(xs, row(g_mlp[0]), w_up_b, w_down_b, 0, None, tile=n_dec)

    kvt_new, q = _swa_sample_proj(xs, row(g_kv), w_kvb.T, row(g_mix[1]), w_q)
    x, kt_prompt, vt_prompt = _swa_prompt(x.reshape(bsz, seq, d), row(g_kv), w_kvb,
                                          row(g_mix[1]), w_q.T, sink, band_bias, w_bo, tile=512)
    bias_row = band_bias[:, 1:n_buf + 1, 0]
    y_prompt, (kt_sample, vt_sample, a) = _mlp(
        x.reshape(bsz * seq, d), row(g_mlp[1]), w_up_b, w_down_b, 1, row(g_final),
        tile=mlp_tile,
        rider=_swa_sample_attend_rider(
            q.reshape(n_dec * n_heads, B_HEAD_DIM), kvt_new,
            _cache_keys_minor(cache_k_win), _cache_keys_minor(cache_v_win),
            bias_row, sink.reshape(n_heads, 1), mlp_steps))
    y_prompt = y_prompt.reshape(bsz, seq, d)
    xs = _swa_sample_out(xs, a.reshape(n_dec, d), w_bo)
    y_sample, _ = _mlp(xs, row(g_mlp[1]), w_up_b, w_down_b, 1, row(g_final), tile=n_dec)
    y_sample = y_sample.reshape(n_dec, 1, d)

    return (y_prompt, y_sample, s_prompt[None], s_sample[None],
            _cache_keys_major(kt_prompt), _cache_keys_major(vt_prompt),
            _cache_keys_major(kt_sample), _cache_keys_major(vt_sample))
```

```python
import functools
import math
from typing import Callable, NamedTuple

import jax
import jax.numpy as jnp
import numpy as np
from jax import lax
from jax.experimental import pallas as pl
from jax.experimental.pallas import tpu as pltpu

F32 = jnp.float32
BF16 = jnp.bfloat16

EPS = 1e-6
NEG = -1e30
LOG2E = math.log2(math.e)
A_HEADS = 8
A_CHUNK = 64
B_HEAD_DIM = 64
B_KV_HEADS = 4
WINDOW = 128
N_BUCKETS = 32
MAX_DISTANCE = 128

V7X_VMEM_LIMIT_BYTES = 56 * 1024 * 1024
V7X_MXU_DIM = 256


def _cparams(*semantics):
    return pltpu.CompilerParams(dimension_semantics=semantics,
                                vmem_limit_bytes=V7X_VMEM_LIMIT_BYTES)


def _resident(shape):
    nd = len(shape)
    return pl.BlockSpec(shape, lambda *_: (0,) * nd, pipeline_mode=pl.Buffered(1))


def _resident_layer(shape, layer):
    nd = len(shape)
    return pl.BlockSpec((None,) + tuple(shape[1:]), lambda *_: (layer,) + (0,) * (nd - 1),
                        pipeline_mode=pl.Buffered(1))


def _rms_scale(x):
    return x * lax.rsqrt(jnp.mean(x * x, axis=-1, keepdims=True) + EPS)


def _sigmoid(x):
    return 1.0 / (1.0 + jnp.exp(-x))


def _dot(a, b):
    return jnp.dot(a, b, preferred_element_type=F32)


def _dot_nt(a, b):
    return lax.dot_general(a, b, (((1,), (1,)), ((), ())), preferred_element_type=F32)


def _dot_tn(a, b):
    return lax.dot_general(a, b, (((0,), (0,)), ((), ())), preferred_element_type=F32)


def _lower_bound(a_lb, layer):
    m = jnp.max(a_lb, axis=0, keepdims=True)
    e = jnp.exp(a_lb - m)
    return jnp.sum(e[: layer + 1], axis=0, keepdims=True) / jnp.sum(e, axis=0, keepdims=True)


def _forget_gate(f_raw, lb):
    return lb + (1.0 - lb) * _sigmoid(f_raw)


def _hgrn_prompt_kernel(layer, rider, x_ref, w_in_ref, a_lb_ref, g_mix_ref, g_norm_ref,
                        w_out_ref, *rest):
    n_in = len(rider.args) if rider else 0
    n_out = len(rider.out_shape) if rider else 0
    y_ref, s_out_ref = rest[n_in:n_in + 2]
    (st_ref, proj_ref, lf_ref, kk_ref, b_ref, qi_ref, ki_ref, qn_ref, ks_ref, vb_ref,
     dec_ref, att_ref, a_ref) = rest[n_in + 2 + n_out:]
    if rider:
        for _ in rider.items(*rest[:n_in], *rest[n_in + 2:n_in + 2 + n_out]):
            pass
    t = pl.program_id(1)
    tile, d = x_ref.shape[1], x_ref.shape[2]
    dk = d // A_HEADS
    c = A_CHUNK
    n_chunks = tile // c

    @pl.when(t == 0)
    def _():
        st_ref[...] = jnp.zeros_like(st_ref)
        dec_ref[...] = jnp.zeros_like(dec_ref)

    x = x_ref[0]
    h = (_rms_scale(x) * g_mix_ref[...]).astype(BF16)
    lb = _lower_bound(a_lb_ref[...], layer)
    g_norm = g_norm_ref[...]

    span = min(tile, V7X_MXU_DIM)
    row = lax.broadcasted_iota(jnp.int32, (span, span), 0)
    col = lax.broadcasted_iota(jnp.int32, (span, span), 1)
    tri = jnp.where((row >= col) & (row // c == col // c), 1.0, 0.0).astype(BF16)
    crow = lax.broadcasted_iota(jnp.int32, (c, c), 0)
    ccol = lax.broadcasted_iota(jnp.int32, (c, c), 1)
    causal = crow >= ccol

    proj_ref[...] = _dot(h, w_in_ref[...])

    for ci in range(n_chunks):
        rows = slice(ci * c, (ci + 1) * c)
        forget = _forget_gate(proj_ref[rows, d:2 * d], lb)
        kk_ref[rows, :] = 1.0 - forget
        logf = jnp.log(forget)
        hi = logf.astype(BF16)
        r1 = logf - hi.astype(F32)
        mid = r1.astype(BF16)
        lf_ref[0, rows, :] = hi
        lf_ref[1, rows, :] = mid
        lf_ref[2, rows, :] = (r1 - mid.astype(F32)).astype(BF16)

    for r0 in range(0, tile, span):
        rows = slice(r0, r0 + span)
        b_ref[rows, :] = (_dot(tri, lf_ref[0, rows, :]) + _dot(tri, lf_ref[1, rows, :])
                          + _dot(tri, lf_ref[2, rows, :]))

    for ci in range(n_chunks):
        rows = slice(ci * c, (ci + 1) * c)
        b = b_ref[rows, :]
        b_mid = b_ref[ci * c + c // 2 - 1:ci * c + c // 2, :]
        b_last = b_ref[ci * c + c - 1:ci * c + c, :]
        q = proj_ref[rows, 0:d]
        k = kk_ref[rows, :]
        qi_ref[rows, :] = (q * jnp.exp(b - b_mid)).astype(BF16)
        ki_ref[rows, :] = (k * jnp.exp(b_mid - b)).astype(BF16)
        qn_ref[rows, :] = (q * jnp.exp(b)).astype(BF16)
        ks_ref[rows, :] = (k * jnp.exp(b_last - b)).astype(BF16)
        vb_ref[rows, :] = proj_ref[rows, 2 * d:3 * d].astype(BF16)
        decay = jnp.exp(b_last)
        for hd in range(A_HEADS):
            dec_ref[ci * A_HEADS + hd:ci * A_HEADS + hd + 1, :] = decay[:, hd * dk:(hd + 1) * dk]

    for ci in range(n_chunks):
        rows = slice(ci * c, (ci + 1) * c)
        for hd in range(A_HEADS):
            ls = slice(hd * dk, (hd + 1) * dk)
            att = jnp.where(causal, _dot_nt(qi_ref[rows, ls], ki_ref[rows, ls]), 0.0)
            att_ref[ci * A_HEADS + hd] = att.astype(BF16)

    dec_t = dec_ref[...].T
    for ci in range(n_chunks):
        rows = slice(ci * c, (ci + 1) * c)
        for hd in range(A_HEADS):
            ls = slice(hd * dk, (hd + 1) * dk)
            j = ci * A_HEADS + hd
            st = st_ref[hd]
            lhs = jnp.concatenate([qn_ref[rows, ls], att_ref[j]], axis=1)
            rhs = jnp.concatenate([st.astype(BF16), vb_ref[rows, ls]], axis=0)
            o = _dot(lhs, rhs)
            st_ref[hd] = st * dec_t[:, j:j + 1] + _dot_tn(ks_ref[rows, ls], vb_ref[rows, ls])
            g = proj_ref[rows, 3 * d + hd * dk:3 * d + (hd + 1) * dk]
            a_ref[rows, ls] = (_rms_scale(o) * g_norm * (g * _sigmoid(g))).astype(BF16)

    y_ref[0] = x + _dot(a_ref[...], w_out_ref[...])

    @pl.when(t == pl.num_programs(1) - 1)
    def _():
        s_out_ref[0] = st_ref[...]


def _cast_items(*refs):
    n = len(refs) // 2
    for src_ref, dst_ref in zip(refs[:n], refs[n:]):
        dst_ref[...] = src_ref[...].astype(dst_ref.dtype)
        yield


def _bf16_cast_rider(arrays, n_outer, n_inner):
    n_steps = n_outer * n_inner
    flat = tuple(a.reshape(-1, a.shape[-1]) for a in arrays)
    specs = []
    for a in flat:
        rows = a.shape[0] // n_steps
        assert rows * n_steps == a.shape[0] and rows % 16 == 0, a.shape
        specs.append(pl.BlockSpec((rows, a.shape[1]), lambda b, t: (b * n_inner + t, 0)))
    return _Rider(
        name="bf16_cast",
        items=_cast_items,
        n_items=len(flat),
        args=flat,
        in_specs=tuple(specs),
        out_shape=tuple(jax.ShapeDtypeStruct(a.shape, BF16) for a in flat),
        out_specs=tuple(specs),
    )


def _hgrn_prompt(x, w_in, a_lb, g_mix, g_norm, w_out, layer, tile, make_rider=None):
    bsz, seq, d = x.shape
    dk = d // A_HEADS
    tile = min(tile, seq)
    rider = make_rider(bsz, seq // tile) if make_rider else None
    outs = pl.pallas_call(
        functools.partial(_hgrn_prompt_kernel, layer, rider),
        grid=(bsz, seq // tile),
        in_specs=[
            pl.BlockSpec((1, tile, d), lambda b, t: (b, t, 0)),
            _resident(w_in.shape),
            _resident(a_lb.shape),
            _resident(g_mix.shape),
            _resident(g_norm.shape),
            _resident(w_out.shape),
        ] + list(rider.in_specs if rider else ()),
        out_specs=[
            pl.BlockSpec((1, tile, d), lambda b, t: (b, t, 0)),
            pl.BlockSpec((1, A_HEADS, dk, dk), lambda b, t: (b, 0, 0, 0)),
        ] + list(rider.out_specs if rider else ()),
        out_shape=[
            jax.ShapeDtypeStruct((bsz, seq, d), F32),
            jax.ShapeDtypeStruct((bsz, A_HEADS, dk, dk), F32),
        ] + list(rider.out_shape if rider else ()),
        scratch_shapes=[
            pltpu.VMEM((A_HEADS, dk, dk), F32),
            pltpu.VMEM((tile, 4 * d), F32),
            pltpu.VMEM((3, tile, d), BF16),
            pltpu.VMEM((tile, d), F32),
            pltpu.VMEM((tile, d), F32),
            pltpu.VMEM((tile, d), BF16),
            pltpu.VMEM((tile, d), BF16),
            pltpu.VMEM((tile, d), BF16),
            pltpu.VMEM((tile, d), BF16),
            pltpu.VMEM((tile, d), BF16),
            pltpu.VMEM((dk, dk), F32),
            pltpu.VMEM((tile // A_CHUNK * A_HEADS, A_CHUNK, A_CHUNK), BF16),
            pltpu.VMEM((tile, d), BF16),
        ],
        compiler_params=_cparams("arbitrary", "arbitrary"),
        name="hgrn_prompt_with_" + rider.name if rider else "hgrn_prompt",
    )(x, w_in, a_lb, g_mix, g_norm, w_out, *(rider.args if rider else ()))
    return outs[0], outs[1], outs[2:]


class _Rider(NamedTuple):
    name: str
    items: Callable
    n_items: int
    args: tuple
    in_specs: tuple
    out_shape: tuple
    out_specs: tuple


def _mlp_kernel(ff_chunk, has_final, rider, x_ref, g_ref, w_up_ref, w_down_ref, *rest):
    rest = list(rest)
    g_final_ref = rest.pop(0) if has_final else None
    n_in = len(rider.args) if rider else 0
    y_ref = rest[n_in]
    side = rider.items(*rest[:n_in], *rest[n_in + 1:]) if rider else iter(())
    n_chunks = w_up_ref.shape[1] // ff_chunk
    x = x_ref[...]
    h = (_rms_scale(x) * g_ref[...]).astype(BF16)
    acc = x
    for j in range(n_chunks):
        cols = slice(j * ff_chunk, (j + 1) * ff_chunk)
        u = jnp.maximum(_dot(h, w_up_ref[:, cols]), 0.0)
        acc = acc + _dot((u * u).astype(BF16), w_down_ref[cols, :])
        for _ in range(rider.n_items // n_chunks if rider else 0):
            next(side, None)
    for _ in side:
        pass
    if has_final:
        acc = _rms_scale(acc) * g_final_ref[...]
    y_ref[...] = acc


def _mlp(x, g, w_up, w_down, layer, g_final, tile, rider=None, ff_chunk=1024):
    m, d = x.shape
    tile = min(tile, m)
    has_final = g_final is not None
    in_specs = [
        pl.BlockSpec((tile, d), lambda i: (i, 0)),
        _resident(g.shape),
        _resident_layer(w_up.shape, layer),
        _resident_layer(w_down.shape, layer),
    ]
    args = [x, g, w_up, w_down]
    if has_final:
        in_specs.append(_resident(g_final.shape))
        args.append(g_final)
    out_specs = [pl.BlockSpec((tile, d), lambda i: (i, 0))]
    out_shape = [jax.ShapeDtypeStruct((m, d), F32)]
    name = "mlp_final" if has_final else "mlp"
    if rider:
        in_specs += list(rider.in_specs)
        args += list(rider.args)
        out_specs += list(rider.out_specs)
        out_shape += list(rider.out_shape)
        name += "_with_" + rider.name
    outs = pl.pallas_call(
        functools.partial(_mlp_kernel, ff_chunk, has_final, rider),
        grid=(m // tile,),
        in_specs=in_specs,
        out_specs=out_specs,
        out_shape=out_shape,
        compiler_params=_cparams("arbitrary"),
        name=name,
    )(*args)
    return outs[0], outs[1:]


def _t5_bucket_table(w):
    dist = np.arange(w)[None, :] + w - np.arange(2 * w)[:, None]
    n = np.maximum(dist, 0)
    max_exact = N_BUCKETS // 2
    nf = np.maximum(n, 1).astype(np.float32)
    large = max_exact + (np.log(nf / np.float32(max_exact))
                         / np.float32(math.log(MAX_DISTANCE / max_exact))
                         * np.float32(N_BUCKETS - max_exact)).astype(np.int32)
    large = np.minimum(large, N_BUCKETS - 1)
    return np.where(n < max_exact, n, large).astype(np.int32)


def _band_bias_heads(rel_bias_ref, bucket_ref, out_ref, first_head):
    n, _, w = out_ref.shape
    kj = lax.broadcasted_iota(jnp.int32, (2 * w, w), 0)
    qi = lax.broadcasted_iota(jnp.int32, (2 * w, w), 1)
    dist = qi + w - kj
    bucket = bucket_ref[...]
    valid = (dist >= 0) & (dist < WINDOW)
    for hl in range(n):
        bias = jnp.zeros((2 * w, w), F32)
        for bk in range(N_BUCKETS):
            bias = jnp.where(bucket == bk, rel_bias_ref[bk, first_head + hl], bias)
        out_ref[hl] = jnp.where(valid, bias * LOG2E, NEG)


def _col_max(x):
    return jnp.max(x, axis=0, keepdims=True)


def _col_sum(x):
    return jnp.sum(x, axis=0, keepdims=True)


def _swa_prompt_kernel(x_ref, g_kv_ref, w_kv_ref, g_q_ref, w_qt_ref, sink_ref, bias_ref, w_out_ref,
                       y_ref, k_out_ref, v_out_ref, k_ref, vt_ref, qt_ref, at_ref, s_ref, p_ref):
    t = pl.program_id(1)
    tile, d = x_ref.shape[1], x_ref.shape[2]
    w = WINDOW
    hd = B_HEAD_DIM
    n_heads = d // hd
    group = n_heads // B_KV_HEADS
    kvd = B_KV_HEADS * hd
    scale = 1.0 / math.sqrt(hd)

    @pl.when(t == 0)
    def _():
        k_ref[:, 0:w, :] = jnp.zeros((B_KV_HEADS, w, hd), BF16)
        vt_ref[:, 0:w] = jnp.zeros((kvd, w), BF16)

    @pl.when(t > 0)
    def _():
        k_ref[:, 0:w, :] = k_ref[:, tile:tile + w, :]
        vt_ref[:, 0:w] = vt_ref[:, tile:tile + w]

    x = x_ref[0]
    xn = _rms_scale(x)
    kv = _dot((xn * g_kv_ref[...]).astype(BF16), w_kv_ref[...])
    for kh in range(B_KV_HEADS):
        k_ref[kh, w:w + tile, :] = kv[:, kh * hd:(kh + 1) * hd].astype(BF16)
    vt_ref[:, w:w + tile] = kv[:, kvd:2 * kvd].T.astype(BF16)
    h_q = (xn * g_q_ref[...]).astype(BF16)
    qt_ref[...] = (_dot_nt(w_qt_ref[...], h_q) * (scale * LOG2E)).astype(BF16)

    @pl.when(t == pl.num_programs(1) - 1)
    def _():
        k_out_ref[0] = kv[tile - w:, 0:kvd].T
        v_out_ref[0] = kv[tile - w:, kvd:2 * kvd].T

    first = t == 0
    sc = 64
    rc = 16
    units = [(blk, kh) for blk in range(tile // w) for kh in range(B_KV_HEADS)]

    def scores(u):
        blk, kh = units[u]
        cols = slice(blk * w, (blk + 1) * w)
        heads = range(kh * group, (kh + 1) * group)
        q4 = jnp.concatenate([qt_ref[h * hd:(h + 1) * hd, cols] for h in heads], axis=1)
        m = None
        for r in range(2 * w // sc):
            s = _dot(k_ref[kh, blk * w + r * sc:blk * w + (r + 1) * sc, :], q4)
            s = s + jnp.concatenate([bias_ref[h, r * sc:(r + 1) * sc, :] for h in heads], axis=1)
            if blk == 0 and r * sc < w:
                s = jnp.where(first, NEG, s)
            s_ref[u % 2, r * sc:(r + 1) * sc, :] = s
            for i in range(sc // 8):
                part = s[i * 8:(i + 1) * 8, :]
                m = part if m is None else jnp.maximum(m, part)
        return m

    def softmax(u, m8):
        blk, kh = units[u]
        heads = range(kh * group, (kh + 1) * group)
        sink = jnp.concatenate([jnp.full((1, w), sink_ref[0, h] * LOG2E, F32) for h in heads],
                               axis=1)
        m = jnp.maximum(_col_max(m8), sink)
        for r in range(2 * w // rc):
            p = jnp.exp2(s_ref[u % 2, r * rc:(r + 1) * rc, :] - m)
            p_ref[u % 2, r * rc:(r + 1) * rc, :] = p.astype(BF16)
        return jnp.exp2(sink - m)

    ones_rows = jnp.ones((rc, 2 * w), BF16)

    def weighted_values(u, sink_term):
        blk, kh = units[u]
        cols = slice(blk * w, (blk + 1) * w)
        keys = slice(blk * w, (blk + 2) * w)
        vt1 = jnp.concatenate([vt_ref[kh * hd:(kh + 1) * hd, keys], ones_rows], axis=0)
        o = _dot(vt1, p_ref[u % 2])
        o = o[0:hd] / (o[hd:hd + 1] + sink_term)
        for g in range(group):
            h = kh * group + g
            at_ref[h * hd:(h + 1) * hd, cols] = o[:, g * w:(g + 1) * w].astype(BF16)

    m_next = scores(0)
    denom_prev = None
    for u in range(len(units)):
        m_cur = m_next
        if u + 1 < len(units):
            m_next = scores(u + 1)
        denom = softmax(u, m_cur)
        if u > 0:
            weighted_values(u - 1, denom_prev)
        denom_prev = denom
    weighted_values(len(units) - 1, denom_prev)

    y_ref[0] = x + _dot_tn(at_ref[...], w_out_ref[...])


def _swa_prompt(x, g_kv, w_kv, g_q, w_qt, sink, band_bias, w_out, tile):
    bsz, seq, d = x.shape
    tile = min(tile, seq)
    kvd = w_kv.shape[1] // 2
    w = WINDOW
    return pl.pallas_call(
        _swa_prompt_kernel,
        grid=(bsz, seq // tile),
        in_specs=[
            pl.BlockSpec((1, tile, d), lambda b, t: (b, t, 0)),
            _resident(g_kv.shape),
            _resident(w_kv.shape),
            _resident(g_q.shape),
            _resident(w_qt.shape),
            pl.BlockSpec(memory_space=pltpu.SMEM),
            _resident(band_bias.shape),
            _resident(w_out.shape),
        ],
        out_specs=[
            pl.BlockSpec((1, tile, d), lambda b, t: (b, t, 0)),
            pl.BlockSpec((1, kvd, w), lambda b, t: (b, 0, 0)),
            pl.BlockSpec((1, kvd, w), lambda b, t: (b, 0, 0)),
        ],
        out_shape=[
            jax.ShapeDtypeStruct((bsz, seq, d), F32),
            jax.ShapeDtypeStruct((bsz, kvd, w), F32),
            jax.ShapeDtypeStruct((bsz, kvd, w), F32),
        ],
        scratch_shapes=[
            pltpu.VMEM((B_KV_HEADS, tile + w, B_HEAD_DIM), BF16),
            pltpu.VMEM((kvd, tile + w), BF16),
            pltpu.VMEM((d, tile), BF16),
            pltpu.VMEM((d, tile), BF16),
            pltpu.VMEM((2, 2 * w, d // B_KV_HEADS * w // B_HEAD_DIM), F32),
            pltpu.VMEM((2, 2 * w, d // B_KV_HEADS * w // B_HEAD_DIM), BF16),
        ],
        compiler_params=_cparams("arbitrary", "arbitrary"),
        name="swa_prompt",
    )(x, g_kv, w_kv, g_q, w_qt, sink, band_bias, w_out)


def _prologue_kernel(layer, x_ref, w_in_ref, w_out_ref, a_lb_ref, g_mix_ref, rel_bias_ref, bucket_ref,
                     w_in_bf_ref, w_out_bf_ref, ft_ref, q_ref, v_ref, g_ref, bias_ref):
    p = pl.program_id(0)
    d = x_ref.shape[1]
    dk = d // A_HEADS
    w_in_bf_ref[...] = w_in_ref[...].astype(BF16)
    w_out_bf_ref[...] = w_out_ref[...].astype(BF16)
    h = (_rms_scale(x_ref[...]) * g_mix_ref[...]).astype(BF16)
    part = _dot(h, w_in_bf_ref[...])

    @pl.when(p == 0)
    def _():
        q_ref[...] = part

    @pl.when(p == 1)
    def _():
        forget = _forget_gate(part, _lower_bound(a_lb_ref[...], layer))
        for hd in range(A_HEADS):
            ft_ref[hd] = forget[:, hd * dk:(hd + 1) * dk].T

    @pl.when(p == 2)
    def _():
        v_ref[...] = part

    @pl.when(p == 3)
    def _():
        g_ref[...] = part

    _band_bias_heads(rel_bias_ref, bucket_ref, bias_ref, p * bias_ref.shape[0])


def _prologue(x, w_in, w_out, a_lb, g_mix, rel_bias, layer):
    m, d = x.shape
    dk = d // A_HEADS
    parts = w_in.shape[1] // d
    n_heads = rel_bias.shape[1]
    assert parts == 4 and n_heads % parts == 0 and d % parts == 0
    w = WINDOW
    rows = pl.BlockSpec((d // parts, d), lambda p: (p, 0))
    cols = pl.BlockSpec((d, d), lambda p: (0, p))
    tokens = pl.BlockSpec((m, d), lambda p: (0, 0))
    bucket = jnp.asarray(_t5_bucket_table(w))
    return pl.pallas_call(
        functools.partial(_prologue_kernel, layer),
        grid=(parts,),
        in_specs=[
            tokens, cols, rows, _resident(a_lb.shape), _resident(g_mix.shape),
            pl.BlockSpec(memory_space=pltpu.SMEM),
            pl.BlockSpec((2 * w, w), lambda p: (0, 0)),
        ],
        out_specs=[
            cols, rows,
            pl.BlockSpec((A_HEADS, dk, m), lambda p: (0, 0, 0)),
            tokens, tokens, tokens,
            pl.BlockSpec((n_heads // parts, 2 * w, w), lambda p: (p, 0, 0)),
        ],
        out_shape=[
            jax.ShapeDtypeStruct(w_in.shape, BF16),
            jax.ShapeDtypeStruct(w_out.shape, BF16),
            jax.ShapeDtypeStruct((A_HEADS, dk, m), F32),
            jax.ShapeDtypeStruct((m, d), F32),
            jax.ShapeDtypeStruct((m, d), F32),
            jax.ShapeDtypeStruct((m, d), F32),
            jax.ShapeDtypeStruct((n_heads, 2 * w, w), F32),
        ],
        compiler_params=_cparams("arbitrary"),
        name="prologue",
    )(x, w_in, w_out, a_lb, g_mix, rel_bias, bucket)


def _hgrn_sample_state_items(ft_ref, q_ref, v_ref, s0_ref, s_ref, o_ref):
    i = pl.program_id(0)
    tb = s0_ref.shape[0]
    dk = s0_ref.shape[2]
    m = ft_ref.shape[2]
    lane = lax.broadcasted_iota(jnp.int32, (dk, m), 1)
    for tk in range(tb):
        pick = lane == i * tb + tk
        for hd in range(A_HEADS):
            ls = slice(hd * dk, (hd + 1) * dk)
            f = jnp.sum(jnp.where(pick, ft_ref[hd], 0.0), axis=1, keepdims=True)
            v = v_ref[0, tk:tk + 1, ls]
            s_new = v + f * (s0_ref[tk, hd] - v)
            s_ref[tk, hd] = s_new
            o = _dot(q_ref[0, :, ls].astype(BF16), s_new.astype(BF16))
            o_ref[0, tk:tk + 1, ls] = o[tk:tk + 1]
            yield


def _hgrn_sample_state_rider(ft, q, v, s0, n_steps):
    m, d = v.shape
    dk = d // A_HEADS
    assert m % n_steps == 0, "the sample group must split evenly over the host call's steps"
    tb = m // n_steps
    rows3 = pl.BlockSpec((1, tb, d), lambda i: (i, 0, 0))
    state = pl.BlockSpec((tb, A_HEADS, dk, dk), lambda i: (i, 0, 0, 0))
    return _Rider(
        name="hgrn_sample_state",
        items=_hgrn_sample_state_items,
        n_items=tb * A_HEADS,
        args=(ft, q.reshape(n_steps, tb, d), v.reshape(n_steps, tb, d), s0),
        in_specs=(_resident(ft.shape), rows3, rows3, state),
        out_shape=(jax.ShapeDtypeStruct(s0.shape, F32),
                   jax.ShapeDtypeStruct((n_steps, tb, d), F32)),
        out_specs=(state, rows3),
    )


def _hgrn_sample_out_kernel(x_ref, o_ref, g_ref, g_norm_ref, w_out_ref, y_ref):
    d = x_ref.shape[1]
    dk = d // A_HEADS
    g = g_ref[...]
    gate = g * _sigmoid(g)
    parts = []
    for hd in range(A_HEADS):
        ls = slice(hd * dk, (hd + 1) * dk)
        parts.append(_rms_scale(o_ref[:, ls]) * g_norm_ref[...] * gate[:, ls])
    a = jnp.concatenate(parts, axis=1).astype(BF16)
    y_ref[...] = x_ref[...] + _dot(a, w_out_ref[...])


def _hgrn_sample_out(x, o, g, g_norm, w_out):
    vm = pl.BlockSpec(memory_space=pltpu.VMEM)
    return pl.pallas_call(
        _hgrn_sample_out_kernel,
        in_specs=[vm] * 5,
        out_specs=vm,
        out_shape=jax.ShapeDtypeStruct(x.shape, F32),
        compiler_params=pltpu.CompilerParams(vmem_limit_bytes=V7X_VMEM_LIMIT_BYTES),
        name="hgrn_sample_out",
    )(x, o, g, g_norm, w_out)


def _swa_sample_proj_kernel(x_ref, g_kv_ref, w_kvt_ref, g_q_ref, w_q_ref, kvt_ref, q_ref):
    scale = 1.0 / math.sqrt(B_HEAD_DIM)
    xn = _rms_scale(x_ref[...])
    kvt_ref[...] = _dot_nt(w_kvt_ref[...], (xn * g_kv_ref[...]).astype(BF16))
    q_ref[...] = (_dot((xn * g_q_ref[...]).astype(BF16), w_q_ref[...])
                  * (scale * LOG2E)).astype(BF16)


def _swa_sample_proj(x, g_kv, w_kvt, g_q, w_q):
    m = x.shape[0]
    vm = pl.BlockSpec(memory_space=pltpu.VMEM)
    return pl.pallas_call(
        _swa_sample_proj_kernel,
        in_specs=[vm] * 5,
        out_specs=[vm, vm],
        out_shape=[
            jax.ShapeDtypeStruct((w_kvt.shape[0], m), F32),
            jax.ShapeDtypeStruct((m, w_q.shape[1]), BF16),
        ],
        compiler_params=pltpu.CompilerParams(vmem_limit_bytes=V7X_VMEM_LIMIT_BYTES),
        name="swa_sample_proj",
    )(x, g_kv, w_kvt, g_q, w_q)


def _swa_sample_attend_items(q_ref, kvt_new_ref, kt_buf_ref, vt_buf_ref, bias_ref, sink_ref,
                             kt_out_ref, vt_out_ref, a_ref):
    i = pl.program_id(0)
    tb, kvd, n_buf = kt_buf_ref.shape
    hd = B_HEAD_DIM
    n_heads = q_ref.shape[0] // tb
    group = n_heads // B_KV_HEADS
    m_tok = kvt_new_ref.shape[1]
    tok_lane = lax.broadcasted_iota(jnp.int32, (2 * kvd, m_tok), 1)
    key_lane = lax.broadcasted_iota(jnp.int32, (kvd, n_buf), 1)
    for tk in range(tb):
        new_col = jnp.sum(jnp.where(tok_lane == i * tb + tk, kvt_new_ref[...], 0.0),
                          axis=1, keepdims=True)
        for buf_ref, out_ref, rows in ((kt_buf_ref, kt_out_ref, slice(0, kvd)),
                                       (vt_buf_ref, vt_out_ref, slice(kvd, 2 * kvd))):
            slid = pltpu.roll(buf_ref[tk], n_buf - 1, 1)
            out_ref[tk] = jnp.where(key_lane == n_buf - 1, new_col[rows], slid)
        yield

    e_row = lax.broadcasted_iota(jnp.int32, (hd, kvd), 0)
    e_col = lax.broadcasted_iota(jnp.int32, (hd, kvd), 1)
    spread = jnp.where(e_col % hd == e_row, 1.0, 0.0).astype(BF16)
    head = lax.broadcasted_iota(jnp.int32, (tb, n_heads, kvd), 1)
    slot = lax.broadcasted_iota(jnp.int32, (tb, n_heads, kvd), 2)
    own_slot = head // group == slot // hd
    qx = _dot(q_ref[...], spread).reshape(tb, n_heads, kvd)
    qx = jnp.where(own_slot, qx, 0.0).astype(BF16)
    yield

    kt = kt_out_ref[...].astype(BF16)
    vt = vt_out_ref[...].astype(BF16)
    s = jnp.einsum("bhc,bcr->bhr", qx, kt, preferred_element_type=F32)
    s = s + bias_ref[...][None]
    sink = sink_ref[...][None] * LOG2E
    m = jnp.maximum(jnp.max(s, axis=-1, keepdims=True), sink)
    p = jnp.exp2(s - m)
    denom = jnp.sum(p, axis=-1, keepdims=True) + jnp.exp2(sink - m)
    yield
    r = jnp.einsum("bhr,bcr->bhc", p.astype(BF16), vt, preferred_element_type=F32) / denom
    r = jnp.where(own_slot, r, 0.0).astype(BF16).reshape(tb * n_heads, kvd)
    a_ref[...] = _dot_nt(r, spread).astype(BF16)
    yield


def _swa_sample_attend_rider(q_rows, kvt_new, kt_buf, vt_buf, bias_row, sink_col, n_steps):
    m, kvd, n_buf = kt_buf.shape
    n_heads = q_rows.shape[0] // m
    assert m % n_steps == 0, "the sample group must split evenly over the host call's steps"
    tb = m // n_steps
    cache = pl.BlockSpec((tb, kvd, n_buf), lambda i: (i, 0, 0))
    head_rows = pl.BlockSpec((tb * n_heads, B_HEAD_DIM), lambda i: (i, 0))
    return _Rider(
        name="swa_sample_attend",
        items=_swa_sample_attend_items,
        n_items=tb + 3,
        args=(q_rows, kvt_new, kt_buf, vt_buf, bias_row, sink_col),
        in_specs=(head_rows, _resident(kvt_new.shape), cache, cache,
                  _resident(bias_row.shape), _resident(sink_col.shape)),
        out_shape=(jax.ShapeDtypeStruct((m, kvd, n_buf), F32),
                   jax.ShapeDtypeStruct((m, kvd, n_buf), F32),
                   jax.ShapeDtypeStruct((m * n_heads, B_HEAD_DIM), BF16)),
        out_specs=(cache, cache, head_rows),
    )


def _swa_sample_out_kernel(x_ref, a_ref, w_out_ref, y_ref):
    y_ref[...] = x_ref[...] + _dot(a_ref[...], w_out_ref[...])


def _swa_sample_out(x, a, w_out):
    vm = pl.BlockSpec(memory_space=pltpu.VMEM)
    return pl.pallas_call(
        _swa_sample_out_kernel,
        in_specs=[vm] * 3,
        out_specs=vm,
        out_shape=jax.ShapeDtypeStruct(x.shape, F32),
        compiler_params=pltpu.CompilerParams(vmem_limit_bytes=V7X_VMEM_LIMIT_BYTES),
        name="swa_sample_out",
    )(x, a, w_out)


def _cache_keys_minor(cache):
    b, n, kvh, hd = cache.shape
    return jnp.transpose(cache, (0, 2, 3, 1)).reshape(b, kvh * hd, n)


def _cache_keys_major(cache_t):
    b, kvd, n = cache_t.shape
    return jnp.transpose(cache_t.reshape(b, B_KV_HEADS, kvd // B_KV_HEADS, n), (0, 3, 1, 2))


def kernel(x_prompt, x_sample, state_hgrn, cache_k_win, cache_v_win, w_a_in, a_lb, a_gnorm, w_a_out,
           g_mix, g_mlp, g_kv, w_kv, w_b_q, b_sink, w_b_out, rel_bias, w_up, w_down, g_final):
    bsz, seq, d = x_prompt.shape
    n_dec = x_sample.shape[0]
    assert x_sample.shape[1] == 1, "the sample group decodes one token per sequence"
    n_a = w_a_in.shape[0]
    n_b = w_b_q.shape[0]
    assert n_a == 1 and n_b == 1, "depth-2 trunk: one HGRN2 layer, then one attention layer"
    n_buf = cache_k_win.shape[1]
    assert n_buf == WINDOW and seq % WINDOW == 0
    kvd = B_KV_HEADS * B_HEAD_DIM
    n_heads = d // B_HEAD_DIM

    row = lambda g: g.reshape(1, -1)
    sink = b_sink[0].reshape(1, n_heads)

    mlp_tile = min(512, bsz * seq)
    mlp_steps = bsz * seq // mlp_tile
    xs = x_sample.reshape(n_dec, d)

    w_in, w_ao, ft, q, v, g, band_bias = _prologue(
        xs, w_a_in[0], w_a_out[0], a_lb, row(g_mix[0]), rel_bias, layer=0)
    later = (w_up, w_down, w_kv, w_b_q[0], w_b_out[0])
    x, s_prompt, cast = _hgrn_prompt(
        x_prompt, w_in, a_lb, row(g_mix[0]), row(a_gnorm[0]), w_ao, layer=0, tile=512,
        make_rider=functools.partial(_bf16_cast_rider, later))
    w_up_b, w_down_b, w_kvb, w_q, w_bo = (c.reshape(w.shape) for c, w in zip(cast, later))
    x, (s_sample, o) = _mlp(
        x.reshape(bsz * seq, d), row(g_mlp[0]), w_up_b, w_down_b, 0, None, tile=mlp_tile,
        rider=_hgrn_sample_state_rider(ft, q, v, state_hgrn[0], mlp_steps))
    xs = _hgrn_sample_out(xs, o.reshape(n_dec, d), g, row(a_gnorm[0]), w_ao)
    xs, _ = _mlp(xs, row(g_mlp[0]), w_up_b, w_down_b, 0, None, tile=n_dec)

    kvt_new, q = _swa_sample_proj(xs, row(g_kv), w_kvb.T, row(g_mix[1]), w_q)
    x, kt_prompt, vt_prompt = _swa_prompt(x.reshape(bsz, seq, d), row(g_kv), w_kvb,
                                          row(g_mix[1]), w_q.T, sink, band_bias, w_bo, tile=512)
    bias_row = band_bias[:, 1:n_buf + 1, 0]
    y_prompt, (kt_sample, vt_sample, a) = _mlp(
        x.reshape(bsz * seq, d), row(g_mlp[1]), w_up_b, w_down_b, 1, row(g_final),
        tile=mlp_tile,
        rider=_swa_sample_attend_rider(
            q.reshape(n_dec * n_heads, B_HEAD_DIM), kvt_new,
            _cache_keys_minor(cache_k_win), _cache_keys_minor(cache_v_win),
            bias_row, sink.reshape(n_heads, 1), mlp_steps))
    y_prompt = y_prompt.reshape(bsz, seq, d)
    xs = _swa_sample_out(xs, a.reshape(n_dec, d), w_bo)
    y_sample, _ = _mlp(xs, row(g_mlp[1]), w_up_b, w_down_b, 1, row(g_final), tile=n_dec)
    y_sample = y_sample.reshape(n_dec, 1, d)

    return (y_prompt, y_sample, s_prompt[None], s_sample[None],
            _cache_keys_major(kt_prompt), _cache_keys_major(vt_prompt),
            _cache_keys_major(kt_sample), _cache_keys_major(vt_sample))
```

```python
import functools
import math
from typing import Callable, NamedTuple

import jax
import jax.numpy as jnp
import numpy as np
from jax import lax
from jax.experimental import pallas as pl
from jax.experimental.pallas import tpu as pltpu

F32 = jnp.float32
BF16 = jnp.bfloat16

EPS = 1e-6
NEG = -1e30
LOG2E = math.log2(math.e)
A_HEADS = 8
A_CHUNK = 64
B_HEAD_DIM = 64
B_KV_HEADS = 4
WINDOW = 128
N_BUCKETS = 32
MAX_DISTANCE = 128

V7X_VMEM_LIMIT_BYTES = 56 * 1024 * 1024
V7X_SUBLANES = 8
V7X_BF16_SUBLANES = 16

HGRN_TILE = 512
MLP_TILE = 512
SWA_TILE = 1024


def _cparams(*semantics):
    return pltpu.CompilerParams(dimension_semantics=semantics,
                                vmem_limit_bytes=V7X_VMEM_LIMIT_BYTES)


def _resident(shape):
    nd = len(shape)
    return pl.BlockSpec(shape, lambda *_: (0,) * nd, pipeline_mode=pl.Buffered(1))


def _resident_layer(shape, layer):
    nd = len(shape)
    return pl.BlockSpec((None,) + tuple(shape[1:]), lambda *_: (layer,) + (0,) * (nd - 1),
                        pipeline_mode=pl.Buffered(1))


def _rms_scale(x):
    return x * lax.rsqrt(jnp.mean(x * x, axis=-1, keepdims=True) + EPS)


def _sigmoid(x):
    return 1.0 / (1.0 + jnp.exp(-x))


def _dot(a, b):
    return jnp.dot(a, b, preferred_element_type=F32)


def _dot_nt(a, b):
    return lax.dot_general(a, b, (((1,), (1,)), ((), ())), preferred_element_type=F32)


def _dot_tn(a, b):
    return lax.dot_general(a, b, (((0,), (0,)), ((), ())), preferred_element_type=F32)


def _lower_bound(a_lb, layer):
    m = jnp.max(a_lb, axis=0, keepdims=True)
    e = jnp.exp(a_lb - m)
    return jnp.sum(e[: layer + 1], axis=0, keepdims=True) / jnp.sum(e, axis=0, keepdims=True)


def _forget_gate(f_raw, lb):
    return lb + (1.0 - lb) * _sigmoid(f_raw)


def _hgrn_prompt_kernel(layer, rider, x_ref, w_in_ref, a_lb_ref, g_mix_ref, g_norm_ref,
                        w_out_ref, *rest):
    n_in = len(rider.args) if rider else 0
    n_out = len(rider.out_shape) if rider else 0
    y_ref, s_out_ref = rest[n_in:n_in + 2]
    (st_ref, proj_ref, kk_ref, b_ref, qi_ref, ki_ref, qn_ref, ks_ref, vb_ref,
     dec_ref, att_ref, a_ref) = rest[n_in + 2 + n_out:]
    if rider:
        for _ in rider.items(*rest[:n_in], *rest[n_in + 2:n_in + 2 + n_out]):
            pass
    t = pl.program_id(1)
    tile, d = x_ref.shape[1], x_ref.shape[2]
    dk = d // A_HEADS
    c = A_CHUNK
    n_chunks = tile // c

    @pl.when(t == 0)
    def _():
        st_ref[...] = jnp.zeros_like(st_ref)
        dec_ref[...] = jnp.zeros_like(dec_ref)

    x = x_ref[0]
    h = (_rms_scale(x) * g_mix_ref[...]).astype(BF16)
    lb = _lower_bound(a_lb_ref[...], layer)
    g_norm = g_norm_ref[...]

    crow = lax.broadcasted_iota(jnp.int32, (c, c), 0)
    ccol = lax.broadcasted_iota(jnp.int32, (c, c), 1)
    causal = crow >= ccol

    chunk_row = lax.broadcasted_iota(jnp.int32, (c, d), 0)
    proj_ref[...] = _dot(h, w_in_ref[...])

    for ci in range(n_chunks):
        rows = slice(ci * c, (ci + 1) * c)
        forget = _forget_gate(proj_ref[rows, d:2 * d], lb)
        kk_ref[rows, :] = 1.0 - forget
        acc = jnp.log(forget)
        step = 1
        while step < c:
            shifted = pltpu.roll(acc, step, 0)
            acc = acc + jnp.where(chunk_row >= step, shifted, 0.0)
            step *= 2
        b_ref[rows, :] = acc

    for ci in range(n_chunks):
        rows = slice(ci * c, (ci + 1) * c)
        b = b_ref[rows, :]
        b_mid = b_ref[ci * c + c // 2 - 1:ci * c + c // 2, :]
        b_last = b_ref[ci * c + c - 1:ci * c + c, :]
        q = proj_ref[rows, 0:d]
        k = kk_ref[rows, :]
        qi_ref[rows, :] = (q * jnp.exp(b - b_mid)).astype(BF16)
        ki_ref[rows, :] = (k * jnp.exp(b_mid - b)).astype(BF16)
        qn_ref[rows, :] = (q * jnp.exp(b)).astype(BF16)
        ks_ref[rows, :] = (k * jnp.exp(b_last - b)).astype(BF16)
        vb_ref[rows, :] = proj_ref[rows, 2 * d:3 * d].astype(BF16)
        decay = jnp.exp(b_last)
        for hd in range(A_HEADS):
            dec_ref[ci * A_HEADS + hd:ci * A_HEADS + hd + 1, :] = decay[:, hd * dk:(hd + 1) * dk]

    for ci in range(n_chunks):
        rows = slice(ci * c, (ci + 1) * c)
        for hd in range(A_HEADS):
            ls = slice(hd * dk, (hd + 1) * dk)
            att = jnp.where(causal, _dot_nt(qi_ref[rows, ls], ki_ref[rows, ls]), 0.0)
            att_ref[ci * A_HEADS + hd] = att.astype(BF16)

    dec_t = dec_ref[...].T
    for ci in range(n_chunks):
        rows = slice(ci * c, (ci + 1) * c)
        for hd in range(A_HEADS):
            ls = slice(hd * dk, (hd + 1) * dk)
            j = ci * A_HEADS + hd
            st = st_ref[hd]
            lhs = jnp.concatenate([qn_ref[rows, ls], att_ref[j]], axis=1)
            rhs = jnp.concatenate([st.astype(BF16), vb_ref[rows, ls]], axis=0)
            o = _dot(lhs, rhs)
            st_ref[hd] = st * dec_t[:, j:j + 1] + _dot_tn(ks_ref[rows, ls], vb_ref[rows, ls])
            g = proj_ref[rows, 3 * d + hd * dk:3 * d + (hd + 1) * dk]
            a_ref[rows, ls] = (_rms_scale(o) * g_norm * (g * _sigmoid(g))).astype(BF16)

    y_ref[0] = x + _dot(a_ref[...], w_out_ref[...])

    @pl.when(t == pl.num_programs(1) - 1)
    def _():
        s_out_ref[0] = st_ref[...]


def _cast_items(*refs):
    n = len(refs) // 2
    for src_ref, dst_ref in zip(refs[:n], refs[n:]):
        dst_ref[...] = src_ref[...].astype(dst_ref.dtype)
        yield


def _bf16_cast_rider(arrays, n_outer, n_inner):
    n_steps = n_outer * n_inner
    flat = tuple(a.reshape(-1, a.shape[-1]) for a in arrays)
    specs = []
    for a in flat:
        rows = a.shape[0] // n_steps
        assert rows * n_steps == a.shape[0] and rows % 16 == 0, a.shape
        specs.append(pl.BlockSpec((rows, a.shape[1]), lambda b, t: (b * n_inner + t, 0)))
    return _Rider(
        name="bf16_cast",
        items=_cast_items,
        n_items=len(flat),
        args=flat,
        in_specs=tuple(specs),
        out_shape=tuple(jax.ShapeDtypeStruct(a.shape, BF16) for a in flat),
        out_specs=tuple(specs),
    )


def _hgrn_prompt(x, w_in, a_lb, g_mix, g_norm, w_out, layer, tile, make_rider=None):
    bsz, seq, d = x.shape
    dk = d // A_HEADS
    tile = min(tile, seq)
    rider = make_rider(bsz, seq // tile) if make_rider else None
    outs = pl.pallas_call(
        functools.partial(_hgrn_prompt_kernel, layer, rider),
        grid=(bsz, seq // tile),
        in_specs=[
            pl.BlockSpec((1, tile, d), lambda b, t: (b, t, 0)),
            _resident(w_in.shape),
            _resident(a_lb.shape),
            _resident(g_mix.shape),
            _resident(g_norm.shape),
            _resident(w_out.shape),
        ] + list(rider.in_specs if rider else ()),
        out_specs=[
            pl.BlockSpec((1, tile, d), lambda b, t: (b, t, 0)),
            pl.BlockSpec((1, A_HEADS, dk, dk), lambda b, t: (b, 0, 0, 0)),
        ] + list(rider.out_specs if rider else ()),
        out_shape=[
            jax.ShapeDtypeStruct((bsz, seq, d), F32),
            jax.ShapeDtypeStruct((bsz, A_HEADS, dk, dk), F32),
        ] + list(rider.out_shape if rider else ()),
        scratch_shapes=[
            pltpu.VMEM((A_HEADS, dk, dk), F32),
            pltpu.VMEM((tile, 4 * d), F32),
            pltpu.VMEM((tile, d), F32),
            pltpu.VMEM((tile, d), F32),
            pltpu.VMEM((tile, d), BF16),
            pltpu.VMEM((tile, d), BF16),
            pltpu.VMEM((tile, d), BF16),
            pltpu.VMEM((tile, d), BF16),
            pltpu.VMEM((tile, d), BF16),
            pltpu.VMEM((dk, dk), F32),
            pltpu.VMEM((tile // A_CHUNK * A_HEADS, A_CHUNK, A_CHUNK), BF16),
            pltpu.VMEM((tile, d), BF16),
        ],
        compiler_params=_cparams("arbitrary", "arbitrary"),
        name="hgrn_prompt_with_" + rider.name if rider else "hgrn_prompt",
    )(x, w_in, a_lb, g_mix, g_norm, w_out, *(rider.args if rider else ()))
    return outs[0], outs[1], outs[2:]


class _Rider(NamedTuple):
    name: str
    items: Callable
    n_items: int
    args: tuple
    in_specs: tuple
    out_shape: tuple
    out_specs: tuple


def _mlp_kernel(ff_chunk, has_final, rider, x_ref, g_ref, w_up_ref, w_down_ref, *rest):
    rest = list(rest)
    g_final_ref = rest.pop(0) if has_final else None
    n_in = len(rider.args) if rider else 0
    y_ref = rest[n_in]
    side = rider.items(*rest[:n_in], *rest[n_in + 1:]) if rider else iter(())
    n_chunks = w_up_ref.shape[1] // ff_chunk
    x = x_ref[...]
    h = (_rms_scale(x) * g_ref[...]).astype(BF16)
    acc = x
    for j in range(n_chunks):
        cols = slice(j * ff_chunk, (j + 1) * ff_chunk)
        u = jnp.maximum(_dot(h, w_up_ref[:, cols]), 0.0)
        acc = acc + _dot((u * u).astype(BF16), w_down_ref[cols, :])
        for _ in range(rider.n_items // n_chunks if rider else 0):
            next(side, None)
    for _ in side:
        pass
    if has_final:
        acc = _rms_scale(acc) * g_final_ref[...]
    y_ref[...] = acc


def _mlp(x, g, w_up, w_down, layer, g_final, tile, rider=None, ff_chunk=1024):
    m, d = x.shape
    tile = min(tile, m)
    has_final = g_final is not None
    in_specs = [
        pl.BlockSpec((tile, d), lambda i: (i, 0)),
        _resident(g.shape),
        _resident_layer(w_up.shape, layer),
        _resident_layer(w_down.shape, layer),
    ]
    args = [x, g, w_up, w_down]
    if has_final:
        in_specs.append(_resident(g_final.shape))
        args.append(g_final)
    out_specs = [pl.BlockSpec((tile, d), lambda i: (i, 0))]
    out_shape = [jax.ShapeDtypeStruct((m, d), F32)]
    name = "mlp_final" if has_final else "mlp"
    if rider:
        in_specs += list(rider.in_specs)
        args += list(rider.args)
        out_specs += list(rider.out_specs)
        out_shape += list(rider.out_shape)
        name += "_with_" + rider.name
    outs = pl.pallas_call(
        functools.partial(_mlp_kernel, ff_chunk, has_final, rider),
        grid=(m // tile,),
        in_specs=in_specs,
        out_specs=out_specs,
        out_shape=out_shape,
        compiler_params=_cparams("arbitrary"),
        name=name,
    )(*args)
    return outs[0], outs[1:]


def _t5_bucket_table(w):
    dist = np.arange(w)[None, :] + w - np.arange(2 * w)[:, None]
    n = np.maximum(dist, 0)
    max_exact = N_BUCKETS // 2
    nf = np.maximum(n, 1).astype(np.float32)
    large = max_exact + (np.log(nf / np.float32(max_exact))
                         / np.float32(math.log(MAX_DISTANCE / max_exact))
                         * np.float32(N_BUCKETS - max_exact)).astype(np.int32)
    large = np.minimum(large, N_BUCKETS - 1)
    return np.where(n < max_exact, n, large).astype(np.int32)


def _band_bias_heads(rel_bias_ref, bucket_ref, out_ref, first_head):
    n, _, w = out_ref.shape
    kj = lax.broadcasted_iota(jnp.int32, (2 * w, w), 0)
    qi = lax.broadcasted_iota(jnp.int32, (2 * w, w), 1)
    dist = qi + w - kj
    bucket = bucket_ref[...]
    valid = (dist >= 0) & (dist < WINDOW)
    for hl in range(n):
        bias = jnp.zeros((2 * w, w), F32)
        for bk in range(N_BUCKETS):
            bias = jnp.where(bucket == bk, rel_bias_ref[bk, first_head + hl], bias)
        out_ref[hl] = jnp.where(valid, bias * LOG2E, NEG)


def _col_max(x):
    return jnp.max(x, axis=0, keepdims=True)


def _col_sum(x):
    return jnp.sum(x, axis=0, keepdims=True)


def _swa_prompt_kernel(x_ref, g_kv_ref, w_kv_ref, g_q_ref, w_qt_ref, sink_ref, bias_ref, w_out_ref,
                       y_ref, k_out_ref, v_out_ref, k_ref, vt_ref, qt_ref, at_ref, s_ref, p_ref):
    t = pl.program_id(1)
    tile, d = x_ref.shape[1], x_ref.shape[2]
    w = WINDOW
    hd = B_HEAD_DIM
    n_heads = d // hd
    group = n_heads // B_KV_HEADS
    kvd = B_KV_HEADS * hd
    scale = 1.0 / math.sqrt(hd)

    @pl.when(t == 0)
    def _():
        k_ref[:, 0:w, :] = jnp.zeros((B_KV_HEADS, w, hd), BF16)
        vt_ref[:, 0:w] = jnp.zeros((kvd, w), BF16)

    @pl.when(t > 0)
    def _():
        k_ref[:, 0:w, :] = k_ref[:, tile:tile + w, :]
        vt_ref[:, 0:w] = vt_ref[:, tile:tile + w]

    x = x_ref[0]
    xn = _rms_scale(x)
    kv = _dot((xn * g_kv_ref[...]).astype(BF16), w_kv_ref[...])
    for kh in range(B_KV_HEADS):
        k_ref[kh, w:w + tile, :] = kv[:, kh * hd:(kh + 1) * hd].astype(BF16)
    vt_ref[:, w:w + tile] = kv[:, kvd:2 * kvd].T.astype(BF16)
    h_q = (xn * g_q_ref[...]).astype(BF16)
    qt_ref[...] = (_dot_nt(w_qt_ref[...], h_q) * (scale * LOG2E)).astype(BF16)

    @pl.when(t == pl.num_programs(1) - 1)
    def _():
        k_out_ref[0] = kv[tile - w:, 0:kvd].T
        v_out_ref[0] = kv[tile - w:, kvd:2 * kvd].T

    first = t == 0
    sc = hd
    rc = V7X_BF16_SUBLANES
    units = [(blk, kh) for blk in range(tile // w) for kh in range(B_KV_HEADS)]

    def scores(u):
        blk, kh = units[u]
        cols = slice(blk * w, (blk + 1) * w)
        heads = range(kh * group, (kh + 1) * group)
        q4 = jnp.concatenate([qt_ref[h * hd:(h + 1) * hd, cols] for h in heads], axis=1)
        m = None
        for r in range(2 * w // sc):
            s = _dot(k_ref[kh, blk * w + r * sc:blk * w + (r + 1) * sc, :], q4)
            s = s + jnp.concatenate([bias_ref[h, r * sc:(r + 1) * sc, :] for h in heads], axis=1)
            if blk == 0 and r * sc < w:
                s = jnp.where(first, NEG, s)
            s_ref[u % 2, r * sc:(r + 1) * sc, :] = s
            for i in range(sc // V7X_SUBLANES):
                part = s[i * V7X_SUBLANES:(i + 1) * V7X_SUBLANES, :]
                m = part if m is None else jnp.maximum(m, part)
        return m

    def softmax(u, m8):
        blk, kh = units[u]
        heads = range(kh * group, (kh + 1) * group)
        sink = jnp.concatenate([jnp.full((1, w), sink_ref[0, h] * LOG2E, F32) for h in heads],
                               axis=1)
        m = jnp.maximum(_col_max(m8), sink)
        for r in range(2 * w // rc):
            p = jnp.exp2(s_ref[u % 2, r * rc:(r + 1) * rc, :] - m)
            p_ref[u % 2, r * rc:(r + 1) * rc, :] = p.astype(BF16)
        return jnp.exp2(sink - m)

    ones_rows = jnp.ones((rc, 2 * w), BF16)

    def weighted_values(u, sink_term):
        blk, kh = units[u]
        cols = slice(blk * w, (blk + 1) * w)
        keys = slice(blk * w, (blk + 2) * w)
        vt1 = jnp.concatenate([vt_ref[kh * hd:(kh + 1) * hd, keys], ones_rows], axis=0)
        o = _dot(vt1, p_ref[u % 2])
        o = o[0:hd] / (o[hd:hd + 1] + sink_term)
        for g in range(group):
            h = kh * group + g
            at_ref[h * hd:(h + 1) * hd, cols] = o[:, g * w:(g + 1) * w].astype(BF16)

    m_next = scores(0)
    denom_prev = None
    for u in range(len(units)):
        m_cur = m_next
        if u + 1 < len(units):
            m_next = scores(u + 1)
        denom = softmax(u, m_cur)
        if u > 0:
            weighted_values(u - 1, denom_prev)
        denom_prev = denom
    weighted_values(len(units) - 1, denom_prev)

    y_ref[0] = x + _dot_tn(at_ref[...], w_out_ref[...])


def _swa_prompt(x, g_kv, w_kv, g_q, w_qt, sink, band_bias, w_out, tile):
    bsz, seq, d = x.shape
    tile = min(tile, seq)
    kvd = w_kv.shape[1] // 2
    w = WINDOW
    return pl.pallas_call(
        _swa_prompt_kernel,
        grid=(bsz, seq // tile),
        in_specs=[
            pl.BlockSpec((1, tile, d), lambda b, t: (b, t, 0)),
            _resident(g_kv.shape),
            _resident(w_kv.shape),
            _resident(g_q.shape),
            _resident(w_qt.shape),
            pl.BlockSpec(memory_space=pltpu.SMEM),
            _resident(band_bias.shape),
            _resident(w_out.shape),
        ],
        out_specs=[
            pl.BlockSpec((1, tile, d), lambda b, t: (b, t, 0)),
            pl.BlockSpec((1, kvd, w), lambda b, t: (b, 0, 0)),
            pl.BlockSpec((1, kvd, w), lambda b, t: (b, 0, 0)),
        ],
        out_shape=[
            jax.ShapeDtypeStruct((bsz, seq, d), F32),
            jax.ShapeDtypeStruct((bsz, kvd, w), F32),
            jax.ShapeDtypeStruct((bsz, kvd, w), F32),
        ],
        scratch_shapes=[
            pltpu.VMEM((B_KV_HEADS, tile + w, B_HEAD_DIM), BF16),
            pltpu.VMEM((kvd, tile + w), BF16),
            pltpu.VMEM((d, tile), BF16),
            pltpu.VMEM((d, tile), BF16),
            pltpu.VMEM((2, 2 * w, d // B_KV_HEADS * w // B_HEAD_DIM), F32),
            pltpu.VMEM((2, 2 * w, d // B_KV_HEADS * w // B_HEAD_DIM), BF16),
        ],
        compiler_params=_cparams("arbitrary", "arbitrary"),
        name="swa_prompt",
    )(x, g_kv, w_kv, g_q, w_qt, sink, band_bias, w_out)


def _prologue_kernel(layer, x_ref, w_in_ref, w_out_ref, a_lb_ref, g_mix_ref, rel_bias_ref, bucket_ref,
                     bucket_row_ref,
                     w_in_bf_ref, w_out_bf_ref, ft_ref, q_ref, v_ref, g_ref, bias_ref, bias_row_ref):
    p = pl.program_id(0)
    d = x_ref.shape[1]
    dk = d // A_HEADS
    w_in_bf_ref[...] = w_in_ref[...].astype(BF16)
    w_out_bf_ref[...] = w_out_ref[...].astype(BF16)
    h = (_rms_scale(x_ref[...]) * g_mix_ref[...]).astype(BF16)
    part = _dot(h, w_in_bf_ref[...])

    @pl.when(p == 0)
    def _():
        q_ref[...] = part
        n_heads, n_buf = bias_row_ref.shape
        head = lax.broadcasted_iota(jnp.int32, (n_heads, n_buf), 0)
        bucket_row = bucket_row_ref[...]
        rows = jnp.zeros((n_heads, n_buf), F32)
        for hd in range(n_heads):
            row = jnp.zeros((1, n_buf), F32)
            for bk in range(N_BUCKETS):
                row = jnp.where(bucket_row == bk, rel_bias_ref[bk, hd], row)
            rows = jnp.where(head == hd, row * LOG2E, rows)
        bias_row_ref[...] = rows

    @pl.when(p == 1)
    def _():
        forget = _forget_gate(part, _lower_bound(a_lb_ref[...], layer))
        for hd in range(A_HEADS):
            ft_ref[hd] = forget[:, hd * dk:(hd + 1) * dk].T

    @pl.when(p == 2)
    def _():
        v_ref[...] = part

    @pl.when(p == 3)
    def _():
        g_ref[...] = part

    _band_bias_heads(rel_bias_ref, bucket_ref, bias_ref, p * bias_ref.shape[0])


def _prologue(x, w_in, w_out, a_lb, g_mix, rel_bias, layer):
    m, d = x.shape
    dk = d // A_HEADS
    parts = w_in.shape[1] // d
    n_heads = rel_bias.shape[1]
    assert parts == 4 and n_heads % parts == 0 and d % parts == 0
    w = WINDOW
    rows = pl.BlockSpec((d // parts, d), lambda p: (p, 0))
    cols = pl.BlockSpec((d, d), lambda p: (0, p))
    tokens = pl.BlockSpec((m, d), lambda p: (0, 0))
    table = _t5_bucket_table(w)
    bucket = jnp.asarray(table)
    bucket_row = jnp.asarray(table[1:w + 1, 0][None, :])
    return pl.pallas_call(
        functools.partial(_prologue_kernel, layer),
        grid=(parts,),
        in_specs=[
            tokens, cols, rows, _resident(a_lb.shape), _resident(g_mix.shape),
            pl.BlockSpec(memory_space=pltpu.SMEM),
            pl.BlockSpec((2 * w, w), lambda p: (0, 0)),
            pl.BlockSpec((1, w), lambda p: (0, 0)),
        ],
        out_specs=[
            cols, rows,
            pl.BlockSpec((A_HEADS, dk, m), lambda p: (0, 0, 0)),
            tokens, tokens, tokens,
            pl.BlockSpec((n_heads // parts, 2 * w, w), lambda p: (p, 0, 0)),
            pl.BlockSpec((n_heads, w), lambda p: (0, 0)),
        ],
        out_shape=[
            jax.ShapeDtypeStruct(w_in.shape, BF16),
            jax.ShapeDtypeStruct(w_out.shape, BF16),
            jax.ShapeDtypeStruct((A_HEADS, dk, m), F32),
            jax.ShapeDtypeStruct((m, d), F32),
            jax.ShapeDtypeStruct((m, d), F32),
            jax.ShapeDtypeStruct((m, d), F32),
            jax.ShapeDtypeStruct((n_heads, 2 * w, w), F32),
            jax.ShapeDtypeStruct((n_heads, w), F32),
        ],
        compiler_params=_cparams("arbitrary"),
        name="prologue",
    )(x, w_in, w_out, a_lb, g_mix, rel_bias, bucket, bucket_row)


def _hgrn_sample_state_items(ft_ref, q_ref, v_ref, s0_ref, s_ref, o_ref):
    i = pl.program_id(0)
    tb = s0_ref.shape[0]
    dk = s0_ref.shape[2]
    m = ft_ref.shape[2]
    lane = lax.broadcasted_iota(jnp.int32, (dk, m), 1)
    for tk in range(tb):
        pick = lane == i * tb + tk
        for hd in range(A_HEADS):
            ls = slice(hd * dk, (hd + 1) * dk)
            f = jnp.sum(jnp.where(pick, ft_ref[hd], 0.0), axis=1, keepdims=True)
            v = v_ref[0, tk:tk + 1, ls]
            s_new = v + f * (s0_ref[tk, hd] - v)
            s_ref[tk, hd] = s_new
            o = _dot(q_ref[0, :, ls].astype(BF16), s_new.astype(BF16))
            o_ref[0, tk:tk + 1, ls] = o[tk:tk + 1]
            yield


def _hgrn_sample_state_rider(ft, q, v, s0, n_steps):
    m, d = v.shape
    dk = d // A_HEADS
    assert m % n_steps == 0, "the sample group must split evenly over the host call's steps"
    tb = m // n_steps
    rows3 = pl.BlockSpec((1, tb, d), lambda i: (i, 0, 0))
    state = pl.BlockSpec((tb, A_HEADS, dk, dk), lambda i: (i, 0, 0, 0))
    return _Rider(
        name="hgrn_sample_state",
        items=_hgrn_sample_state_items,
        n_items=tb * A_HEADS,
        args=(ft, q.reshape(n_steps, tb, d), v.reshape(n_steps, tb, d), s0),
        in_specs=(_resident(ft.shape), rows3, rows3, state),
        out_shape=(jax.ShapeDtypeStruct(s0.shape, F32),
                   jax.ShapeDtypeStruct((n_steps, tb, d), F32)),
        out_specs=(state, rows3),
    )


def _hgrn_sample_out_kernel(x_ref, o_ref, g_ref, g_norm_ref, w_out_ref, y_ref):
    d = x_ref.shape[1]
    dk = d // A_HEADS
    g = g_ref[...]
    gate = g * _sigmoid(g)
    parts = []
    for hd in range(A_HEADS):
        ls = slice(hd * dk, (hd + 1) * dk)
        parts.append(_rms_scale(o_ref[:, ls]) * g_norm_ref[...] * gate[:, ls])
    a = jnp.concatenate(parts, axis=1).astype(BF16)
    y_ref[...] = x_ref[...] + _dot(a, w_out_ref[...])


def _hgrn_sample_out(x, o, g, g_norm, w_out):
    vm = pl.BlockSpec(memory_space=pltpu.VMEM)
    return pl.pallas_call(
        _hgrn_sample_out_kernel,
        in_specs=[vm] * 5,
        out_specs=vm,
        out_shape=jax.ShapeDtypeStruct(x.shape, F32),
        compiler_params=pltpu.CompilerParams(vmem_limit_bytes=V7X_VMEM_LIMIT_BYTES),
        name="hgrn_sample_out",
    )(x, o, g, g_norm, w_out)


def _swa_sample_proj_kernel(x_ref, g_kv_ref, w_kvt_ref, g_q_ref, w_q_ref, kvt_ref, q_ref):
    scale = 1.0 / math.sqrt(B_HEAD_DIM)
    xn = _rms_scale(x_ref[...])
    kvt_ref[...] = _dot_nt(w_kvt_ref[...], (xn * g_kv_ref[...]).astype(BF16))
    q_ref[...] = (_dot((xn * g_q_ref[...]).astype(BF16), w_q_ref[...])
                  * (scale * LOG2E)).astype(BF16)


def _swa_sample_proj(x, g_kv, w_kvt, g_q, w_q):
    m = x.shape[0]
    vm = pl.BlockSpec(memory_space=pltpu.VMEM)
    return pl.pallas_call(
        _swa_sample_proj_kernel,
        in_specs=[vm] * 5,
        out_specs=[vm, vm],
        out_shape=[
            jax.ShapeDtypeStruct((w_kvt.shape[0], m), F32),
            jax.ShapeDtypeStruct((m, w_q.shape[1]), BF16),
        ],
        compiler_params=pltpu.CompilerParams(vmem_limit_bytes=V7X_VMEM_LIMIT_BYTES),
        name="swa_sample_proj",
    )(x, g_kv, w_kvt, g_q, w_q)


def _swa_sample_attend_items(q_ref, kvt_new_ref, kt_buf_ref, vt_buf_ref, bias_ref, sink_ref,
                             kt_out_ref, vt_out_ref, a_ref):
    i = pl.program_id(0)
    tb, kvd, n_buf = kt_buf_ref.shape
    hd = B_HEAD_DIM
    n_heads = q_ref.shape[0] // tb
    group = n_heads // B_KV_HEADS
    m_tok = kvt_new_ref.shape[1]
    tok_lane = lax.broadcasted_iota(jnp.int32, (2 * kvd, m_tok), 1)
    key_lane = lax.broadcasted_iota(jnp.int32, (kvd, n_buf), 1)
    for tk in range(tb):
        new_col = jnp.sum(jnp.where(tok_lane == i * tb + tk, kvt_new_ref[...], 0.0),
                          axis=1, keepdims=True)
        for buf_ref, out_ref, rows in ((kt_buf_ref, kt_out_ref, slice(0, kvd)),
                                       (vt_buf_ref, vt_out_ref, slice(kvd, 2 * kvd))):
            slid = pltpu.roll(buf_ref[tk], n_buf - 1, 1)
            out_ref[tk] = jnp.where(key_lane == n_buf - 1, new_col[rows], slid)
        yield

    e_row = lax.broadcasted_iota(jnp.int32, (hd, kvd), 0)
    e_col = lax.broadcasted_iota(jnp.int32, (hd, kvd), 1)
    spread = jnp.where(e_col % hd == e_row, 1.0, 0.0).astype(BF16)
    head = lax.broadcasted_iota(jnp.int32, (tb, n_heads, kvd), 1)
    slot = lax.broadcasted_iota(jnp.int32, (tb, n_heads, kvd), 2)
    own_slot = head // group == slot // hd
    qx = _dot(q_ref[...], spread).reshape(tb, n_heads, kvd)
    qx = jnp.where(own_slot, qx, 0.0).astype(BF16)
    yield

    kt = kt_out_ref[...].astype(BF16)
    vt = vt_out_ref[...].astype(BF16)
    s = jnp.einsum("bhc,bcr->bhr", qx, kt, preferred_element_type=F32)
    s = s + bias_ref[...][None]
    sink = sink_ref[...][None] * LOG2E
    m = jnp.maximum(jnp.max(s, axis=-1, keepdims=True), sink)
    p = jnp.exp2(s - m)
    denom = jnp.sum(p, axis=-1, keepdims=True) + jnp.exp2(sink - m)
    yield
    r = jnp.einsum("bhr,bcr->bhc", p.astype(BF16), vt, preferred_element_type=F32) / denom
    r = jnp.where(own_slot, r, 0.0).astype(BF16).reshape(tb * n_heads, kvd)
    a_ref[...] = _dot_nt(r, spread).astype(BF16)
    yield


def _swa_sample_attend_rider(q_rows, kvt_new, kt_buf, vt_buf, bias_row, sink_col, n_steps):
    m, kvd, n_buf = kt_buf.shape
    n_heads = q_rows.shape[0] // m
    assert m % n_steps == 0, "the sample group must split evenly over the host call's steps"
    tb = m // n_steps
    cache = pl.BlockSpec((tb, kvd, n_buf), lambda i: (i, 0, 0))
    head_rows = pl.BlockSpec((tb * n_heads, B_HEAD_DIM), lambda i: (i, 0))
    return _Rider(
        name="swa_sample_attend",
        items=_swa_sample_attend_items,
        n_items=tb + 3,
        args=(q_rows, kvt_new, kt_buf, vt_buf, bias_row, sink_col),
        in_specs=(head_rows, _resident(kvt_new.shape), cache, cache,
                  _resident(bias_row.shape), _resident(sink_col.shape)),
        out_shape=(jax.ShapeDtypeStruct((m, kvd, n_buf), F32),
                   jax.ShapeDtypeStruct((m, kvd, n_buf), F32),
                   jax.ShapeDtypeStruct((m * n_heads, B_HEAD_DIM), BF16)),
        out_specs=(cache, cache, head_rows),
    )


def _swa_sample_out_kernel(x_ref, a_ref, w_out_ref, y_ref):
    y_ref[...] = x_ref[...] + _dot(a_ref[...], w_out_ref[...])


def _swa_sample_out(x, a, w_out):
    vm = pl.BlockSpec(memory_space=pltpu.VMEM)
    return pl.pallas_call(
        _swa_sample_out_kernel,
        in_specs=[vm] * 3,
        out_specs=vm,
        out_shape=jax.ShapeDtypeStruct(x.shape, F32),
        compiler_params=pltpu.CompilerParams(vmem_limit_bytes=V7X_VMEM_LIMIT_BYTES),
        name="swa_sample_out",
    )(x, a, w_out)


def _cache_keys_minor(cache):
    b, n, kvh, hd = cache.shape
    return jnp.transpose(cache, (0, 2, 3, 1)).reshape(b, kvh * hd, n)


def _cache_keys_major(cache_t):
    b, kvd, n = cache_t.shape
    return jnp.transpose(cache_t.reshape(b, B_KV_HEADS, kvd // B_KV_HEADS, n), (0, 3, 1, 2))


def kernel(x_prompt, x_sample, state_hgrn, cache_k_win, cache_v_win, w_a_in, a_lb, a_gnorm, w_a_out,
           g_mix, g_mlp, g_kv, w_kv, w_b_q, b_sink, w_b_out, rel_bias, w_up, w_down, g_final):
    bsz, seq, d = x_prompt.shape
    n_dec = x_sample.shape[0]
    assert x_sample.shape[1] == 1, "the sample group decodes one token per sequence"
    n_a = w_a_in.shape[0]
    n_b = w_b_q.shape[0]
    assert n_a == 1 and n_b == 1, "depth-2 trunk: one HGRN2 layer, then one attention layer"
    n_buf = cache_k_win.shape[1]
    assert n_buf == WINDOW and seq % WINDOW == 0
    kvd = B_KV_HEADS * B_HEAD_DIM
    n_heads = d // B_HEAD_DIM

    row = lambda g: g.reshape(1, -1)
    sink = b_sink[0].reshape(1, n_heads)

    mlp_tile = min(MLP_TILE, bsz * seq)
    mlp_steps = bsz * seq // mlp_tile
    xs = x_sample.reshape(n_dec, d)

    w_in, w_ao, ft, q, v, g, band_bias, bias_row = _prologue(
        xs, w_a_in[0], w_a_out[0], a_lb, row(g_mix[0]), rel_bias, layer=0)
    later = (w_up, w_down, w_kv, w_b_q[0], w_b_out[0])
    x, s_prompt, cast = _hgrn_prompt(
        x_prompt, w_in, a_lb, row(g_mix[0]), row(a_gnorm[0]), w_ao, layer=0, tile=HGRN_TILE,
        make_rider=functools.partial(_bf16_cast_rider, later))
    w_up_b, w_down_b, w_kvb, w_q, w_bo = (c.reshape(w.shape) for c, w in zip(cast, later))
    x, (s_sample, o) = _mlp(
        x.reshape(bsz * seq, d), row(g_mlp[0]), w_up_b, w_down_b, 0, None, tile=mlp_tile,
        rider=_hgrn_sample_state_rider(ft, q, v, state_hgrn[0], mlp_steps))
    xs = _hgrn_sample_out(xs, o.reshape(n_dec, d), g, row(a_gnorm[0]), w_ao)
    xs, _ = _mlp(xs, row(g_mlp[0]), w_up_b, w_down_b, 0, None, tile=n_dec)

    kvt_new, q = _swa_sample_proj(xs, row(g_kv), w_kvb.T, row(g_mix[1]), w_q)
    x, kt_prompt, vt_prompt = _swa_prompt(x.reshape(bsz, seq, d), row(g_kv), w_kvb,
                                          row(g_mix[1]), w_q.T, sink, band_bias, w_bo,
                                          tile=SWA_TILE)
    y_prompt, (kt_sample, vt_sample, a) = _mlp(
        x.reshape(bsz * seq, d), row(g_mlp[1]), w_up_b, w_down_b, 1, row(g_final),
        tile=mlp_tile,
        rider=_swa_sample_attend_rider(
            q.reshape(n_dec * n_heads, B_HEAD_DIM), kvt_new,
            _cache_keys_minor(cache_k_win), _cache_keys_minor(cache_v_win),
            bias_row, sink.reshape(n_heads, 1), mlp_steps))
    y_prompt = y_prompt.reshape(bsz, seq, d)
    xs = _swa_sample_out(xs, a.reshape(n_dec, d), w_bo)
    y_sample, _ = _mlp(xs, row(g_mlp[1]), w_up_b, w_down_b, 1, row(g_final), tile=n_dec)
    y_sample = y_sample.reshape(n_dec, 1, d)

    return (y_prompt, y_sample, s_prompt[None], s_sample[None],
            _cache_keys_major(kt_prompt), _cache_keys_major(vt_prompt),
            _cache_keys_major(kt_sample), _cache_keys_major(vt_sample))
```

```python
import functools
import math
from typing import Callable, NamedTuple

import jax
import jax.numpy as jnp
import numpy as np
from jax import lax
from jax.experimental import pallas as pl
from jax.experimental.pallas import tpu as pltpu

F32 = jnp.float32
BF16 = jnp.bfloat16

EPS = 1e-6
NEG = -1e30
LOG2E = math.log2(math.e)
A_HEADS = 8
A_CHUNK = 64
B_HEAD_DIM = 64
B_KV_HEADS = 4
WINDOW = 128
N_BUCKETS = 32
MAX_DISTANCE = 128

V7X_VMEM_LIMIT_BYTES = 56 * 1024 * 1024
V7X_SUBLANES = 8
V7X_BF16_SUBLANES = 16

HGRN_TILE = 512
MLP_TILE = 512
SWA_TILE = 1024


def _cparams(*semantics):
    return pltpu.CompilerParams(dimension_semantics=semantics,
                                vmem_limit_bytes=V7X_VMEM_LIMIT_BYTES)


def _resident(shape):
    nd = len(shape)
    return pl.BlockSpec(shape, lambda *_: (0,) * nd, pipeline_mode=pl.Buffered(1))


def _resident_layer(shape, layer):
    nd = len(shape)
    return pl.BlockSpec((None,) + tuple(shape[1:]), lambda *_: (layer,) + (0,) * (nd - 1),
                        pipeline_mode=pl.Buffered(1))


def _rms_scale(x):
    return x * lax.rsqrt(jnp.mean(x * x, axis=-1, keepdims=True) + EPS)


def _sigmoid(x):
    return 1.0 / (1.0 + jnp.exp(-x))


def _dot(a, b):
    return jnp.dot(a, b, preferred_element_type=F32)


def _dot_nt(a, b):
    return lax.dot_general(a, b, (((1,), (1,)), ((), ())), preferred_element_type=F32)


def _dot_tn(a, b):
    return lax.dot_general(a, b, (((0,), (0,)), ((), ())), preferred_element_type=F32)


def _lower_bound(a_lb, layer):
    m = jnp.max(a_lb, axis=0, keepdims=True)
    e = jnp.exp(a_lb - m)
    return jnp.sum(e[: layer + 1], axis=0, keepdims=True) / jnp.sum(e, axis=0, keepdims=True)


def _forget_gate(f_raw, lb):
    return lb + (1.0 - lb) * _sigmoid(f_raw)


def _hgrn_prompt_kernel(layer, rider, x_ref, w_in_ref, a_lb_ref, g_mix_ref, g_norm_ref,
                        w_out_ref, *rest):
    n_in = len(rider.args) if rider else 0
    n_out = len(rider.out_shape) if rider else 0
    y_ref, s_out_ref = rest[n_in:n_in + 2]
    (st_ref, proj_ref, kk_ref, b_ref, qi_ref, ki_ref, qn_ref, ks_ref, vb_ref,
     dec_ref, att_ref, a_ref) = rest[n_in + 2 + n_out:]
    if rider:
        for _ in rider.items(*rest[:n_in], *rest[n_in + 2:n_in + 2 + n_out]):
            pass
    t = pl.program_id(1)
    tile, d = x_ref.shape[1], x_ref.shape[2]
    dk = d // A_HEADS
    c = A_CHUNK
    n_chunks = tile // c

    @pl.when(t == 0)
    def _():
        st_ref[...] = jnp.zeros_like(st_ref)
        dec_ref[...] = jnp.zeros_like(dec_ref)

    x = x_ref[0]
    h = (_rms_scale(x) * g_mix_ref[...]).astype(BF16)
    lb = _lower_bound(a_lb_ref[...], layer)
    g_norm = g_norm_ref[...]

    crow = lax.broadcasted_iota(jnp.int32, (c, c), 0)
    ccol = lax.broadcasted_iota(jnp.int32, (c, c), 1)
    causal = crow >= ccol

    chunk_row = lax.broadcasted_iota(jnp.int32, (c, d), 0)
    proj_ref[...] = _dot(h, w_in_ref[...])

    for ci in range(n_chunks):
        rows = slice(ci * c, (ci + 1) * c)
        forget = _forget_gate(proj_ref[rows, d:2 * d], lb)
        kk_ref[rows, :] = 1.0 - forget
        acc = jnp.log(forget)
        step = 1
        while step < c:
            shifted = pltpu.roll(acc, step, 0)
            acc = acc + jnp.where(chunk_row >= step, shifted, 0.0)
            step *= 2
        b_ref[rows, :] = acc

    for ci in range(n_chunks):
        rows = slice(ci * c, (ci + 1) * c)
        b = b_ref[rows, :]
        b_mid = b_ref[ci * c + c // 2 - 1:ci * c + c // 2, :]
        b_last = b_ref[ci * c + c - 1:ci * c + c, :]
        q = proj_ref[rows, 0:d]
        k = kk_ref[rows, :]
        qi_ref[rows, :] = (q * jnp.exp(b - b_mid)).astype(BF16)
        ki_ref[rows, :] = (k * jnp.exp(b_mid - b)).astype(BF16)
        qn_ref[rows, :] = (q * jnp.exp(b)).astype(BF16)
        ks_ref[rows, :] = (k * jnp.exp(b_last - b)).astype(BF16)
        vb_ref[rows, :] = proj_ref[rows, 2 * d:3 * d].astype(BF16)
        decay = jnp.exp(b_last)
        for hd in range(A_HEADS):
            dec_ref[ci * A_HEADS + hd:ci * A_HEADS + hd + 1, :] = decay[:, hd * dk:(hd + 1) * dk]

    for ci in range(n_chunks):
        rows = slice(ci * c, (ci + 1) * c)
        for hd in range(A_HEADS):
            ls = slice(hd * dk, (hd + 1) * dk)
            att = jnp.where(causal, _dot_nt(qi_ref[rows, ls], ki_ref[rows, ls]), 0.0)
            att_ref[ci * A_HEADS + hd] = att.astype(BF16)

    dec_t = dec_ref[...].T
    for ci in range(n_chunks):
        rows = slice(ci * c, (ci + 1) * c)
        for hd in range(A_HEADS):
            ls = slice(hd * dk, (hd + 1) * dk)
            j = ci * A_HEADS + hd
            st = st_ref[hd]
            lhs = jnp.concatenate([qn_ref[rows, ls], att_ref[j]], axis=1)
            rhs = jnp.concatenate([st.astype(BF16), vb_ref[rows, ls]], axis=0)
            o = _dot(lhs, rhs)
            st_ref[hd] = st * dec_t[:, j:j + 1] + _dot_tn(ks_ref[rows, ls], vb_ref[rows, ls])
            g = proj_ref[rows, 3 * d + hd * dk:3 * d + (hd + 1) * dk]
            a_ref[rows, ls] = (_rms_scale(o) * g_norm * (g * _sigmoid(g))).astype(BF16)

    y_ref[0] = x + _dot(a_ref[...], w_out_ref[...])

    @pl.when(t == pl.num_programs(1) - 1)
    def _():
        s_out_ref[0] = st_ref[...]


def _cast_items(*refs):
    n = len(refs) // 2
    for src_ref, dst_ref in zip(refs[:n], refs[n:]):
        dst_ref[...] = src_ref[...].astype(dst_ref.dtype)
        yield


def _bf16_cast_rider(arrays, n_outer, n_inner):
    n_steps = n_outer * n_inner
    flat = tuple(a.reshape(-1, a.shape[-1]) for a in arrays)
    specs = []
    for a in flat:
        rows = a.shape[0] // n_steps
        assert rows * n_steps == a.shape[0] and rows % 16 == 0, a.shape
        specs.append(pl.BlockSpec((rows, a.shape[1]), lambda b, t: (b * n_inner + t, 0)))
    return _Rider(
        name="bf16_cast",
        items=_cast_items,
        n_items=len(flat),
        args=flat,
        in_specs=tuple(specs),
        out_shape=tuple(jax.ShapeDtypeStruct(a.shape, BF16) for a in flat),
        out_specs=tuple(specs),
    )


def _hgrn_prompt(x, w_in, a_lb, g_mix, g_norm, w_out, layer, tile, make_rider=None):
    bsz, seq, d = x.shape
    dk = d // A_HEADS
    tile = min(tile, seq)
    rider = make_rider(bsz, seq // tile) if make_rider else None
    outs = pl.pallas_call(
        functools.partial(_hgrn_prompt_kernel, layer, rider),
        grid=(bsz, seq // tile),
        in_specs=[
            pl.BlockSpec((1, tile, d), lambda b, t: (b, t, 0)),
            _resident(w_in.shape),
            _resident(a_lb.shape),
            _resident(g_mix.shape),
            _resident(g_norm.shape),
            _resident(w_out.shape),
        ] + list(rider.in_specs if rider else ()),
        out_specs=[
            pl.BlockSpec((1, tile, d), lambda b, t: (b, t, 0)),
            pl.BlockSpec((1, A_HEADS, dk, dk), lambda b, t: (b, 0, 0, 0)),
        ] + list(rider.out_specs if rider else ()),
        out_shape=[
            jax.ShapeDtypeStruct((bsz, seq, d), F32),
            jax.ShapeDtypeStruct((bsz, A_HEADS, dk, dk), F32),
        ] + list(rider.out_shape if rider else ()),
        scratch_shapes=[
            pltpu.VMEM((A_HEADS, dk, dk), F32),
            pltpu.VMEM((tile, 4 * d), F32),
            pltpu.VMEM((tile, d), F32),
            pltpu.VMEM((tile, d), F32),
            pltpu.VMEM((tile, d), BF16),
            pltpu.VMEM((tile, d), BF16),
            pltpu.VMEM((tile, d), BF16),
            pltpu.VMEM((tile, d), BF16),
            pltpu.VMEM((tile, d), BF16),
            pltpu.VMEM((dk, dk), F32),
            pltpu.VMEM((tile // A_CHUNK * A_HEADS, A_CHUNK, A_CHUNK), BF16),
            pltpu.VMEM((tile, d), BF16),
        ],
        compiler_params=_cparams("arbitrary", "arbitrary"),
        name="hgrn_prompt_with_" + rider.name if rider else "hgrn_prompt",
    )(x, w_in, a_lb, g_mix, g_norm, w_out, *(rider.args if rider else ()))
    return outs[0], outs[1], outs[2:]


class _Rider(NamedTuple):
    name: str
    items: Callable
    n_items: int
    args: tuple
    in_specs: tuple
    out_shape: tuple
    out_specs: tuple


def _mlp_kernel(ff_chunk, has_final, rider, x_ref, g_ref, w_up_ref, w_down_ref, *rest):
    rest = list(rest)
    g_final_ref = rest.pop(0) if has_final else None
    n_in = len(rider.args) if rider else 0
    y_ref = rest[n_in]
    side = rider.items(*rest[:n_in], *rest[n_in + 1:]) if rider else iter(())
    n_chunks = w_up_ref.shape[1] // ff_chunk
    x = x_ref[...]
    h = (_rms_scale(x) * g_ref[...]).astype(BF16)
    acc = x
    for j in range(n_chunks):
        cols = slice(j * ff_chunk, (j + 1) * ff_chunk)
        u = jnp.maximum(_dot(h, w_up_ref[:, cols]), 0.0)
        acc = acc + _dot((u * u).astype(BF16), w_down_ref[cols, :])
        for _ in range(rider.n_items // n_chunks if rider else 0):
            next(side, None)
    for _ in side:
        pass
    if has_final:
        acc = _rms_scale(acc) * g_final_ref[...]
    y_ref[...] = acc


def _mlp(x, g, w_up, w_down, layer, g_final, tile, rider=None, ff_chunk=1024):
    m, d = x.shape
    tile = min(tile, m)
    has_final = g_final is not None
    in_specs = [
        pl.BlockSpec((tile, d), lambda i: (i, 0)),
        _resident(g.shape),
        _resident_layer(w_up.shape, layer),
        _resident_layer(w_down.shape, layer),
    ]
    args = [x, g, w_up, w_down]
    if has_final:
        in_specs.append(_resident(g_final.shape))
        args.append(g_final)
    out_specs = [pl.BlockSpec((tile, d), lambda i: (i, 0))]
    out_shape = [jax.ShapeDtypeStruct((m, d), F32)]
    name = "mlp_final" if has_final else "mlp"
    if rider:
        in_specs += list(rider.in_specs)
        args += list(rider.args)
        out_specs += list(rider.out_specs)
        out_shape += list(rider.out_shape)
        name += "_with_" + rider.name
    outs = pl.pallas_call(
        functools.partial(_mlp_kernel, ff_chunk, has_final, rider),
        grid=(m // tile,),
        in_specs=in_specs,
        out_specs=out_specs,
        out_shape=out_shape,
        compiler_params=_cparams("arbitrary"),
        name=name,
    )(*args)
    return outs[0], outs[1:]


def _t5_bucket_table(w):
    dist = np.arange(w)[None, :] + w - np.arange(2 * w)[:, None]
    n = np.maximum(dist, 0)
    max_exact = N_BUCKETS // 2
    nf = np.maximum(n, 1).astype(np.float32)
    large = max_exact + (np.log(nf / np.float32(max_exact))
                         / np.float32(math.log(MAX_DISTANCE / max_exact))
                         * np.float32(N_BUCKETS - max_exact)).astype(np.int32)
    large = np.minimum(large, N_BUCKETS - 1)
    return np.where(n < max_exact, n, large).astype(np.int32)


def _band_bias_heads(rel_bias_ref, bucket_ref, out_ref, first_head):
    n, _, w = out_ref.shape
    kj = lax.broadcasted_iota(jnp.int32, (2 * w, w), 0)
    qi = lax.broadcasted_iota(jnp.int32, (2 * w, w), 1)
    dist = qi + w - kj
    bucket = bucket_ref[...]
    valid = (dist >= 0) & (dist < WINDOW)
    for hl in range(n):
        bias = jnp.zeros((2 * w, w), F32)
        for bk in range(N_BUCKETS):
            bias = jnp.where(bucket == bk, rel_bias_ref[bk, first_head + hl], bias)
        out_ref[hl] = jnp.where(valid, bias * LOG2E, NEG)


def _col_max(x):
    return jnp.max(x, axis=0, keepdims=True)


def _swa_prompt_kernel(x_ref, g_kv_ref, w_kv_ref, g_q_ref, w_qt_ref, sink_ref, bias_ref, w_out_ref,
                       y_ref, k_out_ref, v_out_ref, k_ref, vt_ref, qt_ref, at_ref, s_ref, p_ref):
    t = pl.program_id(1)
    tile, d = x_ref.shape[1], x_ref.shape[2]
    w = WINDOW
    hd = B_HEAD_DIM
    n_heads = d // hd
    group = n_heads // B_KV_HEADS
    kvd = B_KV_HEADS * hd
    scale = 1.0 / math.sqrt(hd)

    @pl.when(t == 0)
    def _():
        k_ref[:, 0:w, :] = jnp.zeros((B_KV_HEADS, w, hd), BF16)
        vt_ref[:, 0:w] = jnp.zeros((kvd, w), BF16)

    @pl.when(t > 0)
    def _():
        k_ref[:, 0:w, :] = k_ref[:, tile:tile + w, :]
        vt_ref[:, 0:w] = vt_ref[:, tile:tile + w]

    x = x_ref[0]
    xn = _rms_scale(x)
    kv = _dot((xn * g_kv_ref[...]).astype(BF16), w_kv_ref[...])
    for kh in range(B_KV_HEADS):
        k_ref[kh, w:w + tile, :] = kv[:, kh * hd:(kh + 1) * hd].astype(BF16)
    vt_ref[:, w:w + tile] = kv[:, kvd:2 * kvd].T.astype(BF16)
    h_q = (xn * g_q_ref[...]).astype(BF16)
    qt_ref[...] = (_dot_nt(w_qt_ref[...], h_q) * (scale * LOG2E)).astype(BF16)

    @pl.when(t == pl.num_programs(1) - 1)
    def _():
        k_out_ref[0] = kv[tile - w:, 0:kvd].T
        v_out_ref[0] = kv[tile - w:, kvd:2 * kvd].T

    first = t == 0
    sc = hd
    rc = V7X_BF16_SUBLANES
    units = [(blk, kh) for blk in range(tile // w) for kh in range(B_KV_HEADS)]

    def scores(u):
        blk, kh = units[u]
        cols = slice(blk * w, (blk + 1) * w)
        heads = range(kh * group, (kh + 1) * group)
        q4 = jnp.concatenate([qt_ref[h * hd:(h + 1) * hd, cols] for h in heads], axis=1)
        m = None
        for r in range(2 * w // sc):
            s = _dot(k_ref[kh, blk * w + r * sc:blk * w + (r + 1) * sc, :], q4)
            s = s + jnp.concatenate([bias_ref[h, r * sc:(r + 1) * sc, :] for h in heads], axis=1)
            if blk == 0 and r * sc < w:
                s = jnp.where(first, NEG, s)
            s_ref[u % 2, r * sc:(r + 1) * sc, :] = s
            for i in range(sc // V7X_SUBLANES):
                part = s[i * V7X_SUBLANES:(i + 1) * V7X_SUBLANES, :]
                m = part if m is None else jnp.maximum(m, part)
        return m

    def softmax(u, m8):
        blk, kh = units[u]
        heads = range(kh * group, (kh + 1) * group)
        sink = jnp.concatenate([jnp.full((1, w), sink_ref[0, h] * LOG2E, F32) for h in heads],
                               axis=1)
        m = jnp.maximum(_col_max(m8), sink)
        for r in range(2 * w // rc):
            p = jnp.exp2(s_ref[u % 2, r * rc:(r + 1) * rc, :] - m)
            p_ref[u % 2, r * rc:(r + 1) * rc, :] = p.astype(BF16)
        return jnp.exp2(sink - m)

    ones_rows = jnp.ones((rc, 2 * w), BF16)

    def weighted_values(u, sink_term):
        blk, kh = units[u]
        cols = slice(blk * w, (blk + 1) * w)
        keys = slice(blk * w, (blk + 2) * w)
        vt1 = jnp.concatenate([vt_ref[kh * hd:(kh + 1) * hd, keys], ones_rows], axis=0)
        o = _dot(vt1, p_ref[u % 2])
        o = o[0:hd] / (o[hd:hd + 1] + sink_term)
        for g in range(group):
            h = kh * group + g
            at_ref[h * hd:(h + 1) * hd, cols] = o[:, g * w:(g + 1) * w].astype(BF16)

    m_next = scores(0)
    denom_prev = None
    for u in range(len(units)):
        m_cur = m_next
        if u + 1 < len(units):
            m_next = scores(u + 1)
        denom = softmax(u, m_cur)
        if u > 0:
            weighted_values(u - 1, denom_prev)
        denom_prev = denom
    weighted_values(len(units) - 1, denom_prev)

    y_ref[0] = x + _dot_tn(at_ref[...], w_out_ref[...])


def _swa_prompt(x, g_kv, w_kv, g_q, w_qt, sink, band_bias, w_out, tile):
    bsz, seq, d = x.shape
    tile = min(tile, seq)
    kvd = w_kv.shape[1] // 2
    w = WINDOW
    return pl.pallas_call(
        _swa_prompt_kernel,
        grid=(bsz, seq // tile),
        in_specs=[
            pl.BlockSpec((1, tile, d), lambda b, t: (b, t, 0)),
            _resident(g_kv.shape),
            _resident(w_kv.shape),
            _resident(g_q.shape),
            _resident(w_qt.shape),
            pl.BlockSpec(memory_space=pltpu.SMEM),
            _resident(band_bias.shape),
            _resident(w_out.shape),
        ],
        out_specs=[
            pl.BlockSpec((1, tile, d), lambda b, t: (b, t, 0)),
            pl.BlockSpec((1, kvd, w), lambda b, t: (b, 0, 0)),
            pl.BlockSpec((1, kvd, w), lambda b, t: (b, 0, 0)),
        ],
        out_shape=[
            jax.ShapeDtypeStruct((bsz, seq, d), F32),
            jax.ShapeDtypeStruct((bsz, kvd, w), F32),
            jax.ShapeDtypeStruct((bsz, kvd, w), F32),
        ],
        scratch_shapes=[
            pltpu.VMEM((B_KV_HEADS, tile + w, B_HEAD_DIM), BF16),
            pltpu.VMEM((kvd, tile + w), BF16),
            pltpu.VMEM((d, tile), BF16),
            pltpu.VMEM((d, tile), BF16),
            pltpu.VMEM((2, 2 * w, d // B_KV_HEADS * w // B_HEAD_DIM), F32),
            pltpu.VMEM((2, 2 * w, d // B_KV_HEADS * w // B_HEAD_DIM), BF16),
        ],
        compiler_params=_cparams("arbitrary", "arbitrary"),
        name="swa_prompt",
    )(x, g_kv, w_kv, g_q, w_qt, sink, band_bias, w_out)


def _prologue_kernel(layer, x_ref, w_in_ref, w_out_ref, a_lb_ref, g_mix_ref, rel_bias_ref, bucket_ref,
                     bucket_row_ref,
                     w_in_bf_ref, w_out_bf_ref, ft_ref, q_ref, v_ref, g_ref, bias_ref, bias_row_ref):
    p = pl.program_id(0)
    d = x_ref.shape[1]
    dk = d // A_HEADS
    w_in_bf_ref[...] = w_in_ref[...].astype(BF16)
    w_out_bf_ref[...] = w_out_ref[...].astype(BF16)
    h = (_rms_scale(x_ref[...]) * g_mix_ref[...]).astype(BF16)
    part = _dot(h, w_in_bf_ref[...])

    @pl.when(p == 0)
    def _():
        q_ref[...] = part
        n_heads, n_buf = bias_row_ref.shape
        head = lax.broadcasted_iota(jnp.int32, (n_heads, n_buf), 0)
        bucket_row = bucket_row_ref[...]
        rows = jnp.zeros((n_heads, n_buf), F32)
        for hd in range(n_heads):
            row = jnp.zeros((1, n_buf), F32)
            for bk in range(N_BUCKETS):
                row = jnp.where(bucket_row == bk, rel_bias_ref[bk, hd], row)
            rows = jnp.where(head == hd, row * LOG2E, rows)
        bias_row_ref[...] = rows

    @pl.when(p == 1)
    def _():
        forget = _forget_gate(part, _lower_bound(a_lb_ref[...], layer))
        for hd in range(A_HEADS):
            ft_ref[hd] = forget[:, hd * dk:(hd + 1) * dk].T

    @pl.when(p == 2)
    def _():
        v_ref[...] = part

    @pl.when(p == 3)
    def _():
        g_ref[...] = part

    _band_bias_heads(rel_bias_ref, bucket_ref, bias_ref, p * bias_ref.shape[0])


def _prologue(x, w_in, w_out, a_lb, g_mix, rel_bias, layer):
    m, d = x.shape
    dk = d // A_HEADS
    parts = w_in.shape[1] // d
    n_heads = rel_bias.shape[1]
    assert parts == 4 and n_heads % parts == 0 and d % parts == 0
    w = WINDOW
    rows = pl.BlockSpec((d // parts, d), lambda p: (p, 0))
    cols = pl.BlockSpec((d, d), lambda p: (0, p))
    tokens = pl.BlockSpec((m, d), lambda p: (0, 0))
    table = _t5_bucket_table(w)
    bucket = jnp.asarray(table)
    bucket_row = jnp.asarray(table[1:w + 1, 0][None, :])
    return pl.pallas_call(
        functools.partial(_prologue_kernel, layer),
        grid=(parts,),
        in_specs=[
            tokens, cols, rows, _resident(a_lb.shape), _resident(g_mix.shape),
            pl.BlockSpec(memory_space=pltpu.SMEM),
            pl.BlockSpec((2 * w, w), lambda p: (0, 0)),
            pl.BlockSpec((1, w), lambda p: (0, 0)),
        ],
        out_specs=[
            cols, rows,
            pl.BlockSpec((A_HEADS, dk, m), lambda p: (0, 0, 0)),
            tokens, tokens, tokens,
            pl.BlockSpec((n_heads // parts, 2 * w, w), lambda p: (p, 0, 0)),
            pl.BlockSpec((n_heads, w), lambda p: (0, 0)),
        ],
        out_shape=[
            jax.ShapeDtypeStruct(w_in.shape, BF16),
            jax.ShapeDtypeStruct(w_out.shape, BF16),
            jax.ShapeDtypeStruct((A_HEADS, dk, m), F32),
            jax.ShapeDtypeStruct((m, d), F32),
            jax.ShapeDtypeStruct((m, d), F32),
            jax.ShapeDtypeStruct((m, d), F32),
            jax.ShapeDtypeStruct((n_heads, 2 * w, w), F32),
            jax.ShapeDtypeStruct((n_heads, w), F32),
        ],
        compiler_params=_cparams("arbitrary"),
        name="prologue",
    )(x, w_in, w_out, a_lb, g_mix, rel_bias, bucket, bucket_row)


def _hgrn_sample_state_items(ft_ref, q_ref, v_ref, s0_ref, s_ref, o_ref):
    i = pl.program_id(0)
    tb = s0_ref.shape[0]
    dk = s0_ref.shape[2]
    m = ft_ref.shape[2]
    lane = lax.broadcasted_iota(jnp.int32, (dk, m), 1)
    for tk in range(tb):
        pick = lane == i * tb + tk
        for hd in range(A_HEADS):
            ls = slice(hd * dk, (hd + 1) * dk)
            f = jnp.sum(jnp.where(pick, ft_ref[hd], 0.0), axis=1, keepdims=True)
            v = v_ref[0, tk:tk + 1, ls]
            s_new = v + f * (s0_ref[tk, hd] - v)
            s_ref[tk, hd] = s_new
            o = _dot(q_ref[0, :, ls].astype(BF16), s_new.astype(BF16))
            o_ref[0, tk:tk + 1, ls] = o[tk:tk + 1]
            yield


def _hgrn_sample_state_rider(ft, q, v, s0, n_steps):
    m, d = v.shape
    dk = d // A_HEADS
    assert m % n_steps == 0, "the sample group must split evenly over the host call's steps"
    tb = m // n_steps
    rows3 = pl.BlockSpec((1, tb, d), lambda i: (i, 0, 0))
    state = pl.BlockSpec((tb, A_HEADS, dk, dk), lambda i: (i, 0, 0, 0))
    return _Rider(
        name="hgrn_sample_state",
        items=_hgrn_sample_state_items,
        n_items=tb * A_HEADS,
        args=(ft, q.reshape(n_steps, tb, d), v.reshape(n_steps, tb, d), s0),
        in_specs=(_resident(ft.shape), rows3, rows3, state),
        out_shape=(jax.ShapeDtypeStruct(s0.shape, F32),
                   jax.ShapeDtypeStruct((n_steps, tb, d), F32)),
        out_specs=(state, rows3),
    )


def _hgrn_sample_out_kernel(x_ref, o_ref, g_ref, g_norm_ref, w_out_ref, y_ref):
    d = x_ref.shape[1]
    dk = d // A_HEADS
    g = g_ref[...]
    gate = g * _sigmoid(g)
    parts = []
    for hd in range(A_HEADS):
        ls = slice(hd * dk, (hd + 1) * dk)
        parts.append(_rms_scale(o_ref[:, ls]) * g_norm_ref[...] * gate[:, ls])
    a = jnp.concatenate(parts, axis=1).astype(BF16)
    y_ref[...] = x_ref[...] + _dot(a, w_out_ref[...])


def _hgrn_sample_out(x, o, g, g_norm, w_out):
    vm = pl.BlockSpec(memory_space=pltpu.VMEM)
    return pl.pallas_call(
        _hgrn_sample_out_kernel,
        in_specs=[vm] * 5,
        out_specs=vm,
        out_shape=jax.ShapeDtypeStruct(x.shape, F32),
        compiler_params=pltpu.CompilerParams(vmem_limit_bytes=V7X_VMEM_LIMIT_BYTES),
        name="hgrn_sample_out",
    )(x, o, g, g_norm, w_out)


def _swa_sample_proj_kernel(x_ref, g_kv_ref, w_kvt_ref, g_q_ref, w_q_ref, kvt_ref, q_ref):
    scale = 1.0 / math.sqrt(B_HEAD_DIM)
    xn = _rms_scale(x_ref[...])
    kvt_ref[...] = _dot_nt(w_kvt_ref[...], (xn * g_kv_ref[...]).astype(BF16))
    q_ref[...] = (_dot((xn * g_q_ref[...]).astype(BF16), w_q_ref[...])
                  * (scale * LOG2E)).astype(BF16)


def _swa_sample_proj(x, g_kv, w_kvt, g_q, w_q):
    m = x.shape[0]
    vm = pl.BlockSpec(memory_space=pltpu.VMEM)
    return pl.pallas_call(
        _swa_sample_proj_kernel,
        in_specs=[vm] * 5,
        out_specs=[vm, vm],
        out_shape=[
            jax.ShapeDtypeStruct((w_kvt.shape[0], m), F32),
            jax.ShapeDtypeStruct((m, w_q.shape[1]), BF16),
        ],
        compiler_params=pltpu.CompilerParams(vmem_limit_bytes=V7X_VMEM_LIMIT_BYTES),
        name="swa_sample_proj",
    )(x, g_kv, w_kvt, g_q, w_q)


def _swa_sample_attend_items(q_ref, kvt_new_ref, kt_buf_ref, vt_buf_ref, bias_ref, sink_ref,
                             kt_out_ref, vt_out_ref, a_ref):
    i = pl.program_id(0)
    tb, kvd, n_buf = kt_buf_ref.shape
    hd = B_HEAD_DIM
    n_heads = q_ref.shape[0] // tb
    group = n_heads // B_KV_HEADS
    m_tok = kvt_new_ref.shape[1]
    tok_lane = lax.broadcasted_iota(jnp.int32, (2 * kvd, m_tok), 1)
    key_lane = lax.broadcasted_iota(jnp.int32, (kvd, n_buf), 1)
    for tk in range(tb):
        new_col = jnp.sum(jnp.where(tok_lane == i * tb + tk, kvt_new_ref[...], 0.0),
                          axis=1, keepdims=True)
        for buf_ref, out_ref, rows in ((kt_buf_ref, kt_out_ref, slice(0, kvd)),
                                       (vt_buf_ref, vt_out_ref, slice(kvd, 2 * kvd))):
            slid = pltpu.roll(buf_ref[tk], n_buf - 1, 1)
            out_ref[tk] = jnp.where(key_lane == n_buf - 1, new_col[rows], slid)
        yield

    e_row = lax.broadcasted_iota(jnp.int32, (hd, kvd), 0)
    e_col = lax.broadcasted_iota(jnp.int32, (hd, kvd), 1)
    spread = jnp.where(e_col % hd == e_row, 1.0, 0.0).astype(BF16)
    head = lax.broadcasted_iota(jnp.int32, (tb, n_heads, kvd), 1)
    slot = lax.broadcasted_iota(jnp.int32, (tb, n_heads, kvd), 2)
    own_slot = head // group == slot // hd
    qx = _dot(q_ref[...], spread).reshape(tb, n_heads, kvd)
    qx = jnp.where(own_slot, qx, 0.0).astype(BF16)
    yield

    kt = kt_out_ref[...].astype(BF16)
    vt = vt_out_ref[...].astype(BF16)
    s = jnp.einsum("bhc,bcr->bhr", qx, kt, preferred_element_type=F32)
    s = s + bias_ref[...][None]
    sink = sink_ref[...][None] * LOG2E
    m = jnp.maximum(jnp.max(s, axis=-1, keepdims=True), sink)
    p = jnp.exp2(s - m)
    denom = jnp.sum(p, axis=-1, keepdims=True) + jnp.exp2(sink - m)
    yield
    r = jnp.einsum("bhr,bcr->bhc", p.astype(BF16), vt, preferred_element_type=F32) / denom
    r = jnp.where(own_slot, r, 0.0).astype(BF16).reshape(tb * n_heads, kvd)
    a_ref[...] = _dot_nt(r, spread).astype(BF16)
    yield


def _swa_sample_attend_rider(q_rows, kvt_new, kt_buf, vt_buf, bias_row, sink_col, n_steps):
    m, kvd, n_buf = kt_buf.shape
    n_heads = q_rows.shape[0] // m
    assert m % n_steps == 0, "the sample group must split evenly over the host call's steps"
    tb = m // n_steps
    cache = pl.BlockSpec((tb, kvd, n_buf), lambda i: (i, 0, 0))
    head_rows = pl.BlockSpec((tb * n_heads, B_HEAD_DIM), lambda i: (i, 0))
    return _Rider(
        name="swa_sample_attend",
        items=_swa_sample_attend_items,
        n_items=tb + 3,
        args=(q_rows, kvt_new, kt_buf, vt_buf, bias_row, sink_col),
        in_specs=(head_rows, _resident(kvt_new.shape), cache, cache,
                  _resident(bias_row.shape), _resident(sink_col.shape)),
        out_shape=(jax.ShapeDtypeStruct((m, kvd, n_buf), F32),
                   jax.ShapeDtypeStruct((m, kvd, n_buf), F32),
                   jax.ShapeDtypeStruct((m * n_heads, B_HEAD_DIM), BF16)),
        out_specs=(cache, cache, head_rows),
    )


def _swa_sample_out_kernel(x_ref, a_ref, w_out_ref, y_ref):
    y_ref[...] = x_ref[...] + _dot(a_ref[...], w_out_ref[...])


def _swa_sample_out(x, a, w_out):
    vm = pl.BlockSpec(memory_space=pltpu.VMEM)
    return pl.pallas_call(
        _swa_sample_out_kernel,
        in_specs=[vm] * 3,
        out_specs=vm,
        out_shape=jax.ShapeDtypeStruct(x.shape, F32),
        compiler_params=pltpu.CompilerParams(vmem_limit_bytes=V7X_VMEM_LIMIT_BYTES),
        name="swa_sample_out",
    )(x, a, w_out)


def _cache_keys_minor(cache):
    b, n, kvh, hd = cache.shape
    return jnp.transpose(cache, (0, 2, 3, 1)).reshape(b, kvh * hd, n)


def _cache_keys_major(cache_t):
    b, kvd, n = cache_t.shape
    return jnp.transpose(cache_t.reshape(b, B_KV_HEADS, kvd // B_KV_HEADS, n), (0, 3, 1, 2))


def kernel(x_prompt, x_sample, state_hgrn, cache_k_win, cache_v_win, w_a_in, a_lb, a_gnorm, w_a_out,
           g_mix, g_mlp, g_kv, w_kv, w_b_q, b_sink, w_b_out, rel_bias, w_up, w_down, g_final):
    bsz, seq, d = x_prompt.shape
    n_dec = x_sample.shape[0]
    assert x_sample.shape[1] == 1, "the sample group decodes one token per sequence"
    n_a = w_a_in.shape[0]
    n_b = w_b_q.shape[0]
    assert n_a == 1 and n_b == 1, "depth-2 trunk: one HGRN2 layer, then one attention layer"
    n_buf = cache_k_win.shape[1]
    assert n_buf == WINDOW and seq % WINDOW == 0
    kvd = B_KV_HEADS * B_HEAD_DIM
    n_heads = d // B_HEAD_DIM

    row = lambda g: g.reshape(1, -1)
    sink = b_sink[0].reshape(1, n_heads)

    mlp_tile = min(MLP_TILE, bsz * seq)
    mlp_steps = bsz * seq // mlp_tile
    xs = x_sample.reshape(n_dec, d)

    w_in, w_ao, ft, q, v, g, band_bias, bias_row = _prologue(
        xs, w_a_in[0], w_a_out[0], a_lb, row(g_mix[0]), rel_bias, layer=0)
    later = (w_up, w_down, w_kv, w_b_q[0], w_b_out[0])
    x, s_prompt, cast = _hgrn_prompt(
        x_prompt, w_in, a_lb, row(g_mix[0]), row(a_gnorm[0]), w_ao, layer=0, tile=HGRN_TILE,
        make_rider=functools.partial(_bf16_cast_rider, later))
    w_up_b, w_down_b, w_kvb, w_q, w_bo = (c.reshape(w.shape) for c, w in zip(cast, later))
    x, (s_sample, o) = _mlp(
        x.reshape(bsz * seq, d), row(g_mlp[0]), w_up_b, w_down_b, 0, None, tile=mlp_tile,
        rider=_hgrn_sample_state_rider(ft, q, v, state_hgrn[0], mlp_steps))
    xs = _hgrn_sample_out(xs, o.reshape(n_dec, d), g, row(a_gnorm[0]), w_ao)
    xs, _ = _mlp(xs, row(g_mlp[0]), w_up_b, w_down_b, 0, None, tile=n_dec)

    kvt_new, q = _swa_sample_proj(xs, row(g_kv), w_kvb.T, row(g_mix[1]), w_q)
    x, kt_prompt, vt_prompt = _swa_prompt(x.reshape(bsz, seq, d), row(g_kv), w_kvb,
                                          row(g_mix[1]), w_q.T, sink, band_bias, w_bo,
                                          tile=SWA_TILE)
    y_prompt, (kt_sample, vt_sample, a) = _mlp(
        x.reshape(bsz * seq, d), row(g_mlp[1]), w_up_b, w_down_b, 1, row(g_final),
        tile=mlp_tile,
        rider=_swa_sample_attend_rider(
            q.reshape(n_dec * n_heads, B_HEAD_DIM), kvt_new,
            _cache_keys_minor(cache_k_win), _cache_keys_minor(cache_v_win),
            bias_row, sink.reshape(n_heads, 1), mlp_steps))
    y_prompt = y_prompt.reshape(bsz, seq, d)
    xs = _swa_sample_out(xs, a.reshape(n_dec, d), w_bo)
    y_sample, _ = _mlp(xs, row(g_mlp[1]), w_up_b, w_down_b, 1, row(g_final), tile=n_dec)
    y_sample = y_sample.reshape(n_dec, 1, d)

    return (y_prompt, y_sample, s_prompt[None], s_sample[None],
            _cache_keys_major(kt_prompt), _cache_keys_major(vt_prompt),
            _cache_keys_major(kt_sample), _cache_keys_major(vt_sample))
```

```python
import functools
import math
from typing import Callable, NamedTuple

import jax
import jax.numpy as jnp
import numpy as np
from jax import lax
from jax.experimental import pallas as pl
from jax.experimental.pallas import tpu as pltpu

F32 = jnp.float32
BF16 = jnp.bfloat16

EPS = 1e-6
NEG = -1e30
LOG2E = math.log2(math.e)
A_HEADS = 8
A_CHUNK = 64
B_HEAD_DIM = 64
B_KV_HEADS = 4
WINDOW = 128
N_BUCKETS = 32
MAX_DISTANCE = 128

V7X_VMEM_LIMIT_BYTES = 56 * 1024 * 1024
V7X_SUBLANES = 8
V7X_BF16_SUBLANES = 16

HGRN_TILE = 512
MLP_TILE = 512
SWA_TILE = 1024


def _cparams(*semantics):
    return pltpu.CompilerParams(dimension_semantics=semantics,
                                vmem_limit_bytes=V7X_VMEM_LIMIT_BYTES)


def _resident(shape):
    nd = len(shape)
    return pl.BlockSpec(shape, lambda *_: (0,) * nd, pipeline_mode=pl.Buffered(1))


def _resident_layer(shape, layer):
    nd = len(shape)
    return pl.BlockSpec((None,) + tuple(shape[1:]), lambda *_: (layer,) + (0,) * (nd - 1),
                        pipeline_mode=pl.Buffered(1))


def _rms_scale(x):
    return x * lax.rsqrt(jnp.mean(x * x, axis=-1, keepdims=True) + EPS)


def _sigmoid(x):
    return 1.0 / (1.0 + jnp.exp(-x))


def _dot(a, b):
    return jnp.dot(a, b, preferred_element_type=F32)


def _dot_nt(a, b):
    return lax.dot_general(a, b, (((1,), (1,)), ((), ())), preferred_element_type=F32)


def _dot_tn(a, b):
    return lax.dot_general(a, b, (((0,), (0,)), ((), ())), preferred_element_type=F32)


def _lower_bound(a_lb, layer):
    m = jnp.max(a_lb, axis=0, keepdims=True)
    e = jnp.exp(a_lb - m)
    return jnp.sum(e[: layer + 1], axis=0, keepdims=True) / jnp.sum(e, axis=0, keepdims=True)


def _forget_gate(f_raw, lb):
    return lb + (1.0 - lb) * _sigmoid(f_raw)


def _hgrn_prompt_kernel(layer, rider, x_ref, w_in_ref, a_lb_ref, g_mix_ref, g_norm_ref,
                        w_out_ref, *rest):
    n_in = len(rider.args) if rider else 0
    n_out = len(rider.out_shape) if rider else 0
    y_ref, s_out_ref = rest[n_in:n_in + 2]
    (st_ref, proj_ref, kk_ref, b_ref, qi_ref, ki_ref, qn_ref, ks_ref, vb_ref,
     dec_ref, att_ref, a_ref) = rest[n_in + 2 + n_out:]
    if rider:
        for _ in rider.items(*rest[:n_in], *rest[n_in + 2:n_in + 2 + n_out]):
            pass
    t = pl.program_id(1)
    tile, d = x_ref.shape[1], x_ref.shape[2]
    dk = d // A_HEADS
    c = A_CHUNK
    n_chunks = tile // c

    @pl.when(t == 0)
    def _():
        st_ref[...] = jnp.zeros_like(st_ref)
        dec_ref[...] = jnp.zeros_like(dec_ref)

    x = x_ref[0]
    h = (_rms_scale(x) * g_mix_ref[...]).astype(BF16)
    lb = _lower_bound(a_lb_ref[...], layer)
    g_norm = g_norm_ref[...]

    crow = lax.broadcasted_iota(jnp.int32, (c, c), 0)
    ccol = lax.broadcasted_iota(jnp.int32, (c, c), 1)
    causal = crow >= ccol

    chunk_row = lax.broadcasted_iota(jnp.int32, (c, d), 0)
    proj_ref[...] = _dot(h, w_in_ref[...])

    for ci in range(n_chunks):
        rows = slice(ci * c, (ci + 1) * c)
        forget = _forget_gate(proj_ref[rows, d:2 * d], lb)
        kk_ref[rows, :] = 1.0 - forget
        acc = jnp.log(forget)
        step = 1
        while step < c:
            shifted = pltpu.roll(acc, step, 0)
            acc = acc + jnp.where(chunk_row >= step, shifted, 0.0)
            step *= 2
        b_ref[rows, :] = acc

    for ci in range(n_chunks):
        rows = slice(ci * c, (ci + 1) * c)
        b = b_ref[rows, :]
        b_mid = b_ref[ci * c + c // 2 - 1:ci * c + c // 2, :]
        b_last = b_ref[ci * c + c - 1:ci * c + c, :]
        q = proj_ref[rows, 0:d]
        k = kk_ref[rows, :]
        qi_ref[rows, :] = (q * jnp.exp(b - b_mid)).astype(BF16)
        ki_ref[rows, :] = (k * jnp.exp(b_mid - b)).astype(BF16)
        qn_ref[rows, :] = (q * jnp.exp(b)).astype(BF16)
        ks_ref[rows, :] = (k * jnp.exp(b_last - b)).astype(BF16)
        vb_ref[rows, :] = proj_ref[rows, 2 * d:3 * d].astype(BF16)
        decay = jnp.exp(b_last)
        for hd in range(A_HEADS):
            dec_ref[ci * A_HEADS + hd:ci * A_HEADS + hd + 1, :] = decay[:, hd * dk:(hd + 1) * dk]

    for ci in range(n_chunks):
        rows = slice(ci * c, (ci + 1) * c)
        for hd in range(A_HEADS):
            ls = slice(hd * dk, (hd + 1) * dk)
            att = jnp.where(causal, _dot_nt(qi_ref[rows, ls], ki_ref[rows, ls]), 0.0)
            att_ref[ci * A_HEADS + hd] = att.astype(BF16)

    dec_t = dec_ref[...].T
    for ci in range(n_chunks):
        rows = slice(ci * c, (ci + 1) * c)
        for hd in range(A_HEADS):
            ls = slice(hd * dk, (hd + 1) * dk)
            j = ci * A_HEADS + hd
            st = st_ref[hd]
            lhs = jnp.concatenate([qn_ref[rows, ls], att_ref[j]], axis=1)
            rhs = jnp.concatenate([st.astype(BF16), vb_ref[rows, ls]], axis=0)
            o = _dot(lhs, rhs)
            st_ref[hd] = st * dec_t[:, j:j + 1] + _dot_tn(ks_ref[rows, ls], vb_ref[rows, ls])
            g = proj_ref[rows, 3 * d + hd * dk:3 * d + (hd + 1) * dk]
            a_ref[rows, ls] = (_rms_scale(o) * g_norm * (g * _sigmoid(g))).astype(BF16)

    y_ref[0] = x + _dot(a_ref[...], w_out_ref[...])

    @pl.when(t == pl.num_programs(1) - 1)
    def _():
        s_out_ref[0] = st_ref[...]


def _cast_items(*refs):
    n = len(refs) // 2
    for src_ref, dst_ref in zip(refs[:n], refs[n:]):
        dst_ref[...] = src_ref[...].astype(dst_ref.dtype)
        yield


def _bf16_cast_rider(arrays, n_outer, n_inner):
    n_steps = n_outer * n_inner
    flat = tuple(a.reshape(-1, a.shape[-1]) for a in arrays)
    specs = []
    for a in flat:
        rows = a.shape[0] // n_steps
        assert rows * n_steps == a.shape[0] and rows % 16 == 0, a.shape
        specs.append(pl.BlockSpec((rows, a.shape[1]), lambda b, t: (b * n_inner + t, 0)))
    return _Rider(
        name="bf16_cast",
        items=_cast_items,
        n_items=len(flat),
        args=flat,
        in_specs=tuple(specs),
        out_shape=tuple(jax.ShapeDtypeStruct(a.shape, BF16) for a in flat),
        out_specs=tuple(specs),
    )


def _hgrn_prompt(x, w_in, a_lb, g_mix, g_norm, w_out, layer, tile, make_rider=None):
    bsz, seq, d = x.shape
    dk = d // A_HEADS
    tile = min(tile, seq)
    rider = make_rider(bsz, seq // tile) if make_rider else None
    outs = pl.pallas_call(
        functools.partial(_hgrn_prompt_kernel, layer, rider),
        grid=(bsz, seq // tile),
        in_specs=[
            pl.BlockSpec((1, tile, d), lambda b, t: (b, t, 0)),
            _resident(w_in.shape),
            _resident(a_lb.shape),
            _resident(g_mix.shape),
            _resident(g_norm.shape),
            _resident(w_out.shape),
        ] + list(rider.in_specs if rider else ()),
        out_specs=[
            pl.BlockSpec((1, tile, d), lambda b, t: (b, t, 0)),
            pl.BlockSpec((1, A_HEADS, dk, dk), lambda b, t: (b, 0, 0, 0)),
        ] + list(rider.out_specs if rider else ()),
        out_shape=[
            jax.ShapeDtypeStruct((bsz, seq, d), F32),
            jax.ShapeDtypeStruct((bsz, A_HEADS, dk, dk), F32),
        ] + list(rider.out_shape if rider else ()),
        scratch_shapes=[
            pltpu.VMEM((A_HEADS, dk, dk), F32),
            pltpu.VMEM((tile, 4 * d), F32),
            pltpu.VMEM((tile, d), F32),
            pltpu.VMEM((tile, d), F32),
            pltpu.VMEM((tile, d), BF16),
            pltpu.VMEM((tile, d), BF16),
            pltpu.VMEM((tile, d), BF16),
            pltpu.VMEM((tile, d), BF16),
            pltpu.VMEM((tile, d), BF16),
            pltpu.VMEM((dk, dk), F32),
            pltpu.VMEM((tile // A_CHUNK * A_HEADS, A_CHUNK, A_CHUNK), BF16),
            pltpu.VMEM((tile, d), BF16),
        ],
        compiler_params=_cparams("arbitrary", "arbitrary"),
        name="hgrn_prompt_with_" + rider.name if rider else "hgrn_prompt",
    )(x, w_in, a_lb, g_mix, g_norm, w_out, *(rider.args if rider else ()))
    return outs[0], outs[1], outs[2:]


class _Rider(NamedTuple):
    name: str
    items: Callable
    n_items: int
    args: tuple
    in_specs: tuple
    out_shape: tuple
    out_specs: tuple


def _mlp_kernel(ff_chunk, has_final, rider, x_ref, g_ref, w_up_ref, w_down_ref, *rest):
    rest = list(rest)
    g_final_ref = rest.pop(0) if has_final else None
    n_in = len(rider.args) if rider else 0
    y_ref = rest[n_in]
    side = rider.items(*rest[:n_in], *rest[n_in + 1:]) if rider else iter(())
    n_chunks = w_up_ref.shape[1] // ff_chunk
    x = x_ref[...]
    h = (_rms_scale(x) * g_ref[...]).astype(BF16)
    acc = x
    for j in range(n_chunks):
        cols = slice(j * ff_chunk, (j + 1) * ff_chunk)
        u = jnp.maximum(_dot(h, w_up_ref[:, cols]), 0.0)
        acc = acc + _dot((u * u).astype(BF16), w_down_ref[cols, :])
        for _ in range(-(-rider.n_items // n_chunks) if rider else 0):
            next(side, None)
    for _ in side:
        pass
    if has_final:
        acc = _rms_scale(acc) * g_final_ref[...]
    y_ref[...] = acc


def _mlp(x, g, w_up, w_down, layer, g_final, tile, rider=None, ff_chunk=1024):
    m, d = x.shape
    tile = min(tile, m)
    has_final = g_final is not None
    in_specs = [
        pl.BlockSpec((tile, d), lambda i: (i, 0)),
        _resident(g.shape),
        _resident_layer(w_up.shape, layer),
        _resident_layer(w_down.shape, layer),
    ]
    args = [x, g, w_up, w_down]
    if has_final:
        in_specs.append(_resident(g_final.shape))
        args.append(g_final)
    out_specs = [pl.BlockSpec((tile, d), lambda i: (i, 0))]
    out_shape = [jax.ShapeDtypeStruct((m, d), F32)]
    name = "mlp_final" if has_final else "mlp"
    if rider:
        in_specs += list(rider.in_specs)
        args += list(rider.args)
        out_specs += list(rider.out_specs)
        out_shape += list(rider.out_shape)
        name += "_with_" + rider.name
    outs = pl.pallas_call(
        functools.partial(_mlp_kernel, ff_chunk, has_final, rider),
        grid=(m // tile,),
        in_specs=in_specs,
        out_specs=out_specs,
        out_shape=out_shape,
        compiler_params=_cparams("arbitrary"),
        name=name,
    )(*args)
    return outs[0], outs[1:]


def _t5_bucket_table(w):
    dist = np.arange(w)[None, :] + w - np.arange(2 * w)[:, None]
    n = np.maximum(dist, 0)
    max_exact = N_BUCKETS // 2
    nf = np.maximum(n, 1).astype(np.float32)
    large = max_exact + (np.log(nf / np.float32(max_exact))
                         / np.float32(math.log(MAX_DISTANCE / max_exact))
                         * np.float32(N_BUCKETS - max_exact)).astype(np.int32)
    large = np.minimum(large, N_BUCKETS - 1)
    return np.where(n < max_exact, n, large).astype(np.int32)


def _band_bias_heads(rel_bias_ref, bucket_ref, out_ref, first_head):
    n, _, w = out_ref.shape
    kj = lax.broadcasted_iota(jnp.int32, (2 * w, w), 0)
    qi = lax.broadcasted_iota(jnp.int32, (2 * w, w), 1)
    dist = qi + w - kj
    bucket = bucket_ref[...]
    valid = (dist >= 0) & (dist < WINDOW)
    for hl in range(n):
        bias = jnp.zeros((2 * w, w), F32)
        for bk in range(N_BUCKETS):
            bias = jnp.where(bucket == bk, rel_bias_ref[bk, first_head + hl], bias)
        out_ref[hl] = jnp.where(valid, bias * LOG2E, NEG)


def _col_max(x):
    return jnp.max(x, axis=0, keepdims=True)


def _swa_prompt_kernel(x_ref, g_kv_ref, w_kv_ref, g_q_ref, w_qt_ref, sink_ref, bias_ref, w_out_ref,
                       y_ref, k_out_ref, v_out_ref, k_ref, vt_ref, qt_ref, at_ref, s_ref, p_ref):
    t = pl.program_id(1)
    tile, d = x_ref.shape[1], x_ref.shape[2]
    w = WINDOW
    hd = B_HEAD_DIM
    n_heads = d // hd
    group = n_heads // B_KV_HEADS
    kvd = B_KV_HEADS * hd
    scale = 1.0 / math.sqrt(hd)

    @pl.when(t == 0)
    def _():
        k_ref[:, 0:w, :] = jnp.zeros((B_KV_HEADS, w, hd), BF16)
        vt_ref[:, 0:w] = jnp.zeros((kvd, w), BF16)

    @pl.when(t > 0)
    def _():
        k_ref[:, 0:w, :] = k_ref[:, tile:tile + w, :]
        vt_ref[:, 0:w] = vt_ref[:, tile:tile + w]

    x = x_ref[0]
    xn = _rms_scale(x)
    kv = _dot((xn * g_kv_ref[...]).astype(BF16), w_kv_ref[...])
    for kh in range(B_KV_HEADS):
        k_ref[kh, w:w + tile, :] = kv[:, kh * hd:(kh + 1) * hd].astype(BF16)
    vt_ref[:, w:w + tile] = kv[:, kvd:2 * kvd].T.astype(BF16)
    h_q = (xn * g_q_ref[...]).astype(BF16)
    qt_ref[...] = (_dot_nt(w_qt_ref[...], h_q) * (scale * LOG2E)).astype(BF16)

    @pl.when(t == pl.num_programs(1) - 1)
    def _():
        k_out_ref[0] = kv[tile - w:, 0:kvd].T
        v_out_ref[0] = kv[tile - w:, kvd:2 * kvd].T

    first = t == 0
    sc = hd
    rc = V7X_BF16_SUBLANES
    units = [(blk, kh) for blk in range(tile // w) for kh in range(B_KV_HEADS)]

    def scores(u):
        blk, kh = units[u]
        cols = slice(blk * w, (blk + 1) * w)
        heads = range(kh * group, (kh + 1) * group)
        q4 = jnp.concatenate([qt_ref[h * hd:(h + 1) * hd, cols] for h in heads], axis=1)
        m = None
        for r in range(2 * w // sc):
            s = _dot(k_ref[kh, blk * w + r * sc:blk * w + (r + 1) * sc, :], q4)
            s = s + jnp.concatenate([bias_ref[h, r * sc:(r + 1) * sc, :] for h in heads], axis=1)
            if blk == 0 and r * sc < w:
                s = jnp.where(first, NEG, s)
            s_ref[u % 2, r * sc:(r + 1) * sc, :] = s
            for i in range(sc // V7X_SUBLANES):
                part = s[i * V7X_SUBLANES:(i + 1) * V7X_SUBLANES, :]
                m = part if m is None else jnp.maximum(m, part)
        return m

    def softmax(u, m8):
        blk, kh = units[u]
        heads = range(kh * group, (kh + 1) * group)
        sink = jnp.concatenate([jnp.full((1, w), sink_ref[0, h] * LOG2E, F32) for h in heads],
                               axis=1)
        m = jnp.maximum(_col_max(m8), sink)
        for r in range(2 * w // rc):
            p = jnp.exp2(s_ref[u % 2, r * rc:(r + 1) * rc, :] - m)
            p_ref[u % 2, r * rc:(r + 1) * rc, :] = p.astype(BF16)
        return jnp.exp2(sink - m)

    ones_rows = jnp.ones((rc, 2 * w), BF16)

    def weighted_values(u, sink_term):
        blk, kh = units[u]
        cols = slice(blk * w, (blk + 1) * w)
        keys = slice(blk * w, (blk + 2) * w)
        vt1 = jnp.concatenate([vt_ref[kh * hd:(kh + 1) * hd, keys], ones_rows], axis=0)
        o = _dot(vt1, p_ref[u % 2])
        o = o[0:hd] / (o[hd:hd + 1] + sink_term)
        for g in range(group):
            h = kh * group + g
            at_ref[h * hd:(h + 1) * hd, cols] = o[:, g * w:(g + 1) * w].astype(BF16)

    m_next = scores(0)
    denom_prev = None
    for u in range(len(units)):
        m_cur = m_next
        if u + 1 < len(units):
            m_next = scores(u + 1)
        denom = softmax(u, m_cur)
        if u > 0:
            weighted_values(u - 1, denom_prev)
        denom_prev = denom
    weighted_values(len(units) - 1, denom_prev)

    y_ref[0] = x + _dot_tn(at_ref[...], w_out_ref[...])


def _swa_prompt(x, g_kv, w_kv, g_q, w_qt, sink, band_bias, w_out, tile):
    bsz, seq, d = x.shape
    tile = min(tile, seq)
    kvd = w_kv.shape[1] // 2
    w = WINDOW
    return pl.pallas_call(
        _swa_prompt_kernel,
        grid=(bsz, seq // tile),
        in_specs=[
            pl.BlockSpec((1, tile, d), lambda b, t: (b, t, 0)),
            _resident(g_kv.shape),
            _resident(w_kv.shape),
            _resident(g_q.shape),
            _resident(w_qt.shape),
            pl.BlockSpec(memory_space=pltpu.SMEM),
            _resident(band_bias.shape),
            _resident(w_out.shape),
        ],
        out_specs=[
            pl.BlockSpec((1, tile, d), lambda b, t: (b, t, 0)),
            pl.BlockSpec((1, kvd, w), lambda b, t: (b, 0, 0)),
            pl.BlockSpec((1, kvd, w), lambda b, t: (b, 0, 0)),
        ],
        out_shape=[
            jax.ShapeDtypeStruct((bsz, seq, d), F32),
            jax.ShapeDtypeStruct((bsz, kvd, w), F32),
            jax.ShapeDtypeStruct((bsz, kvd, w), F32),
        ],
        scratch_shapes=[
            pltpu.VMEM((B_KV_HEADS, tile + w, B_HEAD_DIM), BF16),
            pltpu.VMEM((kvd, tile + w), BF16),
            pltpu.VMEM((d, tile), BF16),
            pltpu.VMEM((d, tile), BF16),
            pltpu.VMEM((2, 2 * w, d // B_KV_HEADS * w // B_HEAD_DIM), F32),
            pltpu.VMEM((2, 2 * w, d // B_KV_HEADS * w // B_HEAD_DIM), BF16),
        ],
        compiler_params=_cparams("arbitrary", "arbitrary"),
        name="swa_prompt",
    )(x, g_kv, w_kv, g_q, w_qt, sink, band_bias, w_out)


def _prologue_kernel(layer, x_ref, w_in_ref, w_out_ref, a_lb_ref, g_mix_ref, rel_bias_ref, bucket_ref,
                     bucket_row_ref,
                     w_in_bf_ref, w_out_bf_ref, ft_ref, q_ref, v_ref, g_ref, bias_ref, bias_row_ref):
    p = pl.program_id(0)
    d = x_ref.shape[1]
    dk = d // A_HEADS
    w_in_bf_ref[...] = w_in_ref[...].astype(BF16)
    w_out_bf_ref[...] = w_out_ref[...].astype(BF16)
    h = (_rms_scale(x_ref[...]) * g_mix_ref[...]).astype(BF16)
    part = _dot(h, w_in_bf_ref[...])

    @pl.when(p == 0)
    def _():
        q_ref[...] = part
        n_heads, n_buf = bias_row_ref.shape
        head = lax.broadcasted_iota(jnp.int32, (n_heads, n_buf), 0)
        bucket_row = bucket_row_ref[...]
        rows = jnp.zeros((n_heads, n_buf), F32)
        for hd in range(n_heads):
            row = jnp.zeros((1, n_buf), F32)
            for bk in range(N_BUCKETS):
                row = jnp.where(bucket_row == bk, rel_bias_ref[bk, hd], row)
            rows = jnp.where(head == hd, row * LOG2E, rows)
        bias_row_ref[...] = rows

    @pl.when(p == 1)
    def _():
        forget = _forget_gate(part, _lower_bound(a_lb_ref[...], layer))
        for hd in range(A_HEADS):
            ft_ref[hd] = forget[:, hd * dk:(hd + 1) * dk].T

    @pl.when(p == 2)
    def _():
        v_ref[...] = part

    @pl.when(p == 3)
    def _():
        g_ref[...] = part

    _band_bias_heads(rel_bias_ref, bucket_ref, bias_ref, p * bias_ref.shape[0])


def _prologue(x, w_in, w_out, a_lb, g_mix, rel_bias, layer):
    m, d = x.shape
    dk = d // A_HEADS
    parts = w_in.shape[1] // d
    n_heads = rel_bias.shape[1]
    assert parts == 4 and n_heads % parts == 0 and d % parts == 0
    w = WINDOW
    rows = pl.BlockSpec((d // parts, d), lambda p: (p, 0))
    cols = pl.BlockSpec((d, d), lambda p: (0, p))
    tokens = pl.BlockSpec((m, d), lambda p: (0, 0))
    table = _t5_bucket_table(w)
    bucket = jnp.asarray(table)
    bucket_row = jnp.asarray(table[1:w + 1, 0][None, :])
    return pl.pallas_call(
        functools.partial(_prologue_kernel, layer),
        grid=(parts,),
        in_specs=[
            tokens, cols, rows, _resident(a_lb.shape), _resident(g_mix.shape),
            pl.BlockSpec(memory_space=pltpu.SMEM),
            pl.BlockSpec((2 * w, w), lambda p: (0, 0)),
            pl.BlockSpec((1, w), lambda p: (0, 0)),
        ],
        out_specs=[
            cols, rows,
            pl.BlockSpec((A_HEADS, dk, m), lambda p: (0, 0, 0)),
            tokens, tokens, tokens,
            pl.BlockSpec((n_heads // parts, 2 * w, w), lambda p: (p, 0, 0)),
            pl.BlockSpec((n_heads, w), lambda p: (0, 0)),
        ],
        out_shape=[
            jax.ShapeDtypeStruct(w_in.shape, BF16),
            jax.ShapeDtypeStruct(w_out.shape, BF16),
            jax.ShapeDtypeStruct((A_HEADS, dk, m), F32),
            jax.ShapeDtypeStruct((m, d), F32),
            jax.ShapeDtypeStruct((m, d), F32),
            jax.ShapeDtypeStruct((m, d), F32),
            jax.ShapeDtypeStruct((n_heads, 2 * w, w), F32),
            jax.ShapeDtypeStruct((n_heads, w), F32),
        ],
        compiler_params=_cparams("arbitrary"),
        name="prologue",
    )(x, w_in, w_out, a_lb, g_mix, rel_bias, bucket, bucket_row)


def _hgrn_sample_state_items(ft_ref, q_ref, v_ref, s0_ref, s_ref, o_ref):
    i = pl.program_id(0)
    tb = s0_ref.shape[0]
    dk = s0_ref.shape[2]
    m = ft_ref.shape[2]
    lane = lax.broadcasted_iota(jnp.int32, (dk, m), 1)
    for tk in range(tb):
        pick = lane == i * tb + tk
        for hd in range(A_HEADS):
            ls = slice(hd * dk, (hd + 1) * dk)
            f = jnp.sum(jnp.where(pick, ft_ref[hd], 0.0), axis=1, keepdims=True)
            v = v_ref[0, tk:tk + 1, ls]
            s_new = v + f * (s0_ref[tk, hd] - v)
            s_ref[tk, hd] = s_new
            o = _dot(q_ref[0, :, ls].astype(BF16), s_new.astype(BF16))
            o_ref[0, tk:tk + 1, ls] = o[tk:tk + 1]
            yield


def _hgrn_sample_state_rider(ft, q, v, s0, n_steps):
    m, d = v.shape
    dk = d // A_HEADS
    assert m % n_steps == 0, "the sample group must split evenly over the host call's steps"
    tb = m // n_steps
    rows3 = pl.BlockSpec((1, tb, d), lambda i: (i, 0, 0))
    state = pl.BlockSpec((tb, A_HEADS, dk, dk), lambda i: (i, 0, 0, 0))
    return _Rider(
        name="hgrn_sample_state",
        items=_hgrn_sample_state_items,
        n_items=tb * A_HEADS,
        args=(ft, q.reshape(n_steps, tb, d), v.reshape(n_steps, tb, d), s0),
        in_specs=(_resident(ft.shape), rows3, rows3, state),
        out_shape=(jax.ShapeDtypeStruct(s0.shape, F32),
                   jax.ShapeDtypeStruct((n_steps, tb, d), F32)),
        out_specs=(state, rows3),
    )


def _hgrn_sample_out_kernel(x_ref, o_ref, g_ref, g_norm_ref, w_out_ref, y_ref):
    d = x_ref.shape[1]
    dk = d // A_HEADS
    g = g_ref[...]
    gate = g * _sigmoid(g)
    parts = []
    for hd in range(A_HEADS):
        ls = slice(hd * dk, (hd + 1) * dk)
        parts.append(_rms_scale(o_ref[:, ls]) * g_norm_ref[...] * gate[:, ls])
    a = jnp.concatenate(parts, axis=1).astype(BF16)
    y_ref[...] = x_ref[...] + _dot(a, w_out_ref[...])


def _hgrn_sample_out(x, o, g, g_norm, w_out):
    vm = pl.BlockSpec(memory_space=pltpu.VMEM)
    return pl.pallas_call(
        _hgrn_sample_out_kernel,
        in_specs=[vm] * 5,
        out_specs=vm,
        out_shape=jax.ShapeDtypeStruct(x.shape, F32),
        compiler_params=pltpu.CompilerParams(vmem_limit_bytes=V7X_VMEM_LIMIT_BYTES),
        name="hgrn_sample_out",
    )(x, o, g, g_norm, w_out)


def _swa_sample_proj_kernel(x_ref, g_kv_ref, w_kvt_ref, g_q_ref, w_q_ref, kvt_ref, q_ref):
    scale = 1.0 / math.sqrt(B_HEAD_DIM)
    xn = _rms_scale(x_ref[...])
    kvt_ref[...] = _dot_nt(w_kvt_ref[...], (xn * g_kv_ref[...]).astype(BF16))
    q_ref[...] = (_dot((xn * g_q_ref[...]).astype(BF16), w_q_ref[...])
                  * (scale * LOG2E)).astype(BF16)


def _swa_sample_proj(x, g_kv, w_kvt, g_q, w_q):
    m = x.shape[0]
    vm = pl.BlockSpec(memory_space=pltpu.VMEM)
    return pl.pallas_call(
        _swa_sample_proj_kernel,
        in_specs=[vm] * 5,
        out_specs=[vm, vm],
        out_shape=[
            jax.ShapeDtypeStruct((w_kvt.shape[0], m), F32),
            jax.ShapeDtypeStruct((m, w_q.shape[1]), BF16),
        ],
        compiler_params=pltpu.CompilerParams(vmem_limit_bytes=V7X_VMEM_LIMIT_BYTES),
        name="swa_sample_proj",
    )(x, g_kv, w_kvt, g_q, w_q)


def _swa_sample_attend_items(q_ref, kvt_new_ref, kt_buf_ref, vt_buf_ref, bias_ref, sink_ref,
                             kt_out_ref, vt_out_ref, a_ref):
    i = pl.program_id(0)
    tb, kvd, n_buf = kt_buf_ref.shape
    hd = B_HEAD_DIM
    n_heads = q_ref.shape[0] // tb
    group = n_heads // B_KV_HEADS
    m_tok = kvt_new_ref.shape[1]
    tok_lane = lax.broadcasted_iota(jnp.int32, (2 * kvd, m_tok), 1)
    key_lane = lax.broadcasted_iota(jnp.int32, (kvd, n_buf), 1)
    for tk in range(tb):
        new_col = jnp.sum(jnp.where(tok_lane == i * tb + tk, kvt_new_ref[...], 0.0),
                          axis=1, keepdims=True)
        for buf_ref, out_ref, rows in ((kt_buf_ref, kt_out_ref, slice(0, kvd)),
                                       (vt_buf_ref, vt_out_ref, slice(kvd, 2 * kvd))):
            slid = pltpu.roll(buf_ref[tk], n_buf - 1, 1)
            out_ref[tk] = jnp.where(key_lane == n_buf - 1, new_col[rows], slid)
        yield

    e_row = lax.broadcasted_iota(jnp.int32, (hd, kvd), 0)
    e_col = lax.broadcasted_iota(jnp.int32, (hd, kvd), 1)
    spread = jnp.where(e_col % hd == e_row, 1.0, 0.0).astype(BF16)
    head = lax.broadcasted_iota(jnp.int32, (tb, n_heads, kvd), 1)
    slot = lax.broadcasted_iota(jnp.int32, (tb, n_heads, kvd), 2)
    own_slot = head // group == slot // hd
    qx = _dot(q_ref[...], spread).reshape(tb, n_heads, kvd)
    qx = jnp.where(own_slot, qx, 0.0).astype(BF16)
    yield

    kt = kt_out_ref[...].astype(BF16)
    vt = vt_out_ref[...].astype(BF16)
    s = jnp.einsum("bhc,bcr->bhr", qx, kt, preferred_element_type=F32)
    s = s + bias_ref[...][None]
    sink = sink_ref[...][None] * LOG2E
    m = jnp.maximum(jnp.max(s, axis=-1, keepdims=True), sink)
    p = jnp.exp2(s - m)
    denom = jnp.sum(p, axis=-1, keepdims=True) + jnp.exp2(sink - m)
    yield
    r = jnp.einsum("bhr,bcr->bhc", p.astype(BF16), vt, preferred_element_type=F32) / denom
    r = jnp.where(own_slot, r, 0.0).astype(BF16).reshape(tb * n_heads, kvd)
    a_ref[...] = _dot_nt(r, spread).astype(BF16)
    yield


def _swa_sample_attend_rider(q_rows, kvt_new, kt_buf, vt_buf, bias_row, sink_col, n_steps):
    m, kvd, n_buf = kt_buf.shape
    n_heads = q_rows.shape[0] // m
    assert m % n_steps == 0, "the sample group must split evenly over the host call's steps"
    tb = m // n_steps
    cache = pl.BlockSpec((tb, kvd, n_buf), lambda i: (i, 0, 0))
    head_rows = pl.BlockSpec((tb * n_heads, B_HEAD_DIM), lambda i: (i, 0))
    return _Rider(
        name="swa_sample_attend",
        items=_swa_sample_attend_items,
        n_items=tb + 3,
        args=(q_rows, kvt_new, kt_buf, vt_buf, bias_row, sink_col),
        in_specs=(head_rows, _resident(kvt_new.shape), cache, cache,
                  _resident(bias_row.shape), _resident(sink_col.shape)),
        out_shape=(jax.ShapeDtypeStruct((m, kvd, n_buf), F32),
                   jax.ShapeDtypeStruct((m, kvd, n_buf), F32),
                   jax.ShapeDtypeStruct((m * n_heads, B_HEAD_DIM), BF16)),
        out_specs=(cache, cache, head_rows),
    )


def _swa_sample_out_kernel(x_ref, a_ref, w_out_ref, y_ref):
    y_ref[...] = x_ref[...] + _dot(a_ref[...], w_out_ref[...])


def _swa_sample_out(x, a, w_out):
    vm = pl.BlockSpec(memory_space=pltpu.VMEM)
    return pl.pallas_call(
        _swa_sample_out_kernel,
        in_specs=[vm] * 3,
        out_specs=vm,
        out_shape=jax.ShapeDtypeStruct(x.shape, F32),
        compiler_params=pltpu.CompilerParams(vmem_limit_bytes=V7X_VMEM_LIMIT_BYTES),
        name="swa_sample_out",
    )(x, a, w_out)


def _cache_keys_minor(cache):
    b, n, kvh, hd = cache.shape
    return jnp.transpose(cache, (0, 2, 3, 1)).reshape(b, kvh * hd, n)


def _cache_keys_major(cache_t):
    b, kvd, n = cache_t.shape
    return jnp.transpose(cache_t.reshape(b, B_KV_HEADS, kvd // B_KV_HEADS, n), (0, 3, 1, 2))


def kernel(x_prompt, x_sample, state_hgrn, cache_k_win, cache_v_win, w_a_in, a_lb, a_gnorm, w_a_out,
           g_mix, g_mlp, g_kv, w_kv, w_b_q, b_sink, w_b_out, rel_bias, w_up, w_down, g_final):
    bsz, seq, d = x_prompt.shape
    n_dec = x_sample.shape[0]
    assert x_sample.shape[1] == 1, "the sample group decodes one token per sequence"
    n_a = w_a_in.shape[0]
    n_b = w_b_q.shape[0]
    assert n_a == 1 and n_b == 1, "depth-2 trunk: one HGRN2 layer, then one attention layer"
    n_buf = cache_k_win.shape[1]
    assert n_buf == WINDOW and seq % WINDOW == 0
    kvd = B_KV_HEADS * B_HEAD_DIM
    n_heads = d // B_HEAD_DIM

    row = lambda g: g.reshape(1, -1)
    sink = b_sink[0].reshape(1, n_heads)

    mlp_tile = min(MLP_TILE, bsz * seq)
    mlp_steps = bsz * seq // mlp_tile
    xs = x_sample.reshape(n_dec, d)

    w_in, w_ao, ft, q, v, g, band_bias, bias_row = _prologue(
        xs, w_a_in[0], w_a_out[0], a_lb, row(g_mix[0]), rel_bias, layer=0)
    later = (w_up, w_down, w_kv, w_b_q[0], w_b_out[0])
    x, s_prompt, cast = _hgrn_prompt(
        x_prompt, w_in, a_lb, row(g_mix[0]), row(a_gnorm[0]), w_ao, layer=0, tile=HGRN_TILE,
        make_rider=functools.partial(_bf16_cast_rider, later))
    w_up_b, w_down_b, w_kvb, w_q, w_bo = (c.reshape(w.shape) for c, w in zip(cast, later))
    x, (s_sample, o) = _mlp(
        x.reshape(bsz * seq, d), row(g_mlp[0]), w_up_b, w_down_b, 0, None, tile=mlp_tile,
        rider=_hgrn_sample_state_rider(ft, q, v, state_hgrn[0], mlp_steps))
    xs = _hgrn_sample_out(xs, o.reshape(n_dec, d), g, row(a_gnorm[0]), w_ao)
    xs, _ = _mlp(xs, row(g_mlp[0]), w_up_b, w_down_b, 0, None, tile=n_dec)

    kvt_new, q = _swa_sample_proj(xs, row(g_kv), w_kvb.T, row(g_mix[1]), w_q)
    x, kt_prompt, vt_prompt = _swa_prompt(x.reshape(bsz, seq, d), row(g_kv), w_kvb,
                                          row(g_mix[1]), w_q.T, sink, band_bias, w_bo,
                                          tile=SWA_TILE)
    y_prompt, (kt_sample, vt_sample, a) = _mlp(
        x.reshape(bsz * seq, d), row(g_mlp[1]), w_up_b, w_down_b, 1, row(g_final),
        tile=mlp_tile,
        rider=_swa_sample_attend_rider(
            q.reshape(n_dec * n_heads, B_HEAD_DIM), kvt_new,
            _cache_keys_minor(cache_k_win), _cache_keys_minor(cache_v_win),
            bias_row, sink.reshape(n_heads, 1), mlp_steps))
    y_prompt = y_prompt.reshape(bsz, seq, d)
    xs = _swa_sample_out(xs, a.reshape(n_dec, d), w_bo)
    y_sample, _ = _mlp(xs, row(g_mlp[1]), w_up_b, w_down_b, 1, row(g_final), tile=n_dec)
    y_sample = y_sample.reshape(n_dec, 1, d)

    return (y_prompt, y_sample, s_prompt[None], s_sample[None],
            _cache_keys_major(kt_prompt), _cache_keys_major(vt_prompt),
            _cache_keys_major(kt_sample), _cache_keys_major(vt_sample))
```
